```python
import jax
import jax.numpy as jnp
from jax import lax
import numpy as np

D_MODEL = 1024
BATCH = 32
SEQ = 2048
DEPTH = 1

GRID_W = 64
CTX_LEN = 256
CTX_CHUNK = 64
N_MOD = 6
EPS = 1e-6

HG_HEADS = 8
HG_HEAD_K = 128
HG_HEAD_V = D_MODEL // HG_HEADS
HG_KEY = HG_HEADS * HG_HEAD_K
HG_VAL = HG_HEADS * HG_HEAD_V

GLA_HEADS = 4
GLA_KEY = D_MODEL // 2
GLA_VAL = D_MODEL
GLA_HEAD_K = GLA_KEY // GLA_HEADS
GLA_HEAD_V = GLA_VAL // GLA_HEADS
GLA_GATE_RANK = 16
GLA_GATE_NORMALIZER = 16.0

IN_SIZES = (HG_KEY, HG_KEY, HG_KEY, HG_VAL, HG_VAL,
            GLA_KEY, GLA_KEY, GLA_VAL, GLA_VAL, GLA_GATE_RANK, GLA_GATE_RANK,
            D_MODEL, D_MODEL)
D_IN = 3 * HG_KEY + 2 * HG_VAL + 2 * GLA_KEY + 2 * GLA_VAL + 2 * GLA_GATE_RANK + 2 * D_MODEL

PEER_HEADS = 8
PEER_N_KEYS = 128
PEER_EXPERTS = PEER_N_KEYS * PEER_N_KEYS
PEER_QUERY_DIM = 256
PEER_SUB_DIM = PEER_QUERY_DIM // 2
PEER_TOPK = 16
PEER_BLOCK = 128

kernel_name = "hybrid_hgrn2_gla_peer_dit_layer"


def rmsnorm(x, g):
    xf = x.astype(jnp.float32)
    xf = xf * lax.rsqrt(jnp.mean(xf * xf, axis=-1, keepdims=True) + EPS)
    return (xf * g.astype(jnp.float32)).astype(x.dtype)


def modulate(h, shift, scale):
    return h * (1.0 + scale) + shift


def to_heads(a, n_heads):
    bsz, t, _ = a.shape
    return a.reshape(bsz, t, n_heads, -1).transpose(0, 2, 1, 3)


def flip_t(a):
    return a[:, :, ::-1]


def hgrn_lower_bounds(logits):
    p = jax.nn.softmax(logits.astype(jnp.float32), axis=0)
    return jnp.cumsum(p, axis=0)[:DEPTH]


def hgrn_forget(z, lb):
    f = lb + (1.0 - lb) * jax.nn.sigmoid(z.astype(jnp.float32))
    return 1.0 - f, jnp.log(f)


def chunk_gla(q, k, v, log_g, s0, n_chunks):
    bsz, nh, t, _ = q.shape
    dv = v.shape[-1]
    chunk = t // n_chunks

    def blocks(a):
        return jnp.moveaxis(a.reshape(bsz, nh, n_chunks, chunk, a.shape[-1]), 2, 0)

    qc, kc, vc = blocks(q), blocks(k), blocks(v)
    b = jnp.cumsum(blocks(log_g).astype(jnp.float32), axis=3)
    b_last = b[:, :, :, chunk - 1:chunk]
    b_ref = b[:, :, :, chunk // 2:chunk // 2 + 1]
    scores = jnp.einsum("nbhck,nbhsk->nbhcs", qc * jnp.exp(b - b_ref), kc * jnp.exp(b_ref - b))
    causal_in_scan = jnp.tril(jnp.ones((chunk, chunk), dtype=bool))
    o_intra = jnp.einsum("nbhcs,nbhsv->nbhcv", jnp.where(causal_in_scan, scores, 0.0), vc)
    q_dec = qc * jnp.exp(b)
    k_dec = kc * jnp.exp(b_last - b)

    def step(state, inp):
        q_i, k_i, v_i, bl = inp
        o_inter = jnp.einsum("bhck,bhkv->bhcv", q_i, state)
        state = jnp.exp(bl)[:, :, 0, :, None] * state + jnp.einsum("bhsk,bhsv->bhkv", k_i, v_i)
        return state, o_inter

    _, o_inter = lax.scan(step, s0, (q_dec, k_dec, vc, b_last))
    o = o_intra + o_inter
    return jnp.moveaxis(o, 0, 2).reshape(bsz, nh, t, dv)


def final_state(k, v, log_g):
    b = jnp.cumsum(log_g.astype(jnp.float32), axis=2)
    return jnp.einsum("bhtk,bhtv->bhkv", k * jnp.exp(b[:, :, -1:] - b), v)


def mixer_inputs(h, w_in, lb, gk_w, gk_b):
    splits = [int(s) for s in np.cumsum(IN_SIZES)[:-1]]
    (hq, hf_fwd, hf_bwd, hi, hg, gq, gk, gv, gg, gr_fwd, gr_bwd, m_hg, m_gla) = jnp.split(
        h @ w_in, splits, axis=-1)
    hk_fwd, hlf_fwd = hgrn_forget(hf_fwd, lb[0])
    hk_bwd, hlf_bwd = hgrn_forget(hf_bwd, lb[1])
    glf_fwd = jax.nn.log_sigmoid((gr_fwd @ gk_w[0] + gk_b[0]).astype(jnp.float32)) / GLA_GATE_NORMALIZER
    glf_bwd = jax.nn.log_sigmoid((gr_bwd @ gk_w[1] + gk_b[1]).astype(jnp.float32)) / GLA_GATE_NORMALIZER
    gla_k = to_heads(gk, GLA_HEADS)
    return {
        "hg_q": to_heads(hq, HG_HEADS) * HG_HEAD_K ** -0.5,
        "hg_v": to_heads(hi, HG_HEADS),
        "hg_k": (to_heads(hk_fwd, HG_HEADS), to_heads(hk_bwd, HG_HEADS)),
        "hg_lg": (to_heads(hlf_fwd, HG_HEADS), to_heads(hlf_bwd, HG_HEADS)),
        "hg_gate": hg,
        "gla_q": to_heads(gq, GLA_HEADS) * GLA_HEAD_K ** -0.5,
        "gla_v": to_heads(gv, GLA_HEADS),
        "gla_k": (gla_k, gla_k),
        "gla_lg": (to_heads(glf_fwd, GLA_HEADS), to_heads(glf_bwd, GLA_HEADS)),
        "gla_gate": gg,
        "merge": (m_hg, m_gla),
    }


def context_states(t):
    def both(k_pair, v, lg_pair):
        return (final_state(k_pair[0], v, lg_pair[0]),
                final_state(flip_t(k_pair[1]), flip_t(v), flip_t(lg_pair[1])))
    return {"hg": both(t["hg_k"], t["hg_v"], t["hg_lg"]),
            "gla": both(t["gla_k"], t["gla_v"], t["gla_lg"])}


def zero_states(bsz):
    hg = jnp.zeros((bsz, HG_HEADS, HG_HEAD_K, HG_HEAD_V), jnp.float32)
    gla = jnp.zeros((bsz, GLA_HEADS, GLA_HEAD_K, GLA_HEAD_V), jnp.float32)
    return {"hg": (hg, hg), "gla": (gla, gla)}


def bidir(q, k_pair, v, lg_pair, s_pair, n_chunks):
    o_fwd = chunk_gla(q, k_pair[0], v, lg_pair[0], s_pair[0], n_chunks)
    o_bwd = chunk_gla(flip_t(q), flip_t(k_pair[1]), flip_t(v), flip_t(lg_pair[1]), s_pair[1], n_chunks)
    return o_fwd + flip_t(o_bwd)


def gated_head_norm(o, gate, g):
    bsz, nh, t, dv = o.shape
    on = rmsnorm(o.transpose(0, 2, 1, 3), g)
    y = on * jax.nn.silu(gate.astype(jnp.float32)).reshape(bsz, t, nh, dv)
    return y.reshape(bsz, t, nh * dv).astype(gate.dtype)


def mixer_output(t, states, n_chunks, hg_norm_g, gla_norm_g, w_br_hg, w_br_gla, w_o):
    o_hg = bidir(t["hg_q"], t["hg_k"], t["hg_v"], t["hg_lg"], states["hg"], n_chunks)
    o_gla = bidir(t["gla_q"], t["gla_k"], t["gla_v"], t["gla_lg"], states["gla"], n_chunks)
    y_hg = gated_head_norm(o_hg, t["hg_gate"], hg_norm_g) @ w_br_hg
    y_gla = gated_head_norm(o_gla, t["gla_gate"], gla_norm_g) @ w_br_gla
    m_hg, m_gla = t["merge"]
    y = jax.nn.sigmoid(m_hg) * y_hg + jax.nn.sigmoid(m_gla) * y_gla
    return y @ w_o


def peer(h, wq, k1, k2, u, v):
    bsz, t, d = h.shape
    tok = h.reshape(bsz * t, d)
    q = (tok @ wq).reshape(-1, PEER_HEADS, 2, PEER_SUB_DIM)
    s1 = jnp.einsum("nhd,kd->nhk", q[:, :, 0], k1).astype(jnp.float32)
    s2 = jnp.einsum("nhd,kd->nhk", q[:, :, 1], k2).astype(jnp.float32)
    v1, i1 = lax.top_k(s1, PEER_TOPK)
    v2, i2 = lax.top_k(s2, PEER_TOPK)
    cand = (v1[..., :, None] + v2[..., None, :]).reshape(-1, PEER_HEADS, PEER_TOPK * PEER_TOPK)
    top_s, top_c = lax.top_k(cand, PEER_TOPK)
    expert = (jnp.take_along_axis(i1, top_c // PEER_TOPK, axis=-1) * PEER_N_KEYS
              + jnp.take_along_axis(i2, top_c % PEER_TOPK, axis=-1))
    gate = jax.nn.softmax(top_s, axis=-1)
    n_sel = PEER_HEADS * PEER_TOPK

    def expert_block(args):
        xb, eb, gb = args
        act = jax.nn.gelu(jnp.einsum("pd,ped->pe", xb, jnp.take(u, eb, axis=0)), approximate=False)
        return jnp.einsum("pe,ped->pd", (gb * act).astype(v.dtype), jnp.take(v, eb, axis=0))

    out = lax.map(expert_block, (tok.reshape(-1, PEER_BLOCK, d),
                                 expert.reshape(-1, PEER_BLOCK, n_sel),
                                 gate.reshape(-1, PEER_BLOCK, n_sel)))
    return out.reshape(bsz, t, d).astype(h.dtype)


def setup_inputs(seed: int = 0) -> dict:
    key = jax.random.key(seed)
    ks = jax.random.split(key, 24)

    def nrm(k, shape, scale):
        return jax.random.normal(k, shape, jnp.float32) * scale

    def gain(k, shape):
        return 1.0 + 0.1 * jax.random.normal(k, shape, jnp.float32)

    return {
        "x": nrm(ks[0], (BATCH, SEQ, D_MODEL), 1.0),
        "c": nrm(ks[1], (BATCH, D_MODEL), 1.0),
        "ctx": nrm(ks[2], (BATCH, CTX_LEN, D_MODEL), 1.0),
        "c_ctx": nrm(ks[3], (D_MODEL,), 1.0),
        "ada_w": nrm(ks[4], (DEPTH, D_MODEL, N_MOD * D_MODEL), 0.5 * D_MODEL ** -0.5),
        "ada_b": nrm(ks[5], (DEPTH, N_MOD * D_MODEL), 0.02),
        "norm_mix_g": gain(ks[6], (DEPTH, D_MODEL)),
        "w_in": nrm(ks[7], (DEPTH, D_MODEL, D_IN), D_MODEL ** -0.5),
        "hgrn_lb_logits": nrm(ks[8], (DEPTH + 1, 2, HG_KEY), 0.5),
        "hgrn_norm_g": gain(ks[9], (DEPTH, HG_HEAD_V)),
        "gla_gk_w": nrm(ks[10], (DEPTH, 2, GLA_GATE_RANK, GLA_KEY), GLA_GATE_RANK ** -0.5),
        "gla_gk_b": nrm(ks[11], (DEPTH, 2, GLA_KEY), 0.1),
        "gla_norm_g": gain(ks[12], (DEPTH, GLA_HEAD_V)),
        "w_branch_hgrn": nrm(ks[13], (DEPTH, HG_VAL, D_MODEL), HG_VAL ** -0.5),
        "w_branch_gla": nrm(ks[14], (DEPTH, GLA_VAL, D_MODEL), GLA_VAL ** -0.5),
        "w_out": nrm(ks[15], (DEPTH, D_MODEL, D_MODEL), D_MODEL ** -0.5),
        "norm_ffn_g": gain(ks[16], (DEPTH, D_MODEL)),
        "peer_wq": nrm(ks[17], (DEPTH, D_MODEL, PEER_HEADS * PEER_QUERY_DIM), D_MODEL ** -0.5),
        "peer_k1": nrm(ks[18], (DEPTH, PEER_N_KEYS, PEER_SUB_DIM), PEER_SUB_DIM ** -0.5),
        "peer_k2": nrm(ks[19], (DEPTH, PEER_N_KEYS, PEER_SUB_DIM), PEER_SUB_DIM ** -0.5),
        "peer_u": nrm(ks[20], (DEPTH, PEER_EXPERTS, D_MODEL), D_MODEL ** -0.5),
        "peer_v": nrm(ks[21], (DEPTH, PEER_EXPERTS, D_MODEL), (PEER_HEADS * PEER_TOPK) ** -0.5),
        "final_g": gain(ks[22], (D_MODEL,)),
    }


def reference(x, c, ctx, c_ctx, ada_w, ada_b, norm_mix_g, w_in, hgrn_lb_logits, hgrn_norm_g,
              gla_gk_w, gla_gk_b, gla_norm_g, w_branch_hgrn, w_branch_gla, w_out, norm_ffn_g,
              peer_wq, peer_k1, peer_k2, peer_u, peer_v, final_g):
    bsz, seq, _ = x.shape
    rows = seq // GRID_W
    ctx_chunks = ctx.shape[1] // CTX_CHUNK
    lower = hgrn_lower_bounds(hgrn_lb_logits)
    silu_c = jax.nn.silu(c)
    silu_cc = jax.nn.silu(c_ctx)[None]
    xc = ctx
    for l in range(DEPTH):
        mod = jnp.split((silu_c @ ada_w[l] + ada_b[l])[:, None, :], N_MOD, axis=-1)
        mod_c = jnp.split((silu_cc @ ada_w[l] + ada_b[l])[:, None, :], N_MOD, axis=-1)
        mix_w = (hgrn_norm_g[l], gla_norm_g[l], w_branch_hgrn[l], w_branch_gla[l], w_out[l])
        peer_w = (peer_wq[l], peer_k1[l], peer_k2[l], peer_u[l], peer_v[l])
        t_ctx = mixer_inputs(modulate(rmsnorm(xc, norm_mix_g[l]), mod_c[0], mod_c[1]),
                             w_in[l], lower[l], gla_gk_w[l], gla_gk_b[l])
        t_lat = mixer_inputs(modulate(rmsnorm(x, norm_mix_g[l]), mod[0], mod[1]),
                             w_in[l], lower[l], gla_gk_w[l], gla_gk_b[l])
        x = x + mod[2] * mixer_output(t_lat, context_states(t_ctx), rows, *mix_w)
        x = x + mod[5] * peer(modulate(rmsnorm(x, norm_ffn_g[l]), mod[3], mod[4]), *peer_w)
        if l < DEPTH - 1:
            xc = xc + mod_c[2] * mixer_output(t_ctx, zero_states(bsz), ctx_chunks, *mix_w)
            xc = xc + mod_c[5] * peer(modulate(rmsnorm(xc, norm_ffn_g[l]), mod_c[3], mod_c[4]), *peer_w)
    return rmsnorm(x, final_g)
```

```python
import functools

import jax
import jax.numpy as jnp
import numpy as np
from jax import lax
from jax.experimental import pallas as pl
from jax.experimental.pallas import tpu as pltpu

D_MODEL = 1024
GRID_W = 64
N_MOD = 6
EPS = 1e-6

HG_HEADS = 8
HG_HEAD_K = 128
HG_KEY = HG_HEADS * HG_HEAD_K
HG_VAL = D_MODEL
GLA_HEADS = 4
GLA_KEY = D_MODEL // 2
GLA_VAL = D_MODEL
GLA_HEAD_K = GLA_KEY // GLA_HEADS
GLA_GATE_RANK = 16
GLA_GATE_NORMALIZER = 16.0
IN_SIZES = (HG_KEY, HG_KEY, HG_KEY, HG_VAL, HG_VAL,
            GLA_KEY, GLA_KEY, GLA_VAL, GLA_VAL, GLA_GATE_RANK, GLA_GATE_RANK,
            D_MODEL, D_MODEL)

PEER_HEADS = 8
PEER_N_KEYS = 128
PEER_SUB_DIM = 128
PEER_TOPK = 16
N_SEL = PEER_HEADS * PEER_TOPK

LANES = 128
HALF = D_MODEL // 2
ROWS_PER_EXPERT = HALF // LANES
TILE_STRIDE = N_SEL + 8
LHS_ROWS = 16
PEER_TOKENS_PER_STEP = 64
PEER_UNROLL = 2
PEER_VMEM_LIMIT = 52 * 1024 * 1024


def _pack_table(w):
    e = w.shape[0]
    bits = lax.bitcast_convert_type(w.astype(jnp.bfloat16), jnp.uint16).astype(jnp.uint32)
    word = bits[:, :HALF] | (bits[:, HALF:] << 16)
    return lax.bitcast_convert_type(word, jnp.int32).reshape(e * ROWS_PER_EXPERT, LANES)


def _split_bf16(a):
    hi = a.astype(jnp.bfloat16)
    lo = (a - hi.astype(jnp.float32)).astype(jnp.bfloat16)
    return hi, lo


def _load_table_once(tab_hbm, tab, sem):
    @pl.when(pl.program_id(0) == 0)
    def _():
        cp = pltpu.make_async_copy(tab_hbm, tab, sem)
        cp.start()
        cp.wait()


def _gather_token(idx_ref, t, tab, tile):
    for e in range(N_SEL):
        row = pl.multiple_of(idx_ref[t, e], ROWS_PER_EXPERT)
        tile[pl.ds(e, ROWS_PER_EXPERT, stride=TILE_STRIDE), :] = tab[pl.ds(row, ROWS_PER_EXPERT), :]


def _tile_chunk(tile, s):
    return pltpu.bitcast(tile[s * TILE_STRIDE:s * TILE_STRIDE + N_SEL, :], jnp.bfloat16)


def _peer_act_body(idx_ref, lhs_ref, tab_hbm, o_ref, tab, sem, *tiles):
    _load_table_once(tab_hbm, tab, sem)
    lane = lax.broadcasted_iota(jnp.int32, (1, 2 * N_SEL), 1)
    even = (lane & 1) == 0

    def step(k, carry):
        for u in range(PEER_UNROLL):
            t = k * PEER_UNROLL + u
            tile = tiles[u]
            _gather_token(idx_ref, t, tab, tile)
            lhs = lhs_ref[t]
            acc = jnp.zeros((LHS_ROWS, 2 * N_SEL), jnp.float32)
            for s in range(ROWS_PER_EXPERT):
                acc = acc + lax.dot_general(lhs[:, s * LANES:(s + 1) * LANES], _tile_chunk(tile, s),
                                            (((1,), (1,)), ((), ())), preferred_element_type=jnp.float32)
            low = acc[0:1] + acc[1:2]
            high = acc[2:3] + acc[3:4]
            o_ref[pl.ds(t, 1), :] = jnp.where(even, low, high)
        return carry

    lax.fori_loop(0, PEER_TOKENS_PER_STEP // PEER_UNROLL, step, 0)


def _peer_out_body(idx_ref, lhs_ref, tab_hbm, o_ref, tab, sem, *tiles):
    _load_table_once(tab_hbm, tab, sem)

    def step(k, carry):
        for u in range(PEER_UNROLL):
            t = k * PEER_UNROLL + u
            tile = tiles[u]
            _gather_token(idx_ref, t, tab, tile)
            lhs = lhs_ref[t]
            res = [jnp.dot(lhs, _tile_chunk(tile, s), preferred_element_type=jnp.float32)
                   for s in range(ROWS_PER_EXPERT)]
            o_ref[pl.ds(t, 1), :] = jnp.concatenate([r[0:1] + r[1:2] for r in res] + [r[2:3] + r[3:4] for r in res], axis=1)
        return carry

    lax.fori_loop(0, PEER_TOKENS_PER_STEP // PEER_UNROLL, step, 0)


def _peer_call(body, idx, lhs, table, out_width, name):
    n = idx.shape[0]
    tb = PEER_TOKENS_PER_STEP
    return pl.pallas_call(
        body,
        grid=(n // tb,),
        in_specs=[
            pl.BlockSpec((tb, N_SEL), lambda i: (i, 0), memory_space=pltpu.SMEM),
            pl.BlockSpec((tb,) + lhs.shape[1:], lambda i: (i, 0, 0)),
            pl.BlockSpec(memory_space=pl.ANY),
        ],
        out_specs=pl.BlockSpec((tb, out_width), lambda i: (i, 0)),
        out_shape=jax.ShapeDtypeStruct((n, out_width), jnp.float32),
        scratch_shapes=[pltpu.VMEM(table.shape, jnp.int32), pltpu.SemaphoreType.DMA(())]
        + [pltpu.VMEM((ROWS_PER_EXPERT * TILE_STRIDE, LANES), jnp.int32) for _ in range(PEER_UNROLL)],
        compiler_params=pltpu.CompilerParams(dimension_semantics=("arbitrary",),
                                             vmem_limit_bytes=PEER_VMEM_LIMIT),
        name=name,
    )(idx, lhs, table)


def _peer_experts(tok, expert, gate, u, v):
    n = tok.shape[0]
    idx = expert.astype(jnp.int32) * ROWS_PER_EXPERT
    zeros = jnp.zeros((n, LHS_ROWS - 4, HALF), jnp.bfloat16)
    hi, lo = _split_bf16(tok)
    lhs_u = jnp.concatenate([jnp.stack([hi[:, :HALF], lo[:, :HALF], hi[:, HALF:], lo[:, HALF:]], axis=1), zeros], axis=1)
    part = _peer_call(_peer_act_body, idx, lhs_u, _pack_table(u), 2 * N_SEL, "peer_act")
    act = part[:, 0::2] + part[:, 1::2]
    coef = gate * jax.nn.gelu(act, approximate=False)
    chi, clo = _split_bf16(coef)
    z = jnp.zeros_like(chi)
    inter = lambda a, b: jnp.stack([a, b], axis=-1).reshape(n, 2 * N_SEL)
    lhs_v = jnp.concatenate([jnp.stack([inter(chi, z), inter(clo, z), inter(z, chi), inter(z, clo)], axis=1),
                             jnp.zeros((n, LHS_ROWS - 4, 2 * N_SEL), jnp.bfloat16)], axis=1)
    return _peer_call(_peer_out_body, idx, lhs_v, _pack_table(v), D_MODEL, "peer_out")


def _rmsnorm(x, g):
    return x * lax.rsqrt(jnp.mean(x * x, axis=-1, keepdims=True) + EPS) * g


def _to_heads(a, n_heads):
    bsz, t, _ = a.shape
    return a.reshape(bsz, t, n_heads, -1).transpose(0, 2, 1, 3)


def _chunk_gla(q, k, v, log_g, s0, n_chunks):
    bsz, nh, t, _ = q.shape
    dv = v.shape[-1]
    chunk = t // n_chunks

    def blocks(a):
        return jnp.moveaxis(a.reshape(bsz, nh, n_chunks, chunk, a.shape[-1]), 2, 0)

    qc, kc, vc = blocks(q), blocks(k), blocks(v)
    b = jnp.cumsum(blocks(log_g), axis=3)
    b_last = b[:, :, :, chunk - 1:chunk]
    b_ref = b[:, :, :, chunk // 2:chunk // 2 + 1]
    scores = jnp.einsum("nbhck,nbhsk->nbhcs", qc * jnp.exp(b - b_ref), kc * jnp.exp(b_ref - b))
    causal = jnp.tril(jnp.ones((chunk, chunk), dtype=bool))
    o_intra = jnp.einsum("nbhcs,nbhsv->nbhcv", jnp.where(causal, scores, 0.0), vc)
    q_dec = qc * jnp.exp(b)
    k_dec = kc * jnp.exp(b_last - b)

    def step(state, inp):
        q_i, k_i, v_i, bl = inp
        o_inter = jnp.einsum("bhck,bhkv->bhcv", q_i, state)
        state = jnp.exp(bl)[:, :, 0, :, None] * state + jnp.einsum("bhsk,bhsv->bhkv", k_i, v_i)
        return state, o_inter

    _, o_inter = lax.scan(step, s0, (q_dec, k_dec, vc, b_last))
    return jnp.moveaxis(o_intra + o_inter, 0, 2).reshape(bsz, nh, t, dv)


def _final_state(k, v, log_g):
    b = jnp.cumsum(log_g, axis=2)
    return jnp.einsum("bhtk,bhtv->bhkv", k * jnp.exp(b[:, :, -1:] - b), v)


def _mixer_inputs(h, w_in, lb, gk_w, gk_b):
    splits = [int(s) for s in np.cumsum(IN_SIZES)[:-1]]
    (hq, hf_fwd, hf_bwd, hi, hg, gq, gk, gv, gg, gr_fwd, gr_bwd, m_hg, m_gla) = jnp.split(h @ w_in, splits, axis=-1)

    def forget(z, lower):
        f = lower + (1.0 - lower) * jax.nn.sigmoid(z)
        return 1.0 - f, jnp.log(f)

    hk_fwd, hlf_fwd = forget(hf_fwd, lb[0])
    hk_bwd, hlf_bwd = forget(hf_bwd, lb[1])
    glf_fwd = jax.nn.log_sigmoid(gr_fwd @ gk_w[0] + gk_b[0]) / GLA_GATE_NORMALIZER
    glf_bwd = jax.nn.log_sigmoid(gr_bwd @ gk_w[1] + gk_b[1]) / GLA_GATE_NORMALIZER
    gla_k = _to_heads(gk, GLA_HEADS)
    return {
        "hg_q": _to_heads(hq, HG_HEADS) * HG_HEAD_K ** -0.5,
        "hg_v": _to_heads(hi, HG_HEADS),
        "hg_k": (_to_heads(hk_fwd, HG_HEADS), _to_heads(hk_bwd, HG_HEADS)),
        "hg_lg": (_to_heads(hlf_fwd, HG_HEADS), _to_heads(hlf_bwd, HG_HEADS)),
        "hg_gate": hg,
        "gla_q": _to_heads(gq, GLA_HEADS) * GLA_HEAD_K ** -0.5,
        "gla_v": _to_heads(gv, GLA_HEADS),
        "gla_k": (gla_k, gla_k),
        "gla_lg": (_to_heads(glf_fwd, GLA_HEADS), _to_heads(glf_bwd, GLA_HEADS)),
        "gla_gate": gg,
        "merge": (m_hg, m_gla),
    }


def _flip(a):
    return a[:, :, ::-1]


def _context_states(t):
    def both(k_pair, v, lg_pair):
        return (_final_state(k_pair[0], v, lg_pair[0]),
                _final_state(_flip(k_pair[1]), _flip(v), _flip(lg_pair[1])))
    return {"hg": both(t["hg_k"], t["hg_v"], t["hg_lg"]),
            "gla": both(t["gla_k"], t["gla_v"], t["gla_lg"])}


def _bidir(q, k_pair, v, lg_pair, s_pair, n_chunks):
    o_fwd = _chunk_gla(q, k_pair[0], v, lg_pair[0], s_pair[0], n_chunks)
    o_bwd = _chunk_gla(_flip(q), _flip(k_pair[1]), _flip(v), _flip(lg_pair[1]), s_pair[1], n_chunks)
    return o_fwd + _flip(o_bwd)


def _gated_head_norm(o, gate, g):
    bsz, nh, t, dv = o.shape
    on = _rmsnorm(o.transpose(0, 2, 1, 3), g)
    y = on * jax.nn.silu(gate).reshape(bsz, t, nh, dv)
    return y.reshape(bsz, t, nh * dv)


def _mixer_output(t, states, n_chunks, hg_norm_g, gla_norm_g, w_br_hg, w_br_gla, w_o):
    o_hg = _bidir(t["hg_q"], t["hg_k"], t["hg_v"], t["hg_lg"], states["hg"], n_chunks)
    o_gla = _bidir(t["gla_q"], t["gla_k"], t["gla_v"], t["gla_lg"], states["gla"], n_chunks)
    y_hg = _gated_head_norm(o_hg, t["hg_gate"], hg_norm_g) @ w_br_hg
    y_gla = _gated_head_norm(o_gla, t["gla_gate"], gla_norm_g) @ w_br_gla
    m_hg, m_gla = t["merge"]
    y = jax.nn.sigmoid(m_hg) * y_hg + jax.nn.sigmoid(m_gla) * y_gla
    return y @ w_o


def _peer_route(tok, wq, k1, k2):
    q = (tok @ wq).reshape(-1, PEER_HEADS, 2, PEER_SUB_DIM)
    s1 = jnp.einsum("nhd,kd->nhk", q[:, :, 0], k1)
    s2 = jnp.einsum("nhd,kd->nhk", q[:, :, 1], k2)
    v1, i1 = lax.top_k(s1, PEER_TOPK)
    v2, i2 = lax.top_k(s2, PEER_TOPK)
    cand = (v1[..., :, None] + v2[..., None, :]).reshape(-1, PEER_HEADS, PEER_TOPK * PEER_TOPK)
    top_s, top_c = lax.top_k(cand, PEER_TOPK)
    expert = (jnp.take_along_axis(i1, top_c // PEER_TOPK, axis=-1) * PEER_N_KEYS
              + jnp.take_along_axis(i2, top_c % PEER_TOPK, axis=-1))
    gate = jax.nn.softmax(top_s, axis=-1)
    return expert.reshape(-1, N_SEL), gate.reshape(-1, N_SEL)


def kernel(x, c, ctx, c_ctx, ada_w, ada_b, norm_mix_g, w_in, hgrn_lb_logits, hgrn_norm_g, gla_gk_w, gla_gk_b, gla_norm_g, w_branch_hgrn, w_branch_gla, w_out, norm_ffn_g, peer_wq, peer_k1, peer_k2, peer_u, peer_v, final_g):
    bsz, seq, d = x.shape
    rows = seq // GRID_W
    depth = ada_w.shape[0]
    assert depth == 1
    l = 0
    lower = jnp.cumsum(jax.nn.softmax(hgrn_lb_logits, axis=0), axis=0)[:depth]
    mod = jnp.split((jax.nn.silu(c) @ ada_w[l] + ada_b[l])[:, None, :], N_MOD, axis=-1)
    mod_c = jnp.split((jax.nn.silu(c_ctx)[None] @ ada_w[l] + ada_b[l])[:, None, :], N_MOD, axis=-1)
    mix_w = (hgrn_norm_g[l], gla_norm_g[l], w_branch_hgrn[l], w_branch_gla[l], w_out[l])
    t_ctx = _mixer_inputs(_rmsnorm(ctx, norm_mix_g[l]) * (1.0 + mod_c[1]) + mod_c[0],
                          w_in[l], lower[l], gla_gk_w[l], gla_gk_b[l])
    t_lat = _mixer_inputs(_rmsnorm(x, norm_mix_g[l]) * (1.0 + mod[1]) + mod[0],
                          w_in[l], lower[l], gla_gk_w[l], gla_gk_b[l])
    x = x + mod[2] * _mixer_output(t_lat, _context_states(t_ctx), rows, *mix_w)
    h = (_rmsnorm(x, norm_ffn_g[l]) * (1.0 + mod[4]) + mod[3]).reshape(bsz * seq, d)
    expert, gate = _peer_route(h, peer_wq[l], peer_k1[l], peer_k2[l])
    x = x + mod[5] * _peer_experts(h, expert, gate, peer_u[l], peer_v[l]).reshape(bsz, seq, d)
    return _rmsnorm(x, final_g)
```

```python
import functools

import jax
import jax.numpy as jnp
import numpy as np
from jax import lax
from jax.experimental import pallas as pl
from jax.experimental.pallas import tpu as pltpu

F32 = jnp.float32
BF16 = jnp.bfloat16

D_MODEL = 1024
GRID_W = 64
CTX_CHUNK = 64
N_MOD = 6
EPS = 1e-6

HG_HEADS = 8
HEAD_K = 128
HG_KEY = HG_HEADS * HEAD_K
HG_VAL = D_MODEL
HG_HEAD_V = HG_VAL // HG_HEADS
GLA_HEADS = 4
GLA_KEY = D_MODEL // 2
GLA_VAL = D_MODEL
GLA_HEAD_V = GLA_VAL // GLA_HEADS
GLA_GATE_RANK = 16
GLA_GATE_NORMALIZER = 16.0

C_HQ, C_ZF, C_ZB, C_HI, C_HGATE = 0, 1024, 2048, 3072, 4096
C_GQ, C_GK, C_GV, C_GG = 5120, 5632, 6144, 7168
C_RANK = 8192
C_MHG = C_RANK + 2 * GLA_GATE_RANK
C_MGLA = C_MHG + D_MODEL
D_IN = C_MGLA + D_MODEL

PEER_HEADS = 8
PEER_N_KEYS = 128
PEER_SUB_DIM = 128
PEER_TOPK = 16
N_SEL = PEER_HEADS * PEER_TOPK

LANES = 128
SUBLANES = 8
HALF = D_MODEL // 2
ROWS_PER_EXPERT = HALF // LANES
TILE_STRIDE = N_SEL + 8
PEER_TOKENS_PER_STEP = 64
PEER_UNROLL = 2
VMEM_LIMIT = 56 * 1024 * 1024

ROW_TILE = 256
INPROJ_ROW_TILE = 128
ROUTE_TILE = 256


def _resident(shape):
    return pl.BlockSpec(shape, lambda *_: (0,) * len(shape), pipeline_mode=pl.Buffered(1))


def _params(n_axes):
    return pltpu.CompilerParams(dimension_semantics=("arbitrary",) * n_axes, vmem_limit_bytes=VMEM_LIMIT)


def _split3(a):
    p1 = a.astype(BF16)
    r1 = a - p1.astype(F32)
    p2 = r1.astype(BF16)
    p3 = (r1 - p2.astype(F32)).astype(BF16)
    return p1, p2, p3


def _dot(a, b):
    return jnp.dot(a, b, preferred_element_type=F32)


def _dot_exact_rhs(a, rhs_bf16):
    p1, p2, p3 = _split3(a)
    return _dot(p1, rhs_bf16) + _dot(p2, rhs_bf16) + _dot(p3, rhs_bf16)


def _rms(x):
    return x * lax.rsqrt(jnp.mean(x * x, axis=-1, keepdims=True) + EPS)


def _gelu(x):
    return 0.5 * x * (1.0 + lax.erf(x * np.float32(2.0 ** -0.5)))


def _sigmoid(x):
    return 1.0 / (1.0 + jnp.exp(-x))


def _log_sigmoid(x):
    return jnp.minimum(x, 0.0) - jnp.log1p(jnp.exp(-jnp.abs(x)))


def _adaln_body(c_ref, w_ref, b_ref, o_ref):
    c = c_ref[...]
    s = (c * _sigmoid(c)).astype(BF16)
    o_ref[...] = _dot(s, w_ref[...].astype(BF16)) + b_ref[...]


def _adaln(cvec, w, b):
    rows, d = cvec.shape
    n_out = w.shape[1]
    return pl.pallas_call(
        _adaln_body,
        grid=(n_out // d,),
        in_specs=[pl.BlockSpec((rows, d), lambda j: (0, 0)),
                  pl.BlockSpec((d, d), lambda j: (0, j)),
                  pl.BlockSpec((1, d), lambda j: (0, j))],
        out_specs=pl.BlockSpec((rows, d), lambda j: (0, j)),
        out_shape=jax.ShapeDtypeStruct((rows, n_out), F32),
        compiler_params=_params(1),
        name="adaln",
    )(cvec, w, b.reshape(1, n_out))


def _inproj_body(x_ref, shift_ref, scale_ref, g_ref, w_ref, wrank_ref, brank_ref,
                 hq_ref, zf_ref, zb_ref, hi_ref, hgate_ref, gq_ref, gk_ref, gv_ref, gg_ref,
                 glf_ref, glb_ref, mhg_ref, mgla_ref):
    h = _rms(x_ref[0]) * g_ref[...] * (1.0 + scale_ref[0]) + shift_ref[0]
    hb = h.astype(BF16)

    def cols(start, width):
        return _dot(hb, w_ref[:, start:start + width])

    hq_ref[0] = cols(C_HQ, HG_KEY) * np.float32(HEAD_K ** -0.5)
    zf_ref[0] = cols(C_ZF, HG_KEY)
    zb_ref[0] = cols(C_ZB, HG_KEY)
    hi_ref[0] = cols(C_HI, HG_VAL)
    hgate_ref[0] = cols(C_HGATE, HG_VAL)
    gq_ref[0] = cols(C_GQ, GLA_KEY) * np.float32(HEAD_K ** -0.5)
    gk_ref[0] = cols(C_GK, GLA_KEY)
    gv_ref[0] = cols(C_GV, GLA_VAL)
    gg_ref[0] = cols(C_GG, GLA_VAL)
    mhg_ref[0] = cols(C_MHG, D_MODEL)
    mgla_ref[0] = cols(C_MGLA, D_MODEL)
    rank = cols(C_RANK, 2 * GLA_GATE_RANK).astype(BF16)
    pre = _dot(rank, wrank_ref[...]) + brank_ref[...]
    lg = _log_sigmoid(pre) / GLA_GATE_NORMALIZER
    glf_ref[0] = lg[:, :GLA_KEY]
    glb_ref[0] = lg[:, GLA_KEY:]


def _inproj(x, shift, scale, g, w_bf16, wrank, brank):
    bsz, t, d = x.shape
    tm = INPROJ_ROW_TILE
    widths = (HG_KEY, HG_KEY, HG_KEY, HG_VAL, HG_VAL, GLA_KEY, GLA_KEY, GLA_VAL, GLA_VAL,
              GLA_KEY, GLA_KEY, D_MODEL, D_MODEL)
    row = lambda w: pl.BlockSpec((1, tm, w), lambda b, i: (b, i, 0))
    per_batch = pl.BlockSpec((1, 1, d), lambda b, i: (b, 0, 0))
    return pl.pallas_call(
        _inproj_body,
        grid=(bsz, t // tm),
        in_specs=[row(d), per_batch, per_batch, _resident((1, d)), _resident(w_bf16.shape),
                  _resident(wrank.shape), _resident(brank.shape)],
        out_specs=[row(w) for w in widths],
        out_shape=[jax.ShapeDtypeStruct((bsz, t, w), F32) for w in widths],
        compiler_params=_params(2),
        name="inproj",
    )(x, shift, scale, g, w_bf16, wrank, brank)


def _chunk_decays(lg, tri, ref_row, last_row):
    b = _dot_exact_rhs_left(tri, lg)
    b_ref = b[ref_row:ref_row + 1]
    b_last = b[last_row:last_row + 1]
    return b, b_ref, b_last


def _dot_exact_rhs_left(lhs_bf16, a):
    p1, p2, p3 = _split3(a)
    return _dot(lhs_bf16, p1) + _dot(lhs_bf16, p2) + _dot(lhs_bf16, p3)


def _chunk_step(q, k, lg, v, state, tri, mask, ref_row, last_row, want_out):
    b, b_ref, b_last = _chunk_decays(lg, tri, ref_row, last_row)
    vb = v.astype(BF16)
    o = None
    if want_out:
        qd = (q * jnp.exp(b - b_ref)).astype(BF16)
        kd = (k * jnp.exp(b_ref - b)).astype(BF16)
        scores = lax.dot_general(qd, kd, (((1,), (1,)), ((), ())), preferred_element_type=F32)
        scores = jnp.where(mask, scores, 0.0).astype(BF16)
        qs = (q * jnp.exp(b)).astype(BF16)
        o = _dot(scores, vb) + lax.dot_general(qs, state.astype(BF16), (((1,), (1,)), ((), ())),
                                               preferred_element_type=F32)
    kl = (k * jnp.exp(b_last - b)).astype(BF16)
    new_state = state * jnp.exp(b_last) + lax.dot_general(vb, kl, (((0,), (0,)), ((), ())),
                                                          preferred_element_type=F32)
    return o, new_state


def _scan_core(load_fwd, load_bwd, load_ctx_fwd, load_ctx_bwd, gate_ref, g_ref, y_ref,
               of_scr, ob_scr, sf_scr, sb_scr, n_chunks, n_ctx_chunks, dv):
    c = GRID_W
    ri = lax.broadcasted_iota(jnp.int32, (c, c), 0)
    ci = lax.broadcasted_iota(jnp.int32, (c, c), 1)
    lower = ci <= ri
    upper = ci >= ri
    tri_f = lower.astype(BF16)
    tri_b = upper.astype(BF16)
    fwd = dict(tri=tri_f, mask=lower, ref_row=c // 2, last_row=c - 1)
    bwd = dict(tri=tri_b, mask=upper, ref_row=c - 1 - c // 2, last_row=0)

    sf = jnp.zeros((dv, HEAD_K), F32)
    sb = jnp.zeros((dv, HEAD_K), F32)
    for i in range(n_ctx_chunks):
        q, k, lg, v = load_ctx_fwd(i * CTX_CHUNK)
        _, sf = _chunk_step(q, k, lg, v, sf, want_out=False, **fwd)
        q, k, lg, v = load_ctx_bwd((n_ctx_chunks - 1 - i) * CTX_CHUNK)
        _, sb = _chunk_step(q, k, lg, v, sb, want_out=False, **bwd)
    sf_scr[...] = sf
    sb_scr[...] = sb

    def step(i, carry):
        rf = pl.multiple_of(i * c, c)
        q, k, lg, v = load_fwd(rf)
        o, s = _chunk_step(q, k, lg, v, sf_scr[...], want_out=True, **fwd)
        of_scr[pl.ds(rf, c), :] = o
        sf_scr[...] = s
        rb = pl.multiple_of((n_chunks - 1 - i) * c, c)
        q, k, lg, v = load_bwd(rb)
        o, s = _chunk_step(q, k, lg, v, sb_scr[...], want_out=True, **bwd)
        ob_scr[pl.ds(rb, c), :] = o
        sb_scr[...] = s
        return carry

    lax.fori_loop(0, n_chunks, step, 0)

    o = of_scr[...] + ob_scr[...]
    gate = gate_ref[0]
    y_ref[0] = _rms(o) * g_ref[...] * (gate * _sigmoid(gate))


def _hgrn_scan_body(q_ref, zf_ref, zb_ref, v_ref, gate_ref, czf_ref, czb_ref, cv_ref, lb_ref, g_ref,
                    y_ref, of_scr, ob_scr, sf_scr, sb_scr, *, n_chunks, n_ctx_chunks):
    def forget(z, lower):
        f = lower + (1.0 - lower) * _sigmoid(z)
        return 1.0 - f, jnp.log(f)

    def loader(q_r, z_r, v_r, row):
        lower = lb_ref[row:row + 1, :]

        def load(rows):
            sl = pl.ds(rows, GRID_W)
            k, lg = forget(z_r[0, sl, :], lower)
            q = q_r[0, sl, :] if q_r is not None else None
            return q, k, lg, v_r[0, sl, :]
        return load

    _scan_core(loader(q_ref, zf_ref, v_ref, 0), loader(q_ref, zb_ref, v_ref, 1),
               loader(None, czf_ref, cv_ref, 0), loader(None, czb_ref, cv_ref, 1),
               gate_ref, g_ref, y_ref, of_scr, ob_scr, sf_scr, sb_scr, n_chunks, n_ctx_chunks, HG_HEAD_V)


def _gla_scan_body(q_ref, k_ref, lf_ref, lbk_ref, v_ref, gate_ref, ck_ref, clf_ref, clb_ref, cv_ref, g_ref,
                   y_ref, of_scr, ob_scr, sf_scr, sb_scr, *, n_chunks, n_ctx_chunks):
    def loader(q_r, k_r, l_r, v_r):
        def load(rows):
            sl = pl.ds(rows, GRID_W)
            q = q_r[0, sl, :] if q_r is not None else None
            return q, k_r[0, sl, :], l_r[0, sl, :], v_r[0, sl, :]
        return load

    _scan_core(loader(q_ref, k_ref, lf_ref, v_ref), loader(q_ref, k_ref, lbk_ref, v_ref),
               loader(None, ck_ref, clf_ref, cv_ref), loader(None, ck_ref, clb_ref, cv_ref),
               gate_ref, g_ref, y_ref, of_scr, ob_scr, sf_scr, sb_scr, n_chunks, n_ctx_chunks, GLA_HEAD_V)


def _scan_call(body, n_heads, dv, lat, ctx, small, bsz, t, t_ctx, name):
    head = lambda rows, w: pl.BlockSpec((1, rows, w), lambda b, h: (b, 0, h))
    in_specs = ([head(t, w) for _, w in lat] + [head(t_ctx, w) for _, w in ctx]
                + [pl.BlockSpec(bs, im) for _, bs, im in small])
    return pl.pallas_call(
        functools.partial(body, n_chunks=t // GRID_W, n_ctx_chunks=t_ctx // CTX_CHUNK),
        grid=(bsz, n_heads),
        in_specs=in_specs,
        out_specs=head(t, dv),
        out_shape=jax.ShapeDtypeStruct((bsz, t, n_heads * dv), F32),
        scratch_shapes=[pltpu.VMEM((t, dv), F32), pltpu.VMEM((t, dv), F32),
                        pltpu.VMEM((dv, HEAD_K), F32), pltpu.VMEM((dv, HEAD_K), F32)],
        compiler_params=_params(2),
        name=name,
    )(*[a for a, _ in lat], *[a for a, _ in ctx], *[a for a, _, _ in small])


def _mixer_out_body(yh_ref, yg_ref, mh_ref, mg_ref, x_ref, gate1_ref, shift2_ref, scale2_ref, g2_ref,
                    wbh_ref, wbg_ref, wo_ref, wq_ref, k1_ref, k2_ref,
                    x1_ref, h2_ref, st_ref):
    yh = _dot(yh_ref[0].astype(BF16), wbh_ref[...])
    yg = _dot(yg_ref[0].astype(BF16), wbg_ref[...])
    y = _sigmoid(mh_ref[0]) * yh + _sigmoid(mg_ref[0]) * yg
    x1 = x_ref[0] + gate1_ref[0] * _dot(y.astype(BF16), wo_ref[...])
    x1_ref[0] = x1
    h2 = _rms(x1) * g2_ref[...] * (1.0 + scale2_ref[0]) + shift2_ref[0]
    h2_ref[0] = h2
    q = _dot(h2.astype(BF16), wq_ref[...]).astype(BF16)
    for hh in range(2 * PEER_HEADS):
        keys = k1_ref if hh % 2 == 0 else k2_ref
        st_ref[hh] = lax.dot_general(keys[...], q[:, hh * PEER_SUB_DIM:(hh + 1) * PEER_SUB_DIM],
                                     (((1,), (1,)), ((), ())), preferred_element_type=F32)


def _mixer_out(y_hg, y_gla, m_hg, m_gla, x, gate1, shift2, scale2, g2, wbh, wbg, wo, wq, k1, k2):
    bsz, t, d = x.shape
    tm = ROW_TILE
    row = pl.BlockSpec((1, tm, d), lambda b, i: (b, i, 0))
    per_batch = pl.BlockSpec((1, 1, d), lambda b, i: (b, 0, 0))
    tiles_per_batch = t // tm
    n = bsz * t
    return pl.pallas_call(
        _mixer_out_body,
        grid=(bsz, tiles_per_batch),
        in_specs=[row, row, row, row, row, per_batch, per_batch, per_batch, _resident((1, d)),
                  _resident(wbh.shape), _resident(wbg.shape), _resident(wo.shape), _resident(wq.shape),
                  _resident(k1.shape), _resident(k2.shape)],
        out_specs=[row, row,
                   pl.BlockSpec((2 * PEER_HEADS, PEER_N_KEYS, tm), lambda b, i: (0, 0, b * tiles_per_batch + i))],
        out_shape=[jax.ShapeDtypeStruct((bsz, t, d), F32), jax.ShapeDtypeStruct((bsz, t, d), F32),
                   jax.ShapeDtypeStruct((2 * PEER_HEADS, PEER_N_KEYS, n), F32)],
        compiler_params=_params(2),
        name="mixer_out",
    )(y_hg, y_gla, m_hg, m_gla, x, gate1, shift2, scale2, g2, wbh, wbg, wo, wq, k1, k2)


_CAND_ROWS = PEER_TOPK + (PEER_TOPK - 1) * SUBLANES


def _candidate_tables():
    flat = np.full((_CAND_ROWS, 1), -1, np.int32)
    for a in range(PEER_TOPK):
        base = 0 if a == 0 else PEER_TOPK + (a - 1) * SUBLANES
        for b in range(PEER_TOPK if a == 0 else SUBLANES):
            if (a + 1) * (b + 1) <= PEER_TOPK:
                flat[base + b, 0] = a * PEER_TOPK + b
    return flat


def _extract_top(x, key, rounds):
    big = np.int32(2 ** 30)
    vals, keys = [], []
    for _ in range(rounds):
        m = jnp.max(x, axis=0, keepdims=True)
        sel = jnp.min(jnp.where(x == m, key, big), axis=0, keepdims=True)
        vals.append(m)
        keys.append(sel)
        x = jnp.where(key == sel, -jnp.inf, x)
    return vals, keys


def _pick_row(table, row_ids, sel):
    return jnp.sum(jnp.where(row_ids == sel, table, 0), axis=0, keepdims=True)


def _route_body(st_ref, flat_ref, idx_ref, gate_ref):
    tn = st_ref.shape[-1]
    key_ids = lax.broadcasted_iota(jnp.int32, (PEER_N_KEYS, tn), 0)
    rank_ids = lax.broadcasted_iota(jnp.int32, (PEER_TOPK, tn), 0)
    flat = jnp.broadcast_to(flat_ref[...], (_CAND_ROWS, tn))
    valid = flat >= 0
    experts, gates = [], []
    for h in range(PEER_HEADS):
        v1, i1 = _extract_top(st_ref[2 * h], key_ids, PEER_TOPK)
        v2, i2 = _extract_top(st_ref[2 * h + 1], key_ids, PEER_TOPK)
        v1c, i1c = jnp.concatenate(v1, axis=0), jnp.concatenate(i1, axis=0)
        v2c, i2c = jnp.concatenate(v2, axis=0), jnp.concatenate(i2, axis=0)
        cand = jnp.concatenate([v1[0] + v2c] + [v1[a] + v2c[:SUBLANES] for a in range(1, PEER_TOPK)], axis=0)
        cand = jnp.where(valid, cand, -jnp.inf)
        top_s, top_c = _extract_top(cand, flat, PEER_TOPK)
        m = top_s[0]
        ex = [jnp.exp(s - m) for s in top_s]
        denom = functools.reduce(lambda a, b: a + b, ex)
        for s_k, c_k in zip(ex, top_c):
            a = c_k >> 4
            b = c_k & (PEER_TOPK - 1)
            e = _pick_row(i1c, rank_ids, a) * PEER_N_KEYS + _pick_row(i2c, rank_ids, b)
            experts.append(e * ROWS_PER_EXPERT)
            gates.append(s_k / denom)
    idx_ref[...] = jnp.concatenate(experts, axis=0).T
    gate_ref[...] = jnp.concatenate(gates, axis=0).T


def _route(scores_t):
    n = scores_t.shape[-1]
    tn = ROUTE_TILE
    flat = jnp.asarray(_candidate_tables())
    return pl.pallas_call(
        _route_body,
        grid=(n // tn,),
        in_specs=[pl.BlockSpec((2 * PEER_HEADS, PEER_N_KEYS, tn), lambda i: (0, 0, i)),
                  pl.BlockSpec(flat.shape, lambda i: (0, 0))],
        out_specs=[pl.BlockSpec((tn, N_SEL), lambda i: (i, 0)), pl.BlockSpec((tn, N_SEL), lambda i: (i, 0))],
        out_shape=[jax.ShapeDtypeStruct((n, N_SEL), jnp.int32), jax.ShapeDtypeStruct((n, N_SEL), F32)],
        compiler_params=_params(1),
        name="peer_route",
    )(scores_t, flat)


def _pack_table(w):
    e = w.shape[0]
    bits = lax.bitcast_convert_type(w.astype(BF16), jnp.uint16).astype(jnp.uint32)
    word = bits[:, :HALF] | (bits[:, HALF:] << 16)
    return lax.bitcast_convert_type(word, jnp.int32).reshape(e * ROWS_PER_EXPERT, LANES)


def _pair_matrix():
    r = lax.broadcasted_iota(jnp.int32, (N_SEL, 2 * N_SEL), 0)
    c = lax.broadcasted_iota(jnp.int32, (N_SEL, 2 * N_SEL), 1)
    return ((c >> 1) == r).astype(BF16)


def _load_table_once(tab_hbm, tab, sem, lhs):
    @pl.when(pl.program_id(0) == 0)
    def _():
        cp = pltpu.make_async_copy(tab_hbm, tab, sem)
        cp.start()
        lhs[...] = jnp.zeros_like(lhs)
        cp.wait()


def _stage_rows(lhs, chunk, row, value):
    lhs.at[chunk][pl.ds(row, value.shape[0], stride=SUBLANES), :] = value


def _token_lhs(lhs, t, chunks):
    sl = pl.ds(pl.multiple_of(t * SUBLANES, SUBLANES), SUBLANES)
    return jnp.concatenate([lhs[s, sl, :] for s in chunks], axis=1).astype(BF16)


def _gather_token(idx_ref, t, tab, tile):
    for e in range(N_SEL):
        row = pl.multiple_of(idx_ref[t, e], ROWS_PER_EXPERT)
        tile[pl.ds(e, ROWS_PER_EXPERT, stride=TILE_STRIDE), :] = tab[pl.ds(row, ROWS_PER_EXPERT), :]


def _tile_chunk(tile, s):
    return pltpu.bitcast(tile[s * TILE_STRIDE:s * TILE_STRIDE + N_SEL, :], BF16)


def _peer_act_body(idx_ref, x_ref, tab_hbm, o_ref, tab, sem, lhs, part, *tiles):
    _load_table_once(tab_hbm, tab, sem, lhs)
    x = x_ref[...]
    hi = x.astype(BF16).astype(F32)
    lo = (x - hi).astype(BF16).astype(F32)
    for s in range(ROWS_PER_EXPERT):
        low, high = slice(s * LANES, (s + 1) * LANES), slice(HALF + s * LANES, HALF + (s + 1) * LANES)
        _stage_rows(lhs, s, 0, hi[:, low])
        _stage_rows(lhs, s, 1, lo[:, low])
        _stage_rows(lhs, s, 2, hi[:, high])
        _stage_rows(lhs, s, 3, lo[:, high])
    lane = lax.broadcasted_iota(jnp.int32, (1, 2 * N_SEL), 1)
    even = (lane & 1) == 0

    def step(k, carry):
        for u in range(PEER_UNROLL):
            t = k * PEER_UNROLL + u
            tile = tiles[u]
            _gather_token(idx_ref, t, tab, tile)
            acc = jnp.zeros((SUBLANES, 2 * N_SEL), F32)
            for s in range(ROWS_PER_EXPERT):
                acc = acc + lax.dot_general(_token_lhs(lhs, t, (s,)), _tile_chunk(tile, s),
                                            (((1,), (1,)), ((), ())), preferred_element_type=F32)
            part[pl.ds(t, 1), :] = jnp.where(even, acc[0:1] + acc[1:2], acc[2:3] + acc[3:4])
        return carry

    lax.fori_loop(0, PEER_TOKENS_PER_STEP // PEER_UNROLL, step, 0)
    pair_t = _pair_matrix_t()
    o_ref[...] = _dot_exact_rhs(part[...], pair_t)


def _pair_matrix_t():
    r = lax.broadcasted_iota(jnp.int32, (2 * N_SEL, N_SEL), 0)
    c = lax.broadcasted_iota(jnp.int32, (2 * N_SEL, N_SEL), 1)
    return ((r >> 1) == c).astype(BF16)


def _peer_out_body(idx_ref, act_ref, gate_ref, x1_ref, gate2_ref, fg_ref, tab_hbm, o_ref, tab, sem, lhs, acc_scr,
                   *tiles):
    _load_table_once(tab_hbm, tab, sem, lhs)
    coef = gate_ref[...] * _gelu(act_ref[...])
    wide = _dot_exact_rhs(coef, _pair_matrix())
    hi = wide.astype(BF16).astype(F32)
    lo = (wide - hi).astype(BF16).astype(F32)
    lane = lax.broadcasted_iota(jnp.int32, (1, 2 * N_SEL), 1)
    even = (lane & 1) == 0
    for s in range(2):
        sl = slice(s * LANES, (s + 1) * LANES)
        ev = even[:, sl]
        _stage_rows(lhs, s, 0, jnp.where(ev, hi[:, sl], 0.0))
        _stage_rows(lhs, s, 1, jnp.where(ev, lo[:, sl], 0.0))
        _stage_rows(lhs, s, 2, jnp.where(ev, 0.0, hi[:, sl]))
        _stage_rows(lhs, s, 3, jnp.where(ev, 0.0, lo[:, sl]))

    def step(k, carry):
        for u in range(PEER_UNROLL):
            t = k * PEER_UNROLL + u
            tile = tiles[u]
            _gather_token(idx_ref, t, tab, tile)
            coef_rows = _token_lhs(lhs, t, (0, 1))
            res = [_dot(coef_rows, _tile_chunk(tile, s)) for s in range(ROWS_PER_EXPERT)]
            acc_scr[pl.ds(t, 1), :] = jnp.concatenate([r[0:1] + r[1:2] for r in res] + [r[2:3] + r[3:4] for r in res],
                                                       axis=1)
        return carry

    lax.fori_loop(0, PEER_TOKENS_PER_STEP // PEER_UNROLL, step, 0)
    x2 = x1_ref[...] + gate2_ref[0] * acc_scr[...]
    o_ref[...] = _rms(x2) * fg_ref[...]


def _peer_scratch(table, lhs_chunks, extra):
    tb = PEER_TOKENS_PER_STEP
    return ([pltpu.VMEM(table.shape, jnp.int32), pltpu.SemaphoreType.DMA(()),
             pltpu.VMEM((lhs_chunks, tb * SUBLANES, LANES), F32), extra]
            + [pltpu.VMEM((ROWS_PER_EXPERT * TILE_STRIDE, LANES), jnp.int32) for _ in range(PEER_UNROLL)])


def _peer_act(idx, h2, table):
    n, d = h2.shape
    tb = PEER_TOKENS_PER_STEP
    return pl.pallas_call(
        _peer_act_body,
        grid=(n // tb,),
        in_specs=[pl.BlockSpec((tb, N_SEL), lambda i: (i, 0), memory_space=pltpu.SMEM),
                  pl.BlockSpec((tb, d), lambda i: (i, 0)),
                  pl.BlockSpec(memory_space=pl.ANY)],
        out_specs=pl.BlockSpec((tb, N_SEL), lambda i: (i, 0)),
        out_shape=jax.ShapeDtypeStruct((n, N_SEL), F32),
        scratch_shapes=_peer_scratch(table, ROWS_PER_EXPERT, pltpu.VMEM((tb, 2 * N_SEL), F32)),
        compiler_params=_params(1),
        name="peer_act",
    )(idx, h2, table)


def _peer_out(idx, act, gate, x1, gate2, final_g, table, tokens_per_batch):
    n, d = x1.shape
    tb = PEER_TOKENS_PER_STEP
    steps_per_batch = tokens_per_batch // tb
    row = lambda w: pl.BlockSpec((tb, w), lambda i: (i, 0))
    return pl.pallas_call(
        _peer_out_body,
        grid=(n // tb,),
        in_specs=[pl.BlockSpec((tb, N_SEL), lambda i: (i, 0), memory_space=pltpu.SMEM),
                  row(N_SEL), row(N_SEL), row(d),
                  pl.BlockSpec((1, 1, d), lambda i: (i // steps_per_batch, 0, 0)),
                  pl.BlockSpec((1, d), lambda i: (0, 0)),
                  pl.BlockSpec(memory_space=pl.ANY)],
        out_specs=row(d),
        out_shape=jax.ShapeDtypeStruct((n, d), F32),
        scratch_shapes=_peer_scratch(table, 2, pltpu.VMEM((tb, d), F32)),
        compiler_params=_params(1),
        name="peer_out",
    )(idx, act, gate, x1, gate2, final_g, table)


def kernel(x, c, ctx, c_ctx, ada_w, ada_b, norm_mix_g, w_in, hgrn_lb_logits, hgrn_norm_g, gla_gk_w, gla_gk_b, gla_norm_g, w_branch_hgrn, w_branch_gla, w_out, norm_ffn_g, peer_wq, peer_k1, peer_k2, peer_u, peer_v, final_g):
    bsz, seq, d = x.shape
    t_ctx = ctx.shape[1]
    depth = ada_w.shape[0]
    assert depth == 1 and d == D_MODEL and w_in.shape[-1] == D_IN
    l = 0
    row2 = lambda a: a.reshape(1, -1)

    lower = jnp.cumsum(jax.nn.softmax(hgrn_lb_logits, axis=0), axis=0)[l]
    lower = jnp.concatenate([lower, jnp.zeros((SUBLANES - 2, HG_KEY), F32)], axis=0)
    cvec = jnp.concatenate([c, c_ctx[None], jnp.zeros((SUBLANES - 1, d), F32)], axis=0)
    mod_all = _adaln(cvec, ada_w[l], ada_b[l])
    mod = [m.reshape(bsz, 1, d) for m in jnp.split(mod_all[:bsz], N_MOD, axis=-1)]
    mod_c = [jnp.broadcast_to(m.reshape(1, 1, d), (bsz, 1, d)) for m in jnp.split(mod_all[bsz:bsz + 1], N_MOD, axis=-1)]
    w_in_b = w_in[l].astype(BF16)
    zr = jnp.zeros((GLA_GATE_RANK, GLA_KEY), F32)
    wrank = jnp.concatenate([jnp.concatenate([gla_gk_w[l, 0], zr], axis=1),
                             jnp.concatenate([zr, gla_gk_w[l, 1]], axis=1)], axis=0).astype(BF16)
    brank = gla_gk_b[l].reshape(1, 2 * GLA_KEY)

    lat = _inproj(x, mod[0], mod[1], row2(norm_mix_g[l]), w_in_b, wrank, brank)
    cx = _inproj(ctx, mod_c[0], mod_c[1], row2(norm_mix_g[l]), w_in_b, wrank, brank)
    (hq, zf, zb, hi, hgate, gq, gk, gv, gg, glf, glb, m_hg, m_gla) = lat
    (_, czf, czb, chi, _, _, cgk, cgv, _, cglf, cglb, _, _) = cx

    y_hg = _scan_call(
        _hgrn_scan_body, HG_HEADS, HG_HEAD_V,
        [(hq, HEAD_K), (zf, HEAD_K), (zb, HEAD_K), (hi, HG_HEAD_V), (hgate, HG_HEAD_V)],
        [(czf, HEAD_K), (czb, HEAD_K), (chi, HG_HEAD_V)],
        [(lower, (SUBLANES, HEAD_K), lambda b, h: (0, h)), (row2(hgrn_norm_g[l]), (1, HG_HEAD_V), lambda b, h: (0, 0))],
        bsz, seq, t_ctx, "hgrn_scan")
    y_gla = _scan_call(
        _gla_scan_body, GLA_HEADS, GLA_HEAD_V,
        [(gq, HEAD_K), (gk, HEAD_K), (glf, HEAD_K), (glb, HEAD_K), (gv, GLA_HEAD_V), (gg, GLA_HEAD_V)],
        [(cgk, HEAD_K), (cglf, HEAD_K), (cglb, HEAD_K), (cgv, GLA_HEAD_V)],
        [(row2(gla_norm_g[l]), (1, GLA_HEAD_V), lambda b, h: (0, 0))],
        bsz, seq, t_ctx, "gla_scan")

    x1, h2, scores_t = _mixer_out(
        y_hg, y_gla, m_hg, m_gla, x, mod[2], mod[3], mod[4], row2(norm_ffn_g[l]),
        w_branch_hgrn[l].astype(BF16), w_branch_gla[l].astype(BF16), w_out[l].astype(BF16),
        peer_wq[l].astype(BF16), peer_k1[l].astype(BF16), peer_k2[l].astype(BF16))

    idx, gate = _route(scores_t)
    n = bsz * seq
    act = _peer_act(idx, h2.reshape(n, d), _pack_table(peer_u[l]))
    out = _peer_out(idx, act, gate, x1.reshape(n, d), mod[5], row2(final_g), _pack_table(peer_v[l]), seq)
    return out.reshape(bsz, seq, d)
```

```python
import functools

import jax
import jax.numpy as jnp
import numpy as np
from jax import lax
from jax.experimental import pallas as pl
from jax.experimental.pallas import tpu as pltpu

F32 = jnp.float32
BF16 = jnp.bfloat16

D_MODEL = 1024
GRID_W = 64
CTX_CHUNK = 64
N_MOD = 6
EPS = 1e-6

HG_HEADS = 8
HEAD_K = 128
HG_KEY = HG_HEADS * HEAD_K
HG_VAL = D_MODEL
HG_HEAD_V = HG_VAL // HG_HEADS
GLA_HEADS = 4
GLA_KEY = D_MODEL // 2
GLA_VAL = D_MODEL
GLA_HEAD_V = GLA_VAL // GLA_HEADS
GLA_GATE_RANK = 16
GLA_GATE_NORMALIZER = 16.0

C_HQ, C_ZF, C_ZB, C_HI, C_HGATE = 0, 1024, 2048, 3072, 4096
C_GQ, C_GK, C_GV, C_GG = 5120, 5632, 6144, 7168
C_RANK = 8192
C_MHG = C_RANK + 2 * GLA_GATE_RANK
C_MGLA = C_MHG + D_MODEL
D_IN = C_MGLA + D_MODEL

PEER_HEADS = 8
PEER_N_KEYS = 128
PEER_SUB_DIM = 128
PEER_TOPK = 16
N_SEL = PEER_HEADS * PEER_TOPK

LANES = 128
SUBLANES = 8
HALF = D_MODEL // 2
ROWS_PER_EXPERT = HALF // LANES
TILE_STRIDE = N_SEL + 8
PEER_TOKENS_PER_STEP = 128
PEER_TILES = 4
VMEM_LIMIT = 56 * 1024 * 1024

ROW_TILE = 256
INPROJ_ROW_TILE = 128
ROUTE_TILE = 256
SCAN_CHUNKS_PER_TRIP = 4


def _resident(shape):
    return pl.BlockSpec(shape, lambda *_: (0,) * len(shape), pipeline_mode=pl.Buffered(1))


def _params(n_axes):
    return pltpu.CompilerParams(dimension_semantics=("arbitrary",) * n_axes, vmem_limit_bytes=VMEM_LIMIT)


def _split3(a):
    p1 = a.astype(BF16)
    r1 = a - p1.astype(F32)
    p2 = r1.astype(BF16)
    p3 = (r1 - p2.astype(F32)).astype(BF16)
    return p1, p2, p3


def _dot(a, b):
    return jnp.dot(a, b, preferred_element_type=F32)


def _dot_exact_rhs(a, rhs_bf16):
    p1, p2, p3 = _split3(a)
    return _dot(p1, rhs_bf16) + _dot(p2, rhs_bf16) + _dot(p3, rhs_bf16)


def _rms(x):
    return x * lax.rsqrt(jnp.mean(x * x, axis=-1, keepdims=True) + EPS)


def _gelu(x):
    return 0.5 * x * (1.0 + lax.erf(x * np.float32(2.0 ** -0.5)))


def _sigmoid(x):
    return 1.0 / (1.0 + jnp.exp(-x))


def _log_sigmoid(x):
    return jnp.minimum(x, 0.0) - jnp.log1p(jnp.exp(-jnp.abs(x)))


def _adaln_body(c_ref, w_ref, b_ref, o_ref):
    c = c_ref[...]
    s = (c * _sigmoid(c)).astype(BF16)
    o_ref[...] = _dot(s, w_ref[...].astype(BF16)) + b_ref[...]


def _adaln(cvec, w, b):
    rows, d = cvec.shape
    n_out = w.shape[1]
    return pl.pallas_call(
        _adaln_body,
        grid=(n_out // d,),
        in_specs=[pl.BlockSpec((rows, d), lambda j: (0, 0)),
                  pl.BlockSpec((d, d), lambda j: (0, j)),
                  pl.BlockSpec((1, d), lambda j: (0, j))],
        out_specs=pl.BlockSpec((rows, d), lambda j: (0, j)),
        out_shape=jax.ShapeDtypeStruct((rows, n_out), F32),
        compiler_params=_params(1),
        name="adaln",
    )(cvec, w, b.reshape(1, n_out))


def _inproj_body(x_ref, shift_ref, scale_ref, g_ref, w_ref, wrank_ref, brank_ref,
                 hq_ref, zf_ref, zb_ref, hi_ref, hgate_ref, gq_ref, gk_ref, gv_ref, gg_ref,
                 glf_ref, glb_ref, mhg_ref, mgla_ref):
    h = _rms(x_ref[0]) * g_ref[...] * (1.0 + scale_ref[0]) + shift_ref[0]
    hb = h.astype(BF16)

    def cols(start, width):
        return _dot(hb, w_ref[:, start:start + width])

    hq_ref[0] = cols(C_HQ, HG_KEY) * np.float32(HEAD_K ** -0.5)
    zf_ref[0] = cols(C_ZF, HG_KEY)
    zb_ref[0] = cols(C_ZB, HG_KEY)
    hi_ref[0] = cols(C_HI, HG_VAL)
    hgate_ref[0] = cols(C_HGATE, HG_VAL)
    gq_ref[0] = cols(C_GQ, GLA_KEY) * np.float32(HEAD_K ** -0.5)
    gk_ref[0] = cols(C_GK, GLA_KEY)
    gv_ref[0] = cols(C_GV, GLA_VAL)
    gg_ref[0] = cols(C_GG, GLA_VAL)
    mhg_ref[0] = cols(C_MHG, D_MODEL)
    mgla_ref[0] = cols(C_MGLA, D_MODEL)
    rank = cols(C_RANK, 2 * GLA_GATE_RANK).astype(BF16)
    pre = _dot(rank, wrank_ref[...]) + brank_ref[...]
    lg = _log_sigmoid(pre) / GLA_GATE_NORMALIZER
    glf_ref[0] = lg[:, :GLA_KEY]
    glb_ref[0] = lg[:, GLA_KEY:]


def _inproj(x, shift, scale, g, w_bf16, wrank, brank):
    bsz, t, d = x.shape
    tm = INPROJ_ROW_TILE
    widths = (HG_KEY, HG_KEY, HG_KEY, HG_VAL, HG_VAL, GLA_KEY, GLA_KEY, GLA_VAL, GLA_VAL,
              GLA_KEY, GLA_KEY, D_MODEL, D_MODEL)
    row = lambda w: pl.BlockSpec((1, tm, w), lambda b, i: (b, i, 0))
    per_batch = pl.BlockSpec((1, 1, d), lambda b, i: (b, 0, 0))
    return pl.pallas_call(
        _inproj_body,
        grid=(bsz, t // tm),
        in_specs=[row(d), per_batch, per_batch, _resident((1, d)), _resident(w_bf16.shape),
                  _resident(wrank.shape), _resident(brank.shape)],
        out_specs=[row(w) for w in widths],
        out_shape=[jax.ShapeDtypeStruct((bsz, t, w), F32) for w in widths],
        compiler_params=_params(2),
        name="inproj",
    )(x, shift, scale, g, w_bf16, wrank, brank)


def _cumulative_decays(lgs, tri3):
    wide = jnp.concatenate(lgs, axis=1)
    b = _dot(tri3, jnp.concatenate(_split3(wide), axis=0))
    return [b[:, j * HEAD_K:(j + 1) * HEAD_K] for j in range(len(lgs))]


def _chunk_step(q, k, b, v, state, mask, ref_row, last_row, want_out):
    b_ref = b[ref_row:ref_row + 1]
    b_last = b[last_row:last_row + 1]
    vb = v.astype(BF16)
    o = None
    if want_out:
        qd = (q * jnp.exp(b - b_ref)).astype(BF16)
        kd = (k * jnp.exp(b_ref - b)).astype(BF16)
        scores = lax.dot_general(qd, kd, (((1,), (1,)), ((), ())), preferred_element_type=F32)
        scores = jnp.where(mask, scores, 0.0).astype(BF16)
        qs = (q * jnp.exp(b)).astype(BF16)
        o = _dot(scores, vb) + lax.dot_general(qs, state.astype(BF16), (((1,), (1,)), ((), ())),
                                               preferred_element_type=F32)
    kl = (k * jnp.exp(b_last - b)).astype(BF16)
    new_state = state * jnp.exp(b_last) + lax.dot_general(vb, kl, (((0,), (0,)), ((), ())),
                                                          preferred_element_type=F32)
    return o, new_state


def _scan_core(load_fwd, load_bwd, load_ctx_fwd, load_ctx_bwd, gate_ref, g_ref, y_ref,
               of_scr, ob_scr, sf_scr, sb_scr, n_chunks, n_ctx_chunks, dv):
    c = GRID_W
    ri = lax.broadcasted_iota(jnp.int32, (c, c), 0)
    ci = lax.broadcasted_iota(jnp.int32, (c, c), 1)
    lower = ci <= ri
    upper = ci >= ri
    tri_f = jnp.concatenate([lower.astype(BF16)] * 3, axis=1)
    tri_b = jnp.concatenate([upper.astype(BF16)] * 3, axis=1)
    fwd = dict(mask=lower, ref_row=c // 2, last_row=c - 1)
    bwd = dict(mask=upper, ref_row=c - 1 - c // 2, last_row=0)

    def run(load, rows, tri3, state, out_scr, kind):
        loaded = [load(r) for r in rows]
        bs = _cumulative_decays([lg for _, _, lg, _ in loaded], tri3)
        for r, (q, k, _, v), b in zip(rows, loaded, bs):
            o, state = _chunk_step(q, k, b, v, state, want_out=out_scr is not None, **kind)
            if out_scr is not None:
                out_scr[pl.ds(r, c), :] = o
        return state

    zero = jnp.zeros((dv, HEAD_K), F32)
    ctx_rows = [i * CTX_CHUNK for i in range(n_ctx_chunks)]
    sf_scr[...] = run(load_ctx_fwd, ctx_rows, tri_f, zero, None, fwd)
    sb_scr[...] = run(load_ctx_bwd, ctx_rows[::-1], tri_b, zero, None, bwd)

    def step(i, carry):
        ns = [i * SCAN_CHUNKS_PER_TRIP + j for j in range(SCAN_CHUNKS_PER_TRIP)]
        rows_f = [pl.multiple_of(n * c, c) for n in ns]
        rows_b = [pl.multiple_of((n_chunks - 1 - n) * c, c) for n in ns]
        sf_scr[...] = run(load_fwd, rows_f, tri_f, sf_scr[...], of_scr, fwd)
        sb_scr[...] = run(load_bwd, rows_b, tri_b, sb_scr[...], ob_scr, bwd)
        return carry

    lax.fori_loop(0, n_chunks // SCAN_CHUNKS_PER_TRIP, step, 0)

    o = of_scr[...] + ob_scr[...]
    gate = gate_ref[0]
    y_ref[0] = _rms(o) * g_ref[...] * (gate * _sigmoid(gate))


def _hgrn_scan_body(q_ref, zf_ref, zb_ref, v_ref, gate_ref, czf_ref, czb_ref, cv_ref, lb_ref, g_ref,
                    y_ref, of_scr, ob_scr, sf_scr, sb_scr, *, n_chunks, n_ctx_chunks):
    def forget(z, lower):
        f = lower + (1.0 - lower) * _sigmoid(z)
        return 1.0 - f, jnp.log(f)

    def loader(q_r, z_r, v_r, row):
        lower = lb_ref[row:row + 1, :]

        def load(rows):
            sl = pl.ds(rows, GRID_W)
            k, lg = forget(z_r[0, sl, :], lower)
            q = q_r[0, sl, :] if q_r is not None else None
            return q, k, lg, v_r[0, sl, :]
        return load

    _scan_core(loader(q_ref, zf_ref, v_ref, 0), loader(q_ref, zb_ref, v_ref, 1),
               loader(None, czf_ref, cv_ref, 0), loader(None, czb_ref, cv_ref, 1),
               gate_ref, g_ref, y_ref, of_scr, ob_scr, sf_scr, sb_scr, n_chunks, n_ctx_chunks, HG_HEAD_V)


def _gla_scan_body(q_ref, k_ref, lf_ref, lbk_ref, v_ref, gate_ref, ck_ref, clf_ref, clb_ref, cv_ref, g_ref,
                   y_ref, of_scr, ob_scr, sf_scr, sb_scr, *, n_chunks, n_ctx_chunks):
    def loader(q_r, k_r, l_r, v_r):
        def load(rows):
            sl = pl.ds(rows, GRID_W)
            q = q_r[0, sl, :] if q_r is not None else None
            return q, k_r[0, sl, :], l_r[0, sl, :], v_r[0, sl, :]
        return load

    _scan_core(loader(q_ref, k_ref, lf_ref, v_ref), loader(q_ref, k_ref, lbk_ref, v_ref),
               loader(None, ck_ref, clf_ref, cv_ref), loader(None, ck_ref, clb_ref, cv_ref),
               gate_ref, g_ref, y_ref, of_scr, ob_scr, sf_scr, sb_scr, n_chunks, n_ctx_chunks, GLA_HEAD_V)


def _scan_call(body, n_heads, dv, lat, ctx, small, bsz, t, t_ctx, name):
    head = lambda rows, w: pl.BlockSpec((1, rows, w), lambda b, h: (b, 0, h))
    in_specs = ([head(t, w) for _, w in lat] + [head(t_ctx, w) for _, w in ctx]
                + [pl.BlockSpec(bs, im) for _, bs, im in small])
    return pl.pallas_call(
        functools.partial(body, n_chunks=t // GRID_W, n_ctx_chunks=t_ctx // CTX_CHUNK),
        grid=(bsz, n_heads),
        in_specs=in_specs,
        out_specs=head(t, dv),
        out_shape=jax.ShapeDtypeStruct((bsz, t, n_heads * dv), F32),
        scratch_shapes=[pltpu.VMEM((t, dv), F32), pltpu.VMEM((t, dv), F32),
                        pltpu.VMEM((dv, HEAD_K), F32), pltpu.VMEM((dv, HEAD_K), F32)],
        compiler_params=_params(2),
        name=name,
    )(*[a for a, _ in lat], *[a for a, _ in ctx], *[a for a, _, _ in small])


def _mixer_out_body(yh_ref, yg_ref, mh_ref, mg_ref, x_ref, gate1_ref, shift2_ref, scale2_ref, g2_ref,
                    wbh_ref, wbg_ref, wo_ref, wq_ref, k1_ref, k2_ref,
                    x1_ref, h2_ref, st_ref):
    yh = _dot(yh_ref[0].astype(BF16), wbh_ref[...])
    yg = _dot(yg_ref[0].astype(BF16), wbg_ref[...])
    y = _sigmoid(mh_ref[0]) * yh + _sigmoid(mg_ref[0]) * yg
    x1 = x_ref[0] + gate1_ref[0] * _dot(y.astype(BF16), wo_ref[...])
    x1_ref[0] = x1
    h2 = _rms(x1) * g2_ref[...] * (1.0 + scale2_ref[0]) + shift2_ref[0]
    h2_ref[0] = h2
    q = _dot(h2.astype(BF16), wq_ref[...]).astype(BF16)
    for hh in range(2 * PEER_HEADS):
        keys = k1_ref if hh % 2 == 0 else k2_ref
        st_ref[hh] = lax.dot_general(keys[...], q[:, hh * PEER_SUB_DIM:(hh + 1) * PEER_SUB_DIM],
                                     (((1,), (1,)), ((), ())), preferred_element_type=F32)


def _mixer_out(y_hg, y_gla, m_hg, m_gla, x, gate1, shift2, scale2, g2, wbh, wbg, wo, wq, k1, k2):
    bsz, t, d = x.shape
    tm = ROW_TILE
    row = pl.BlockSpec((1, tm, d), lambda b, i: (b, i, 0))
    per_batch = pl.BlockSpec((1, 1, d), lambda b, i: (b, 0, 0))
    tiles_per_batch = t // tm
    n = bsz * t
    return pl.pallas_call(
        _mixer_out_body,
        grid=(bsz, tiles_per_batch),
        in_specs=[row, row, row, row, row, per_batch, per_batch, per_batch, _resident((1, d)),
                  _resident(wbh.shape), _resident(wbg.shape), _resident(wo.shape), _resident(wq.shape),
                  _resident(k1.shape), _resident(k2.shape)],
        out_specs=[row, row,
                   pl.BlockSpec((2 * PEER_HEADS, PEER_N_KEYS, tm), lambda b, i: (0, 0, b * tiles_per_batch + i))],
        out_shape=[jax.ShapeDtypeStruct((bsz, t, d), F32), jax.ShapeDtypeStruct((bsz, t, d), F32),
                   jax.ShapeDtypeStruct((2 * PEER_HEADS, PEER_N_KEYS, n), F32)],
        compiler_params=_params(2),
        name="mixer_out",
    )(y_hg, y_gla, m_hg, m_gla, x, gate1, shift2, scale2, g2, wbh, wbg, wo, wq, k1, k2)


_CAND_ROWS = PEER_TOPK + (PEER_TOPK - 1) * SUBLANES


def _candidate_tables():
    flat = np.full((_CAND_ROWS, 1), -1, np.int32)
    for a in range(PEER_TOPK):
        base = 0 if a == 0 else PEER_TOPK + (a - 1) * SUBLANES
        for b in range(PEER_TOPK if a == 0 else SUBLANES):
            if (a + 1) * (b + 1) <= PEER_TOPK:
                flat[base + b, 0] = a * PEER_TOPK + b
    return flat


def _extract_top(x, key, rounds):
    big = np.int32(2 ** 30)
    vals, keys = [], []
    for _ in range(rounds):
        m = jnp.max(x, axis=0, keepdims=True)
        sel = jnp.min(jnp.where(x == m, key, big), axis=0, keepdims=True)
        vals.append(m)
        keys.append(sel)
        x = jnp.where(key == sel, -jnp.inf, x)
    return vals, keys


def _pick_row(table, row_ids, sel):
    return jnp.sum(jnp.where(row_ids == sel, table, 0), axis=0, keepdims=True)


def _route_body(st_ref, flat_ref, idx_ref, gate_ref):
    tn = st_ref.shape[-1]
    key_ids = lax.broadcasted_iota(jnp.int32, (PEER_N_KEYS, tn), 0)
    rank_ids = lax.broadcasted_iota(jnp.int32, (PEER_TOPK, tn), 0)
    flat = jnp.broadcast_to(flat_ref[...], (_CAND_ROWS, tn))
    valid = flat >= 0
    experts, gates = [], []
    for h in range(PEER_HEADS):
        v1, i1 = _extract_top(st_ref[2 * h], key_ids, PEER_TOPK)
        v2, i2 = _extract_top(st_ref[2 * h + 1], key_ids, PEER_TOPK)
        v1c, i1c = jnp.concatenate(v1, axis=0), jnp.concatenate(i1, axis=0)
        v2c, i2c = jnp.concatenate(v2, axis=0), jnp.concatenate(i2, axis=0)
        cand = jnp.concatenate([v1[0] + v2c] + [v1[a] + v2c[:SUBLANES] for a in range(1, PEER_TOPK)], axis=0)
        cand = jnp.where(valid, cand, -jnp.inf)
        top_s, top_c = _extract_top(cand, flat, PEER_TOPK)
        m = top_s[0]
        ex = [jnp.exp(s - m) for s in top_s]
        denom = functools.reduce(lambda a, b: a + b, ex)
        for s_k, c_k in zip(ex, top_c):
            a = c_k >> 4
            b = c_k & (PEER_TOPK - 1)
            e = _pick_row(i1c, rank_ids, a) * PEER_N_KEYS + _pick_row(i2c, rank_ids, b)
            experts.append(e * ROWS_PER_EXPERT)
            gates.append(s_k / denom)
    idx_ref[...] = jnp.concatenate(experts, axis=0).T
    gate_ref[...] = jnp.concatenate(gates, axis=0).T


def _route(scores_t):
    n = scores_t.shape[-1]
    tn = ROUTE_TILE
    flat = jnp.asarray(_candidate_tables())
    return pl.pallas_call(
        _route_body,
        grid=(n // tn,),
        in_specs=[pl.BlockSpec((2 * PEER_HEADS, PEER_N_KEYS, tn), lambda i: (0, 0, i)),
                  pl.BlockSpec(flat.shape, lambda i: (0, 0))],
        out_specs=[pl.BlockSpec((tn, N_SEL), lambda i: (i, 0)), pl.BlockSpec((tn, N_SEL), lambda i: (i, 0))],
        out_shape=[jax.ShapeDtypeStruct((n, N_SEL), jnp.int32), jax.ShapeDtypeStruct((n, N_SEL), F32)],
        compiler_params=_params(1),
        name="peer_route",
    )(scores_t, flat)


def _pack_table(w):
    e = w.shape[0]
    bits = lax.bitcast_convert_type(w.astype(BF16), jnp.uint16).astype(jnp.uint32)
    word = bits[:, :HALF] | (bits[:, HALF:] << 16)
    return lax.bitcast_convert_type(word, jnp.int32).reshape(e * ROWS_PER_EXPERT, LANES)


def _pair_matrix():
    r = lax.broadcasted_iota(jnp.int32, (N_SEL, 2 * N_SEL), 0)
    c = lax.broadcasted_iota(jnp.int32, (N_SEL, 2 * N_SEL), 1)
    return ((c >> 1) == r).astype(BF16)


def _load_table_once(tab_hbm, tab, sem, lhs):
    @pl.when(pl.program_id(0) == 0)
    def _():
        cp = pltpu.make_async_copy(tab_hbm, tab, sem)
        cp.start()
        lhs[...] = jnp.zeros_like(lhs)
        cp.wait()


def _stage_rows(lhs, chunk, row, value):
    lhs.at[chunk][pl.ds(row, value.shape[0], stride=SUBLANES), :] = value


def _token_lhs(lhs, t, chunks):
    sl = pl.ds(pl.multiple_of(t * SUBLANES, SUBLANES), SUBLANES)
    return jnp.concatenate([lhs[s, sl, :] for s in chunks], axis=1).astype(BF16)


def _gather_token(idx_ref, t, tab, tile):
    for e in range(N_SEL):
        row = pl.multiple_of(idx_ref[t, e], ROWS_PER_EXPERT)
        tile[pl.ds(e, ROWS_PER_EXPERT, stride=TILE_STRIDE), :] = tab[pl.ds(row, ROWS_PER_EXPERT), :]


def _tile_chunk(tile, s):
    return pltpu.bitcast(tile[s * TILE_STRIDE:s * TILE_STRIDE + N_SEL, :], BF16)


def _pipelined_tokens(idx_ref, tab, tiles, contract, finish):
    tb = PEER_TOKENS_PER_STEP
    pair_a, pair_b = tiles[:2], tiles[2:]

    def gather(t, tile):
        _gather_token(idx_ref, jnp.minimum(t, tb - 1), tab, tile)

    def trip(t, ready, free):
        for j in range(2):
            result = contract(t + j, ready[j])
            gather(t + 2 + j, free[j])
            finish(t + j, result)

    gather(0, pair_a[0])
    gather(1, pair_a[1])

    def step(k, carry):
        t = 2 * k

        @pl.when(k % 2 == 0)
        def _():
            trip(t, pair_a, pair_b)

        @pl.when(k % 2 == 1)
        def _():
            trip(t, pair_b, pair_a)

        return carry

    lax.fori_loop(0, tb // 2, step, 0)


def _peer_act_body(idx_ref, x_ref, tab_hbm, o_ref, tab, sem, lhs, part, *tiles):
    _load_table_once(tab_hbm, tab, sem, lhs)
    x = x_ref[...]
    hi = x.astype(BF16).astype(F32)
    lo = (x - hi).astype(BF16).astype(F32)
    for s in range(ROWS_PER_EXPERT):
        low, high = slice(s * LANES, (s + 1) * LANES), slice(HALF + s * LANES, HALF + (s + 1) * LANES)
        _stage_rows(lhs, s, 0, hi[:, low])
        _stage_rows(lhs, s, 1, lo[:, low])
        _stage_rows(lhs, s, 2, hi[:, high])
        _stage_rows(lhs, s, 3, lo[:, high])
    lane = lax.broadcasted_iota(jnp.int32, (1, 2 * N_SEL), 1)
    even = (lane & 1) == 0

    def contract(t, tile):
        acc = jnp.zeros((SUBLANES, 2 * N_SEL), F32)
        for s in range(ROWS_PER_EXPERT):
            acc = acc + lax.dot_general(_token_lhs(lhs, t, (s,)), _tile_chunk(tile, s),
                                        (((1,), (1,)), ((), ())), preferred_element_type=F32)
        return acc

    def finish(t, acc):
        part[pl.ds(t, 1), :] = jnp.where(even, acc[0:1] + acc[1:2], acc[2:3] + acc[3:4])

    _pipelined_tokens(idx_ref, tab, tiles, contract, finish)
    pair_t = _pair_matrix_t()
    o_ref[...] = _dot_exact_rhs(part[...], pair_t)


def _pair_matrix_t():
    r = lax.broadcasted_iota(jnp.int32, (2 * N_SEL, N_SEL), 0)
    c = lax.broadcasted_iota(jnp.int32, (2 * N_SEL, N_SEL), 1)
    return ((r >> 1) == c).astype(BF16)


def _peer_out_body(idx_ref, act_ref, gate_ref, x1_ref, gate2_ref, fg_ref, tab_hbm, o_ref, tab, sem, lhs, acc_scr,
                   *tiles):
    _load_table_once(tab_hbm, tab, sem, lhs)
    coef = gate_ref[...] * _gelu(act_ref[...])
    wide = _dot_exact_rhs(coef, _pair_matrix())
    hi = wide.astype(BF16).astype(F32)
    lo = (wide - hi).astype(BF16).astype(F32)
    lane = lax.broadcasted_iota(jnp.int32, (1, 2 * N_SEL), 1)
    even = (lane & 1) == 0
    for s in range(2):
        sl = slice(s * LANES, (s + 1) * LANES)
        ev = even[:, sl]
        _stage_rows(lhs, s, 0, jnp.where(ev, hi[:, sl], 0.0))
        _stage_rows(lhs, s, 1, jnp.where(ev, lo[:, sl], 0.0))
        _stage_rows(lhs, s, 2, jnp.where(ev, 0.0, hi[:, sl]))
        _stage_rows(lhs, s, 3, jnp.where(ev, 0.0, lo[:, sl]))

    def contract(t, tile):
        coef_rows = _token_lhs(lhs, t, (0, 1))
        return [_dot(coef_rows, _tile_chunk(tile, s)) for s in range(ROWS_PER_EXPERT)]

    def finish(t, res):
        acc_scr[pl.ds(t, 1), :] = jnp.concatenate([r[0:1] + r[1:2] for r in res] + [r[2:3] + r[3:4] for r in res],
                                                   axis=1)

    _pipelined_tokens(idx_ref, tab, tiles, contract, finish)
    x2 = x1_ref[...] + gate2_ref[0] * acc_scr[...]
    o_ref[...] = _rms(x2) * fg_ref[...]


def _peer_scratch(table, lhs_chunks, extra):
    tb = PEER_TOKENS_PER_STEP
    return ([pltpu.VMEM(table.shape, jnp.int32), pltpu.SemaphoreType.DMA(()),
             pltpu.VMEM((lhs_chunks, tb * SUBLANES, LANES), F32), extra]
            + [pltpu.VMEM((ROWS_PER_EXPERT * TILE_STRIDE, LANES), jnp.int32) for _ in range(PEER_TILES)])


def _peer_act(idx, h2, table):
    n, d = h2.shape
    tb = PEER_TOKENS_PER_STEP
    return pl.pallas_call(
        _peer_act_body,
        grid=(n // tb,),
        in_specs=[pl.BlockSpec((tb, N_SEL), lambda i: (i, 0), memory_space=pltpu.SMEM),
                  pl.BlockSpec((tb, d), lambda i: (i, 0)),
                  pl.BlockSpec(memory_space=pl.ANY)],
        out_specs=pl.BlockSpec((tb, N_SEL), lambda i: (i, 0)),
        out_shape=jax.ShapeDtypeStruct((n, N_SEL), F32),
        scratch_shapes=_peer_scratch(table, ROWS_PER_EXPERT, pltpu.VMEM((tb, 2 * N_SEL), F32)),
        compiler_params=_params(1),
        name="peer_act",
    )(idx, h2, table)


def _peer_out(idx, act, gate, x1, gate2, final_g, table, tokens_per_batch):
    n, d = x1.shape
    tb = PEER_TOKENS_PER_STEP
    steps_per_batch = tokens_per_batch // tb
    row = lambda w: pl.BlockSpec((tb, w), lambda i: (i, 0))
    return pl.pallas_call(
        _peer_out_body,
        grid=(n // tb,),
        in_specs=[pl.BlockSpec((tb, N_SEL), lambda i: (i, 0), memory_space=pltpu.SMEM),
                  row(N_SEL), row(N_SEL), row(d),
                  pl.BlockSpec((1, 1, d), lambda i: (i // steps_per_batch, 0, 0)),
                  pl.BlockSpec((1, d), lambda i: (0, 0)),
                  pl.BlockSpec(memory_space=pl.ANY)],
        out_specs=row(d),
        out_shape=jax.ShapeDtypeStruct((n, d), F32),
        scratch_shapes=_peer_scratch(table, 2, pltpu.VMEM((tb, d), F32)),
        compiler_params=_params(1),
        name="peer_out",
    )(idx, act, gate, x1, gate2, final_g, table)


def kernel(x, c, ctx, c_ctx, ada_w, ada_b, norm_mix_g, w_in, hgrn_lb_logits, hgrn_norm_g, gla_gk_w, gla_gk_b, gla_norm_g, w_branch_hgrn, w_branch_gla, w_out, norm_ffn_g, peer_wq, peer_k1, peer_k2, peer_u, peer_v, final_g):
    bsz, seq, d = x.shape
    t_ctx = ctx.shape[1]
    depth = ada_w.shape[0]
    assert depth == 1 and d == D_MODEL and w_in.shape[-1] == D_IN
    l = 0
    row2 = lambda a: a.reshape(1, -1)

    lower = jnp.cumsum(jax.nn.softmax(hgrn_lb_logits, axis=0), axis=0)[l]
    lower = jnp.concatenate([lower, jnp.zeros((SUBLANES - 2, HG_KEY), F32)], axis=0)
    cvec = jnp.concatenate([c, c_ctx[None], jnp.zeros((SUBLANES - 1, d), F32)], axis=0)
    mod_all = _adaln(cvec, ada_w[l], ada_b[l])
    mod = [m.reshape(bsz, 1, d) for m in jnp.split(mod_all[:bsz], N_MOD, axis=-1)]
    mod_c = [jnp.broadcast_to(m.reshape(1, 1, d), (bsz, 1, d)) for m in jnp.split(mod_all[bsz:bsz + 1], N_MOD, axis=-1)]
    w_in_b = w_in[l].astype(BF16)
    zr = jnp.zeros((GLA_GATE_RANK, GLA_KEY), F32)
    wrank = jnp.concatenate([jnp.concatenate([gla_gk_w[l, 0], zr], axis=1),
                             jnp.concatenate([zr, gla_gk_w[l, 1]], axis=1)], axis=0).astype(BF16)
    brank = gla_gk_b[l].reshape(1, 2 * GLA_KEY)

    lat = _inproj(x, mod[0], mod[1], row2(norm_mix_g[l]), w_in_b, wrank, brank)
    cx = _inproj(ctx, mod_c[0], mod_c[1], row2(norm_mix_g[l]), w_in_b, wrank, brank)
    (hq, zf, zb, hi, hgate, gq, gk, gv, gg, glf, glb, m_hg, m_gla) = lat
    (_, czf, czb, chi, _, _, cgk, cgv, _, cglf, cglb, _, _) = cx

    y_hg = _scan_call(
        _hgrn_scan_body, HG_HEADS, HG_HEAD_V,
        [(hq, HEAD_K), (zf, HEAD_K), (zb, HEAD_K), (hi, HG_HEAD_V), (hgate, HG_HEAD_V)],
        [(czf, HEAD_K), (czb, HEAD_K), (chi, HG_HEAD_V)],
        [(lower, (SUBLANES, HEAD_K), lambda b, h: (0, h)), (row2(hgrn_norm_g[l]), (1, HG_HEAD_V), lambda b, h: (0, 0))],
        bsz, seq, t_ctx, "hgrn_scan")
    y_gla = _scan_call(
        _gla_scan_body, GLA_HEADS, GLA_HEAD_V,
        [(gq, HEAD_K), (gk, HEAD_K), (glf, HEAD_K), (glb, HEAD_K), (gv, GLA_HEAD_V), (gg, GLA_HEAD_V)],
        [(cgk, HEAD_K), (cglf, HEAD_K), (cglb, HEAD_K), (cgv, GLA_HEAD_V)],
        [(row2(gla_norm_g[l]), (1, GLA_HEAD_V), lambda b, h: (0, 0))],
        bsz, seq, t_ctx, "gla_scan")

    x1, h2, scores_t = _mixer_out(
        y_hg, y_gla, m_hg, m_gla, x, mod[2], mod[3], mod[4], row2(norm_ffn_g[l]),
        w_branch_hgrn[l].astype(BF16), w_branch_gla[l].astype(BF16), w_out[l].astype(BF16),
        peer_wq[l].astype(BF16), peer_k1[l].astype(BF16), peer_k2[l].astype(BF16))

    idx, gate = _route(scores_t)
    n = bsz * seq
    act = _peer_act(idx, h2.reshape(n, d), _pack_table(peer_u[l]))
    out = _peer_out(idx, act, gate, x1.reshape(n, d), mod[5], row2(final_g), _pack_table(peer_v[l]), seq)
    return out.reshape(bsz, seq, d)
```

```python
import functools
import math

import jax
import jax.numpy as jnp
import numpy as np
from jax import lax
from jax.experimental import pallas as pl
from jax.experimental.pallas import tpu as pltpu

F32 = jnp.float32
BF16 = jnp.bfloat16

D_MODEL = 1024
GRID_W = 64
CTX_CHUNK = 64
N_MOD = 6
EPS = 1e-6

HG_HEADS = 8
HEAD_K = 128
HG_KEY = HG_HEADS * HEAD_K
HG_VAL = D_MODEL
HG_HEAD_V = HG_VAL // HG_HEADS
GLA_HEADS = 4
GLA_KEY = D_MODEL // 2
GLA_VAL = D_MODEL
GLA_HEAD_V = GLA_VAL // GLA_HEADS
GLA_GATE_RANK = 16
GLA_GATE_NORMALIZER = 16.0

C_HQ, C_ZF, C_ZB, C_HI, C_HGATE = 0, 1024, 2048, 3072, 4096
C_GQ, C_GK, C_GV, C_GG = 5120, 5632, 6144, 7168
C_RANK = 8192
C_MHG = C_RANK + 2 * GLA_GATE_RANK
C_MGLA = C_MHG + D_MODEL
D_IN = C_MGLA + D_MODEL

PEER_HEADS = 8
PEER_N_KEYS = 128
PEER_SUB_DIM = 128
PEER_TOPK = 16
N_SEL = PEER_HEADS * PEER_TOPK

LANES = 128
SUBLANES = 8
HALF = D_MODEL // 2
ROWS_PER_EXPERT = HALF // LANES
TILE_STRIDE = N_SEL + 8
PEER_TOKENS_PER_STEP = 256
PEER_ACT_TOKENS_PER_TRIP = 4
PEER_OUT_TOKENS_PER_TRIP = 2
VMEM_LIMIT = 56 * 1024 * 1024

ROW_TILE = 256
INPROJ_ROW_TILE = 128
ROUTE_TILE = 256
SCAN_CHUNKS_PER_TRIP = 16


def _resident(shape):
    return pl.BlockSpec(shape, lambda *_: (0,) * len(shape), pipeline_mode=pl.Buffered(1))


def _params(n_axes):
    return pltpu.CompilerParams(dimension_semantics=("arbitrary",) * n_axes, vmem_limit_bytes=VMEM_LIMIT)


def _split3(a):
    p1 = a.astype(BF16)
    r1 = a - p1.astype(F32)
    p2 = r1.astype(BF16)
    p3 = (r1 - p2.astype(F32)).astype(BF16)
    return p1, p2, p3


def _dot(a, b):
    return jnp.dot(a, b, preferred_element_type=F32)


def _dot_exact_rhs(a, rhs_bf16):
    p1, p2, p3 = _split3(a)
    return _dot(p1, rhs_bf16) + _dot(p2, rhs_bf16) + _dot(p3, rhs_bf16)


def _rms(x):
    return x * lax.rsqrt(jnp.mean(x * x, axis=-1, keepdims=True) + EPS)


def _gelu(x):
    return 0.5 * x * (1.0 + lax.erf(x * np.float32(2.0 ** -0.5)))


def _sigmoid(x):
    return 1.0 / (1.0 + jnp.exp(-x))


def _log_sigmoid(x):
    return jnp.minimum(x, 0.0) - jnp.log1p(jnp.exp(-jnp.abs(x)))


def _adaln_body(c_ref, w_ref, b_ref, o_ref):
    c = c_ref[...]
    s = (c * _sigmoid(c)).astype(BF16)
    o_ref[...] = _dot(s, w_ref[...].astype(BF16)) + b_ref[...]


def _adaln(cvec, w, b):
    rows, d = cvec.shape
    n_out = w.shape[1]
    return pl.pallas_call(
        _adaln_body,
        grid=(n_out // d,),
        in_specs=[pl.BlockSpec((rows, d), lambda j: (0, 0)),
                  pl.BlockSpec((d, d), lambda j: (0, j)),
                  pl.BlockSpec((1, d), lambda j: (0, j))],
        out_specs=pl.BlockSpec((rows, d), lambda j: (0, j)),
        out_shape=jax.ShapeDtypeStruct((rows, n_out), F32),
        compiler_params=_params(1),
        name="adaln",
    )(cvec, w, b.reshape(1, n_out))


def _inproj_body(x_ref, shift_ref, scale_ref, g_ref, w_ref, wrank_ref, brank_ref,
                 hq_ref, zf_ref, zb_ref, hi_ref, hgate_ref, gq_ref, gk_ref, gv_ref, gg_ref,
                 glf_ref, glb_ref, mhg_ref, mgla_ref):
    h = _rms(x_ref[0]) * g_ref[...] * (1.0 + scale_ref[0]) + shift_ref[0]
    hb = h.astype(BF16)

    def cols(start, width):
        return _dot(hb, w_ref[:, start:start + width])

    hq_ref[0] = cols(C_HQ, HG_KEY) * np.float32(HEAD_K ** -0.5)
    zf_ref[0] = cols(C_ZF, HG_KEY)
    zb_ref[0] = cols(C_ZB, HG_KEY)
    hi_ref[0] = cols(C_HI, HG_VAL)
    hgate_ref[0] = cols(C_HGATE, HG_VAL)
    gq_ref[0] = cols(C_GQ, GLA_KEY) * np.float32(HEAD_K ** -0.5)
    gk_ref[0] = cols(C_GK, GLA_KEY)
    gv_ref[0] = cols(C_GV, GLA_VAL)
    gg_ref[0] = cols(C_GG, GLA_VAL)
    mhg_ref[0] = cols(C_MHG, D_MODEL)
    mgla_ref[0] = cols(C_MGLA, D_MODEL)
    rank = cols(C_RANK, 2 * GLA_GATE_RANK).astype(BF16)
    pre = _dot(rank, wrank_ref[...]) + brank_ref[...]
    lg = _log_sigmoid(pre) / GLA_GATE_NORMALIZER
    glf_ref[0] = lg[:, :GLA_KEY]
    glb_ref[0] = lg[:, GLA_KEY:]


def _inproj(x, shift, scale, g, w_bf16, wrank, brank):
    bsz, t, d = x.shape
    tm = INPROJ_ROW_TILE
    widths = (HG_KEY, HG_KEY, HG_KEY, HG_VAL, HG_VAL, GLA_KEY, GLA_KEY, GLA_VAL, GLA_VAL,
              GLA_KEY, GLA_KEY, D_MODEL, D_MODEL)
    row = lambda w: pl.BlockSpec((1, tm, w), lambda b, i: (b, i, 0))
    per_batch = pl.BlockSpec((1, 1, d), lambda b, i: (b, 0, 0))
    return pl.pallas_call(
        _inproj_body,
        grid=(bsz, t // tm),
        in_specs=[row(d), per_batch, per_batch, _resident((1, d)), _resident(w_bf16.shape),
                  _resident(wrank.shape), _resident(brank.shape)],
        out_specs=[row(w) for w in widths],
        out_shape=[jax.ShapeDtypeStruct((bsz, t, w), F32) for w in widths],
        compiler_params=_params(2),
        name="inproj",
    )(x, shift, scale, g, w_bf16, wrank, brank)


def _cumulative_decays(lgs, tri3):
    wide = jnp.concatenate(lgs, axis=1)
    b = _dot(tri3, jnp.concatenate(_split3(wide), axis=0))
    return [b[:, j * HEAD_K:(j + 1) * HEAD_K] for j in range(len(lgs))]


def _chunk_step(q, k, b, v, state, mask, ref_row, last_row, want_out):
    b_ref = b[ref_row:ref_row + 1]
    b_last = b[last_row:last_row + 1]
    vb = v.astype(BF16)
    o = None
    if want_out:
        qd = (q * jnp.exp(b - b_ref)).astype(BF16)
        kd = (k * jnp.exp(b_ref - b)).astype(BF16)
        scores = lax.dot_general(qd, kd, (((1,), (1,)), ((), ())), preferred_element_type=F32)
        scores = jnp.where(mask, scores, 0.0).astype(BF16)
        qs = (q * jnp.exp(b)).astype(BF16)
        o = _dot(scores, vb) + lax.dot_general(qs, state.astype(BF16), (((1,), (1,)), ((), ())),
                                               preferred_element_type=F32)
    kl = (k * jnp.exp(b_last - b)).astype(BF16)
    new_state = state * jnp.exp(b_last) + lax.dot_general(vb, kl, (((0,), (0,)), ((), ())),
                                                          preferred_element_type=F32)
    return o, new_state


def _scan_core(load_fwd, load_bwd, load_ctx_fwd, load_ctx_bwd, gate_ref, g_ref, y_ref,
               of_scr, ob_scr, sf_scr, sb_scr, n_chunks, n_ctx_chunks, dv):
    c = GRID_W
    ri = lax.broadcasted_iota(jnp.int32, (c, c), 0)
    ci = lax.broadcasted_iota(jnp.int32, (c, c), 1)
    lower = ci <= ri
    upper = ci >= ri
    tri_f = jnp.concatenate([lower.astype(BF16)] * 3, axis=1)
    tri_b = jnp.concatenate([upper.astype(BF16)] * 3, axis=1)
    fwd = dict(mask=lower, ref_row=c // 2, last_row=c - 1)
    bwd = dict(mask=upper, ref_row=c - 1 - c // 2, last_row=0)

    def run(load, rows, tri3, state, out_scr, kind):
        loaded = [load(r) for r in rows]
        bs = _cumulative_decays([lg for _, _, lg, _ in loaded], tri3)
        for r, (q, k, _, v), b in zip(rows, loaded, bs):
            o, state = _chunk_step(q, k, b, v, state, want_out=out_scr is not None, **kind)
            if out_scr is not None:
                out_scr[pl.ds(r, c), :] = o
        return state

    zero = jnp.zeros((dv, HEAD_K), F32)
    ctx_rows = [i * CTX_CHUNK for i in range(n_ctx_chunks)]
    sf_scr[...] = run(load_ctx_fwd, ctx_rows, tri_f, zero, None, fwd)
    sb_scr[...] = run(load_ctx_bwd, ctx_rows[::-1], tri_b, zero, None, bwd)

    per_trip = math.gcd(n_chunks, SCAN_CHUNKS_PER_TRIP)

    def step(i, carry):
        ns = [i * per_trip + j for j in range(per_trip)]
        rows_f = [pl.multiple_of(n * c, c) for n in ns]
        rows_b = [pl.multiple_of((n_chunks - 1 - n) * c, c) for n in ns]
        sf_scr[...] = run(load_fwd, rows_f, tri_f, sf_scr[...], of_scr, fwd)
        sb_scr[...] = run(load_bwd, rows_b, tri_b, sb_scr[...], ob_scr, bwd)
        return carry

    lax.fori_loop(0, n_chunks // per_trip, step, 0)

    o = of_scr[...] + ob_scr[...]
    gate = gate_ref[0]
    y_ref[0] = _rms(o) * g_ref[...] * (gate * _sigmoid(gate))


def _hgrn_scan_body(q_ref, zf_ref, zb_ref, v_ref, gate_ref, czf_ref, czb_ref, cv_ref, lb_ref, g_ref,
                    y_ref, of_scr, ob_scr, sf_scr, sb_scr, *, n_chunks, n_ctx_chunks):
    def forget(z, lower):
        f = lower + (1.0 - lower) * _sigmoid(z)
        return 1.0 - f, jnp.log(f)

    def loader(q_r, z_r, v_r, row):
        lower = lb_ref[row:row + 1, :]

        def load(rows):
            sl = pl.ds(rows, GRID_W)
            k, lg = forget(z_r[0, sl, :], lower)
            q = q_r[0, sl, :] if q_r is not None else None
            return q, k, lg, v_r[0, sl, :]
        return load

    _scan_core(loader(q_ref, zf_ref, v_ref, 0), loader(q_ref, zb_ref, v_ref, 1),
               loader(None, czf_ref, cv_ref, 0), loader(None, czb_ref, cv_ref, 1),
               gate_ref, g_ref, y_ref, of_scr, ob_scr, sf_scr, sb_scr, n_chunks, n_ctx_chunks, HG_HEAD_V)


def _gla_scan_body(q_ref, k_ref, lf_ref, lbk_ref, v_ref, gate_ref, ck_ref, clf_ref, clb_ref, cv_ref, g_ref,
                   y_ref, of_scr, ob_scr, sf_scr, sb_scr, *, n_chunks, n_ctx_chunks):
    def loader(q_r, k_r, l_r, v_r):
        def load(rows):
            sl = pl.ds(rows, GRID_W)
            q = q_r[0, sl, :] if q_r is not None else None
            return q, k_r[0, sl, :], l_r[0, sl, :], v_r[0, sl, :]
        return load

    _scan_core(loader(q_ref, k_ref, lf_ref, v_ref), loader(q_ref, k_ref, lbk_ref, v_ref),
               loader(None, ck_ref, clf_ref, cv_ref), loader(None, ck_ref, clb_ref, cv_ref),
               gate_ref, g_ref, y_ref, of_scr, ob_scr, sf_scr, sb_scr, n_chunks, n_ctx_chunks, GLA_HEAD_V)


def _scan_call(body, n_heads, dv, lat, ctx, small, bsz, t, t_ctx, name):
    head = lambda rows, w: pl.BlockSpec((1, rows, w), lambda b, h: (b, 0, h))
    in_specs = ([head(t, w) for _, w in lat] + [head(t_ctx, w) for _, w in ctx]
                + [pl.BlockSpec(bs, im) for _, bs, im in small])
    return pl.pallas_call(
        functools.partial(body, n_chunks=t // GRID_W, n_ctx_chunks=t_ctx // CTX_CHUNK),
        grid=(bsz, n_heads),
        in_specs=in_specs,
        out_specs=head(t, dv),
        out_shape=jax.ShapeDtypeStruct((bsz, t, n_heads * dv), F32),
        scratch_shapes=[pltpu.VMEM((t, dv), F32), pltpu.VMEM((t, dv), F32),
                        pltpu.VMEM((dv, HEAD_K), F32), pltpu.VMEM((dv, HEAD_K), F32)],
        compiler_params=_params(2),
        name=name,
    )(*[a for a, _ in lat], *[a for a, _ in ctx], *[a for a, _, _ in small])


def _mixer_out_body(yh_ref, yg_ref, mh_ref, mg_ref, x_ref, gate1_ref, shift2_ref, scale2_ref, g2_ref,
                    wbh_ref, wbg_ref, wo_ref, wq_ref, k1_ref, k2_ref,
                    x1_ref, h2_ref, st_ref):
    yh = _dot(yh_ref[0].astype(BF16), wbh_ref[...])
    yg = _dot(yg_ref[0].astype(BF16), wbg_ref[...])
    y = _sigmoid(mh_ref[0]) * yh + _sigmoid(mg_ref[0]) * yg
    x1 = x_ref[0] + gate1_ref[0] * _dot(y.astype(BF16), wo_ref[...])
    x1_ref[0] = x1
    h2 = _rms(x1) * g2_ref[...] * (1.0 + scale2_ref[0]) + shift2_ref[0]
    h2_ref[0] = h2
    q = _dot(h2.astype(BF16), wq_ref[...]).astype(BF16)
    for hh in range(2 * PEER_HEADS):
        keys = k1_ref if hh % 2 == 0 else k2_ref
        st_ref[hh] = lax.dot_general(keys[...], q[:, hh * PEER_SUB_DIM:(hh + 1) * PEER_SUB_DIM],
                                     (((1,), (1,)), ((), ())), preferred_element_type=F32)


def _mixer_out(y_hg, y_gla, m_hg, m_gla, x, gate1, shift2, scale2, g2, wbh, wbg, wo, wq, k1, k2):
    bsz, t, d = x.shape
    tm = ROW_TILE
    row = pl.BlockSpec((1, tm, d), lambda b, i: (b, i, 0))
    per_batch = pl.BlockSpec((1, 1, d), lambda b, i: (b, 0, 0))
    tiles_per_batch = t // tm
    n = bsz * t
    return pl.pallas_call(
        _mixer_out_body,
        grid=(bsz, tiles_per_batch),
        in_specs=[row, row, row, row, row, per_batch, per_batch, per_batch, _resident((1, d)),
                  _resident(wbh.shape), _resident(wbg.shape), _resident(wo.shape), _resident(wq.shape),
                  _resident(k1.shape), _resident(k2.shape)],
        out_specs=[row, row,
                   pl.BlockSpec((2 * PEER_HEADS, PEER_N_KEYS, tm), lambda b, i: (0, 0, b * tiles_per_batch + i))],
        out_shape=[jax.ShapeDtypeStruct((bsz, t, d), F32), jax.ShapeDtypeStruct((bsz, t, d), F32),
                   jax.ShapeDtypeStruct((2 * PEER_HEADS, PEER_N_KEYS, n), F32)],
        compiler_params=_params(2),
        name="mixer_out",
    )(y_hg, y_gla, m_hg, m_gla, x, gate1, shift2, scale2, g2, wbh, wbg, wo, wq, k1, k2)


_CAND_GROUPS = (("a", 0, 0), ("a", 0, 8), ("a", 1, 0), ("b", 0, 8),
                ("b", 0, 0), ("b", 1, 0), ("b", 2, 0), ("b", 3, 0), ("b", 4, 0))
_CAND_ROWS = len(_CAND_GROUPS) * SUBLANES


def _candidate_tables():
    flat = np.full((_CAND_ROWS, 1), -1, np.int32)
    seen = set()
    for g, (kind, fixed, start) in enumerate(_CAND_GROUPS):
        for r in range(SUBLANES):
            a, b = (fixed, start + r) if kind == "a" else (start + r, fixed)
            if (a + 1) * (b + 1) <= PEER_TOPK and (a, b) not in seen:
                seen.add((a, b))
                flat[g * SUBLANES + r, 0] = a * PEER_TOPK + b
    needed = {(a, b) for a in range(PEER_TOPK) for b in range(PEER_TOPK) if (a + 1) * (b + 1) <= PEER_TOPK}
    assert seen == needed
    return flat


def _extract_top(x, key, rounds):
    big = np.int32(2 ** 30)
    vals, keys = [], []
    for _ in range(rounds):
        m = jnp.max(x, axis=0, keepdims=True)
        sel = jnp.min(jnp.where(x == m, key, big), axis=0, keepdims=True)
        vals.append(m)
        keys.append(sel)
        x = jnp.where(key == sel, -jnp.inf, x)
    return vals, keys


def _pick_row(table, row_ids, sel):
    return jnp.sum(jnp.where(row_ids == sel, table, 0), axis=0, keepdims=True)


def _route_body(st_ref, flat_ref, idx_ref, gate_ref):
    tn = st_ref.shape[-1]
    key_ids = lax.broadcasted_iota(jnp.int32, (PEER_N_KEYS, tn), 0)
    rank_ids = lax.broadcasted_iota(jnp.int32, (PEER_TOPK, tn), 0)
    flat = jnp.broadcast_to(flat_ref[...], (_CAND_ROWS, tn))
    valid = flat >= 0
    experts, gates = [], []
    for h in range(PEER_HEADS):
        v1, i1 = _extract_top(st_ref[2 * h], key_ids, PEER_TOPK)
        v2, i2 = _extract_top(st_ref[2 * h + 1], key_ids, PEER_TOPK)
        v1c, i1c = jnp.concatenate(v1, axis=0), jnp.concatenate(i1, axis=0)
        v2c, i2c = jnp.concatenate(v2, axis=0), jnp.concatenate(i2, axis=0)
        cand = jnp.concatenate(
            [v1[fixed] + v2c[start:start + SUBLANES] if kind == "a" else v1c[start:start + SUBLANES] + v2[fixed]
             for kind, fixed, start in _CAND_GROUPS], axis=0)
        cand = jnp.where(valid, cand, -jnp.inf)
        top_s, top_c = _extract_top(cand, flat, PEER_TOPK)
        m = top_s[0]
        ex = [jnp.exp(s - m) for s in top_s]
        denom = functools.reduce(lambda a, b: a + b, ex)
        for s_k, c_k in zip(ex, top_c):
            a = c_k >> 4
            b = c_k & (PEER_TOPK - 1)
            e = _pick_row(i1c, rank_ids, a) * PEER_N_KEYS + _pick_row(i2c, rank_ids, b)
            experts.append(e * ROWS_PER_EXPERT)
            gates.append(s_k / denom)
    idx_ref[...] = jnp.concatenate(experts, axis=0).T
    gate_ref[...] = jnp.concatenate(gates, axis=0).T


def _route(scores_t):
    n = scores_t.shape[-1]
    tn = ROUTE_TILE
    flat = jnp.asarray(_candidate_tables())
    return pl.pallas_call(
        _route_body,
        grid=(n // tn,),
        in_specs=[pl.BlockSpec((2 * PEER_HEADS, PEER_N_KEYS, tn), lambda i: (0, 0, i)),
                  pl.BlockSpec(flat.shape, lambda i: (0, 0))],
        out_specs=[pl.BlockSpec((tn, N_SEL), lambda i: (i, 0)), pl.BlockSpec((tn, N_SEL), lambda i: (i, 0))],
        out_shape=[jax.ShapeDtypeStruct((n, N_SEL), jnp.int32), jax.ShapeDtypeStruct((n, N_SEL), F32)],
        compiler_params=_params(1),
        name="peer_route",
    )(scores_t, flat)


def _pack_table(w):
    e = w.shape[0]
    bits = lax.bitcast_convert_type(w.astype(BF16), jnp.uint16).astype(jnp.uint32)
    word = bits[:, :HALF] | (bits[:, HALF:] << 16)
    return lax.bitcast_convert_type(word, jnp.int32).reshape(e * ROWS_PER_EXPERT, LANES)


def _pair_matrix():
    r = lax.broadcasted_iota(jnp.int32, (N_SEL, 2 * N_SEL), 0)
    c = lax.broadcasted_iota(jnp.int32, (N_SEL, 2 * N_SEL), 1)
    return ((c >> 1) == r).astype(BF16)


def _load_table_once(tab_hbm, tab, sem, lhs):
    @pl.when(pl.program_id(0) == 0)
    def _():
        cp = pltpu.make_async_copy(tab_hbm, tab, sem)
        cp.start()
        lhs[...] = jnp.zeros_like(lhs)
        cp.wait()


def _stage_rows(lhs, chunk, row, value):
    lhs.at[chunk][pl.ds(row, value.shape[0], stride=SUBLANES), :] = value


def _token_lhs(lhs, t, chunks):
    sl = pl.ds(pl.multiple_of(t * SUBLANES, SUBLANES), SUBLANES)
    return jnp.concatenate([lhs[s, sl, :] for s in chunks], axis=1).astype(BF16)


def _gather_token(idx_ref, t, tab, tile):
    for e in range(N_SEL):
        row = pl.multiple_of(idx_ref[t, e], ROWS_PER_EXPERT)
        tile[pl.ds(e, ROWS_PER_EXPERT, stride=TILE_STRIDE), :] = tab[pl.ds(row, ROWS_PER_EXPERT), :]


def _tile_chunk(tile, s):
    return pltpu.bitcast(tile[s * TILE_STRIDE:s * TILE_STRIDE + N_SEL, :], BF16)


def _pipelined_tokens(idx_ref, tab, tiles, contract, finish):
    tb = PEER_TOKENS_PER_STEP
    per_trip = len(tiles) // 2
    set_a, set_b = tiles[:per_trip], tiles[per_trip:]

    def gather(t, tile):
        _gather_token(idx_ref, jnp.minimum(t, tb - 1), tab, tile)

    def trip(t, ready, free):
        for j in range(per_trip):
            result = contract(t + j, ready[j])
            gather(t + per_trip + j, free[j])
            finish(t + j, result)

    for j in range(per_trip):
        gather(j, set_a[j])

    def step(k, carry):
        t = per_trip * k

        @pl.when(k % 2 == 0)
        def _():
            trip(t, set_a, set_b)

        @pl.when(k % 2 == 1)
        def _():
            trip(t, set_b, set_a)

        return carry

    lax.fori_loop(0, tb // per_trip, step, 0)


def _peer_act_body(idx_ref, x_ref, tab_hbm, o_ref, tab, sem, lhs, part, *tiles):
    _load_table_once(tab_hbm, tab, sem, lhs)
    x = x_ref[...]
    hi = x.astype(BF16).astype(F32)
    lo = (x - hi).astype(BF16).astype(F32)
    for s in range(ROWS_PER_EXPERT):
        low, high = slice(s * LANES, (s + 1) * LANES), slice(HALF + s * LANES, HALF + (s + 1) * LANES)
        _stage_rows(lhs, s, 0, hi[:, low])
        _stage_rows(lhs, s, 1, lo[:, low])
        _stage_rows(lhs, s, 2, hi[:, high])
        _stage_rows(lhs, s, 3, lo[:, high])
    lane = lax.broadcasted_iota(jnp.int32, (1, 2 * N_SEL), 1)
    even = (lane & 1) == 0

    def contract(t, tile):
        acc = jnp.zeros((SUBLANES, 2 * N_SEL), F32)
        for s in range(ROWS_PER_EXPERT):
            acc = acc + lax.dot_general(_token_lhs(lhs, t, (s,)), _tile_chunk(tile, s),
                                        (((1,), (1,)), ((), ())), preferred_element_type=F32)
        return acc

    def finish(t, acc):
        part[pl.ds(t, 1), :] = jnp.where(even, acc[0:1] + acc[1:2], acc[2:3] + acc[3:4])

    _pipelined_tokens(idx_ref, tab, tiles, contract, finish)
    pair_t = _pair_matrix_t()
    o_ref[...] = _dot_exact_rhs(part[...], pair_t)


def _pair_matrix_t():
    r = lax.broadcasted_iota(jnp.int32, (2 * N_SEL, N_SEL), 0)
    c = lax.broadcasted_iota(jnp.int32, (2 * N_SEL, N_SEL), 1)
    return ((r >> 1) == c).astype(BF16)


def _peer_out_body(idx_ref, act_ref, gate_ref, x1_ref, gate2_ref, fg_ref, tab_hbm, o_ref, tab, sem, lhs, acc_scr,
                   *tiles):
    _load_table_once(tab_hbm, tab, sem, lhs)
    coef = gate_ref[...] * _gelu(act_ref[...])
    wide = _dot_exact_rhs(coef, _pair_matrix())
    hi = wide.astype(BF16).astype(F32)
    lo = (wide - hi).astype(BF16).astype(F32)
    lane = lax.broadcasted_iota(jnp.int32, (1, 2 * N_SEL), 1)
    even = (lane & 1) == 0
    for s in range(2):
        sl = slice(s * LANES, (s + 1) * LANES)
        ev = even[:, sl]
        _stage_rows(lhs, s, 0, jnp.where(ev, hi[:, sl], 0.0))
        _stage_rows(lhs, s, 1, jnp.where(ev, lo[:, sl], 0.0))
        _stage_rows(lhs, s, 2, jnp.where(ev, 0.0, hi[:, sl]))
        _stage_rows(lhs, s, 3, jnp.where(ev, 0.0, lo[:, sl]))

    def contract(t, tile):
        coef_rows = _token_lhs(lhs, t, (0, 1))
        return [_dot(coef_rows, _tile_chunk(tile, s)) for s in range(ROWS_PER_EXPERT)]

    def finish(t, res):
        acc_scr[pl.ds(t, 1), :] = jnp.concatenate([r[0:1] + r[1:2] for r in res] + [r[2:3] + r[3:4] for r in res],
                                                   axis=1)

    _pipelined_tokens(idx_ref, tab, tiles, contract, finish)
    x2 = x1_ref[...] + gate2_ref[0] * acc_scr[...]
    o_ref[...] = _rms(x2) * fg_ref[...]


def _peer_scratch(table, lhs_chunks, extra, tokens_per_trip):
    tb = PEER_TOKENS_PER_STEP
    return ([pltpu.VMEM(table.shape, jnp.int32), pltpu.SemaphoreType.DMA(()),
             pltpu.VMEM((lhs_chunks, tb * SUBLANES, LANES), F32), extra]
            + [pltpu.VMEM((ROWS_PER_EXPERT * TILE_STRIDE, LANES), jnp.int32) for _ in range(2 * tokens_per_trip)])


def _peer_act(idx, h2, table):
    n, d = h2.shape
    tb = PEER_TOKENS_PER_STEP
    return pl.pallas_call(
        _peer_act_body,
        grid=(n // tb,),
        in_specs=[pl.BlockSpec((tb, N_SEL), lambda i: (i, 0), memory_space=pltpu.SMEM),
                  pl.BlockSpec((tb, d), lambda i: (i, 0)),
                  pl.BlockSpec(memory_space=pl.ANY)],
        out_specs=pl.BlockSpec((tb, N_SEL), lambda i: (i, 0)),
        out_shape=jax.ShapeDtypeStruct((n, N_SEL), F32),
        scratch_shapes=_peer_scratch(table, ROWS_PER_EXPERT, pltpu.VMEM((tb, 2 * N_SEL), F32),
                                     PEER_ACT_TOKENS_PER_TRIP),
        compiler_params=_params(1),
        name="peer_act",
    )(idx, h2, table)


def _peer_out(idx, act, gate, x1, gate2, final_g, table, tokens_per_batch):
    n, d = x1.shape
    tb = PEER_TOKENS_PER_STEP
    steps_per_batch = tokens_per_batch // tb
    row = lambda w: pl.BlockSpec((tb, w), lambda i: (i, 0))
    return pl.pallas_call(
        _peer_out_body,
        grid=(n // tb,),
        in_specs=[pl.BlockSpec((tb, N_SEL), lambda i: (i, 0), memory_space=pltpu.SMEM),
                  row(N_SEL), row(N_SEL), row(d),
                  pl.BlockSpec((1, 1, d), lambda i: (i // steps_per_batch, 0, 0)),
                  pl.BlockSpec((1, d), lambda i: (0, 0)),
                  pl.BlockSpec(memory_space=pl.ANY)],
        out_specs=row(d),
        out_shape=jax.ShapeDtypeStruct((n, d), F32),
        scratch_shapes=_peer_scratch(table, 2, pltpu.VMEM((tb, d), F32), PEER_OUT_TOKENS_PER_TRIP),
        compiler_params=_params(1),
        name="peer_out",
    )(idx, act, gate, x1, gate2, final_g, table)


def kernel(x, c, ctx, c_ctx, ada_w, ada_b, norm_mix_g, w_in, hgrn_lb_logits, hgrn_norm_g, gla_gk_w, gla_gk_b, gla_norm_g, w_branch_hgrn, w_branch_gla, w_out, norm_ffn_g, peer_wq, peer_k1, peer_k2, peer_u, peer_v, final_g):
    bsz, seq, d = x.shape
    t_ctx = ctx.shape[1]
    depth = ada_w.shape[0]
    assert depth == 1 and d == D_MODEL and w_in.shape[-1] == D_IN
    l = 0
    row2 = lambda a: a.reshape(1, -1)

    lower = jnp.cumsum(jax.nn.softmax(hgrn_lb_logits, axis=0), axis=0)[l]
    lower = jnp.concatenate([lower, jnp.zeros((SUBLANES - 2, HG_KEY), F32)], axis=0)
    cvec = jnp.concatenate([c, c_ctx[None], jnp.zeros((SUBLANES - 1, d), F32)], axis=0)
    mod_all = _adaln(cvec, ada_w[l], ada_b[l])
    mod = [m.reshape(bsz, 1, d) for m in jnp.split(mod_all[:bsz], N_MOD, axis=-1)]
    mod_c = [jnp.broadcast_to(m.reshape(1, 1, d), (bsz, 1, d)) for m in jnp.split(mod_all[bsz:bsz + 1], N_MOD, axis=-1)]
    w_in_b = w_in[l].astype(BF16)
    zr = jnp.zeros((GLA_GATE_RANK, GLA_KEY), F32)
    wrank = jnp.concatenate([jnp.concatenate([gla_gk_w[l, 0], zr], axis=1),
                             jnp.concatenate([zr, gla_gk_w[l, 1]], axis=1)], axis=0).astype(BF16)
    brank = gla_gk_b[l].reshape(1, 2 * GLA_KEY)

    lat = _inproj(x, mod[0], mod[1], row2(norm_mix_g[l]), w_in_b, wrank, brank)
    cx = _inproj(ctx, mod_c[0], mod_c[1], row2(norm_mix_g[l]), w_in_b, wrank, brank)
    (hq, zf, zb, hi, hgate, gq, gk, gv, gg, glf, glb, m_hg, m_gla) = lat
    (_, czf, czb, chi, _, _, cgk, cgv, _, cglf, cglb, _, _) = cx

    y_hg = _scan_call(
        _hgrn_scan_body, HG_HEADS, HG_HEAD_V,
        [(hq, HEAD_K), (zf, HEAD_K), (zb, HEAD_K), (hi, HG_HEAD_V), (hgate, HG_HEAD_V)],
        [(czf, HEAD_K), (czb, HEAD_K), (chi, HG_HEAD_V)],
        [(lower, (SUBLANES, HEAD_K), lambda b, h: (0, h)), (row2(hgrn_norm_g[l]), (1, HG_HEAD_V), lambda b, h: (0, 0))],
        bsz, seq, t_ctx, "hgrn_scan")
    y_gla = _scan_call(
        _gla_scan_body, GLA_HEADS, GLA_HEAD_V,
        [(gq, HEAD_K), (gk, HEAD_K), (glf, HEAD_K), (glb, HEAD_K), (gv, GLA_HEAD_V), (gg, GLA_HEAD_V)],
        [(cgk, HEAD_K), (cglf, HEAD_K), (cglb, HEAD_K), (cgv, GLA_HEAD_V)],
        [(row2(gla_norm_g[l]), (1, GLA_HEAD_V), lambda b, h: (0, 0))],
        bsz, seq, t_ctx, "gla_scan")

    x1, h2, scores_t = _mixer_out(
        y_hg, y_gla, m_hg, m_gla, x, mod[2], mod[3], mod[4], row2(norm_ffn_g[l]),
        w_branch_hgrn[l].astype(BF16), w_branch_gla[l].astype(BF16), w_out[l].astype(BF16),
        peer_wq[l].astype(BF16), peer_k1[l].astype(BF16), peer_k2[l].astype(BF16))

    idx, gate = _route(scores_t)
    n = bsz * seq
    act = _peer_act(idx, h2.reshape(n, d), _pack_table(peer_u[l]))
    out = _peer_out(idx, act, gate, x1.reshape(n, d), mod[5], row2(final_g), _pack_table(peer_v[l]), seq)
    return out.reshape(bsz, seq, d)
```

```python
import functools
import math

import jax
import jax.numpy as jnp
import numpy as np
from jax import lax
from jax.experimental import pallas as pl
from jax.experimental.pallas import tpu as pltpu

F32 = jnp.float32
BF16 = jnp.bfloat16

D_MODEL = 1024
GRID_W = 64
CTX_CHUNK = 64
N_MOD = 6
EPS = 1e-6

HG_HEADS = 8
HEAD_K = 128
HG_KEY = HG_HEADS * HEAD_K
HG_VAL = D_MODEL
HG_HEAD_V = HG_VAL // HG_HEADS
GLA_HEADS = 4
GLA_KEY = D_MODEL // 2
GLA_VAL = D_MODEL
GLA_HEAD_V = GLA_VAL // GLA_HEADS
GLA_GATE_RANK = 16
GLA_GATE_NORMALIZER = 16.0

C_HQ, C_ZF, C_ZB, C_HI, C_HGATE = 0, 1024, 2048, 3072, 4096
C_GQ, C_GK, C_GV, C_GG = 5120, 5632, 6144, 7168
C_RANK = 8192
C_MHG = C_RANK + 2 * GLA_GATE_RANK
C_MGLA = C_MHG + D_MODEL
D_IN = C_MGLA + D_MODEL

PEER_HEADS = 8
PEER_N_KEYS = 128
PEER_SUB_DIM = 128
PEER_TOPK = 16
N_SEL = PEER_HEADS * PEER_TOPK

SUBLANES = 8
VMEM_LIMIT = 56 * 1024 * 1024

ROW_TILE = 256
INPROJ_ROW_TILE = 256
ROUTE_TILE = 256
PEER_ROW_TILE = 512
PEER_EXPERT_TILE = 512
PEER_GATE_TOKENS_PER_TRIP = 8
SCAN_CHUNKS_PER_TRIP = 16


def _resident(shape):
    return pl.BlockSpec(shape, lambda *_: (0,) * len(shape), pipeline_mode=pl.Buffered(1))


def _params(n_axes):
    return pltpu.CompilerParams(dimension_semantics=("arbitrary",) * n_axes, vmem_limit_bytes=VMEM_LIMIT)


def _split3(a):
    p1 = a.astype(BF16)
    r1 = a - p1.astype(F32)
    p2 = r1.astype(BF16)
    p3 = (r1 - p2.astype(F32)).astype(BF16)
    return p1, p2, p3


def _dot(a, b):
    return jnp.dot(a, b, preferred_element_type=F32)


def _rms(x):
    return x * lax.rsqrt(jnp.mean(x * x, axis=-1, keepdims=True) + EPS)


def _gelu(x):
    return 0.5 * x * (1.0 + lax.erf(x * np.float32(2.0 ** -0.5)))


def _sigmoid(x):
    return 1.0 / (1.0 + jnp.exp(-x))


def _log_sigmoid(x):
    return jnp.minimum(x, 0.0) - jnp.log1p(jnp.exp(-jnp.abs(x)))


def _adaln_body(c_ref, w_ref, b_ref, o_ref):
    c = c_ref[...]
    s = (c * _sigmoid(c)).astype(BF16)
    o_ref[...] = _dot(s, w_ref[...].astype(BF16)) + b_ref[...]


def _adaln(cvec, w, b):
    rows, d = cvec.shape
    n_out = w.shape[1]
    return pl.pallas_call(
        _adaln_body,
        grid=(n_out // d,),
        in_specs=[pl.BlockSpec((rows, d), lambda j: (0, 0)),
                  pl.BlockSpec((d, d), lambda j: (0, j)),
                  pl.BlockSpec((1, d), lambda j: (0, j))],
        out_specs=pl.BlockSpec((rows, d), lambda j: (0, j)),
        out_shape=jax.ShapeDtypeStruct((rows, n_out), F32),
        compiler_params=_params(1),
        name="adaln",
    )(cvec, w, b.reshape(1, n_out))


def _inproj_body(x_ref, shift_ref, scale_ref, g_ref, w_ref, wrank_ref, brank_ref,
                 hq_ref, zf_ref, zb_ref, hi_ref, hgate_ref, gq_ref, gk_ref, gv_ref, gg_ref,
                 glf_ref, glb_ref, mhg_ref, mgla_ref):
    h = _rms(x_ref[0]) * g_ref[...] * (1.0 + scale_ref[0]) + shift_ref[0]
    hb = h.astype(BF16)

    def cols(start, width):
        return _dot(hb, w_ref[:, start:start + width])

    hq_ref[0] = cols(C_HQ, HG_KEY) * np.float32(HEAD_K ** -0.5)
    zf_ref[0] = cols(C_ZF, HG_KEY)
    zb_ref[0] = cols(C_ZB, HG_KEY)
    hi_ref[0] = cols(C_HI, HG_VAL)
    hgate_ref[0] = cols(C_HGATE, HG_VAL)
    gq_ref[0] = cols(C_GQ, GLA_KEY) * np.float32(HEAD_K ** -0.5)
    gk_ref[0] = cols(C_GK, GLA_KEY)
    gv_ref[0] = cols(C_GV, GLA_VAL)
    gg_ref[0] = cols(C_GG, GLA_VAL)
    mhg_ref[0] = cols(C_MHG, D_MODEL)
    mgla_ref[0] = cols(C_MGLA, D_MODEL)
    rank = cols(C_RANK, 2 * GLA_GATE_RANK).astype(BF16)
    pre = _dot(rank, wrank_ref[...]) + brank_ref[...]
    lg = _log_sigmoid(pre) / GLA_GATE_NORMALIZER
    glf_ref[0] = lg[:, :GLA_KEY]
    glb_ref[0] = lg[:, GLA_KEY:]


def _inproj(x, shift, scale, g, w_bf16, wrank, brank):
    bsz, t, d = x.shape
    tm = INPROJ_ROW_TILE
    widths = (HG_KEY, HG_KEY, HG_KEY, HG_VAL, HG_VAL, GLA_KEY, GLA_KEY, GLA_VAL, GLA_VAL,
              GLA_KEY, GLA_KEY, D_MODEL, D_MODEL)
    row = lambda w: pl.BlockSpec((1, tm, w), lambda b, i: (b, i, 0))
    per_batch = pl.BlockSpec((1, 1, d), lambda b, i: (b, 0, 0))
    return pl.pallas_call(
        _inproj_body,
        grid=(bsz, t // tm),
        in_specs=[row(d), per_batch, per_batch, _resident((1, d)), _resident(w_bf16.shape),
                  _resident(wrank.shape), _resident(brank.shape)],
        out_specs=[row(w) for w in widths],
        out_shape=[jax.ShapeDtypeStruct((bsz, t, w), F32) for w in widths],
        compiler_params=_params(2),
        name="inproj",
    )(x, shift, scale, g, w_bf16, wrank, brank)


def _cumulative_decays(lgs, tri3):
    wide = jnp.concatenate(lgs, axis=1)
    b = _dot(tri3, jnp.concatenate(_split3(wide), axis=0))
    return [b[:, j * HEAD_K:(j + 1) * HEAD_K] for j in range(len(lgs))]


def _chunk_step(q, k, b, v, state, mask, ref_row, last_row, want_out):
    b_ref = b[ref_row:ref_row + 1]
    b_last = b[last_row:last_row + 1]
    vb = v.astype(BF16)
    o = None
    if want_out:
        qd = (q * jnp.exp(b - b_ref)).astype(BF16)
        kd = (k * jnp.exp(b_ref - b)).astype(BF16)
        scores = lax.dot_general(qd, kd, (((1,), (1,)), ((), ())), preferred_element_type=F32)
        scores = jnp.where(mask, scores, 0.0).astype(BF16)
        qs = (q * jnp.exp(b)).astype(BF16)
        o = _dot(scores, vb) + lax.dot_general(qs, state.astype(BF16), (((1,), (1,)), ((), ())),
                                               preferred_element_type=F32)
    kl = (k * jnp.exp(b_last - b)).astype(BF16)
    new_state = state * jnp.exp(b_last) + lax.dot_general(vb, kl, (((0,), (0,)), ((), ())),
                                                          preferred_element_type=F32)
    return o, new_state


def _scan_core(load_fwd, load_bwd, load_ctx_fwd, load_ctx_bwd, gate_ref, g_ref, y_ref,
               of_scr, ob_scr, sf_scr, sb_scr, n_chunks, n_ctx_chunks, dv):
    c = GRID_W
    ri = lax.broadcasted_iota(jnp.int32, (c, c), 0)
    ci = lax.broadcasted_iota(jnp.int32, (c, c), 1)
    lower = ci <= ri
    upper = ci >= ri
    tri_f = jnp.concatenate([lower.astype(BF16)] * 3, axis=1)
    tri_b = jnp.concatenate([upper.astype(BF16)] * 3, axis=1)
    fwd = dict(mask=lower, ref_row=c // 2, last_row=c - 1)
    bwd = dict(mask=upper, ref_row=c - 1 - c // 2, last_row=0)

    def run(load, rows, tri3, state, out_scr, kind):
        loaded = [load(r) for r in rows]
        bs = _cumulative_decays([lg for _, _, lg, _ in loaded], tri3)
        for r, (q, k, _, v), b in zip(rows, loaded, bs):
            o, state = _chunk_step(q, k, b, v, state, want_out=out_scr is not None, **kind)
            if out_scr is not None:
                out_scr[pl.ds(r, c), :] = o
        return state

    zero = jnp.zeros((dv, HEAD_K), F32)
    ctx_rows = [i * CTX_CHUNK for i in range(n_ctx_chunks)]
    sf_scr[...] = run(load_ctx_fwd, ctx_rows, tri_f, zero, None, fwd)
    sb_scr[...] = run(load_ctx_bwd, ctx_rows[::-1], tri_b, zero, None, bwd)

    per_trip = math.gcd(n_chunks, SCAN_CHUNKS_PER_TRIP)

    def step(i, carry):
        ns = [i * per_trip + j for j in range(per_trip)]
        rows_f = [pl.multiple_of(n * c, c) for n in ns]
        rows_b = [pl.multiple_of((n_chunks - 1 - n) * c, c) for n in ns]
        sf_scr[...] = run(load_fwd, rows_f, tri_f, sf_scr[...], of_scr, fwd)
        sb_scr[...] = run(load_bwd, rows_b, tri_b, sb_scr[...], ob_scr, bwd)
        return carry

    lax.fori_loop(0, n_chunks // per_trip, step, 0)

    o = of_scr[...] + ob_scr[...]
    gate = gate_ref[0]
    y_ref[0] = _rms(o) * g_ref[...] * (gate * _sigmoid(gate))


def _hgrn_scan_body(q_ref, zf_ref, zb_ref, v_ref, gate_ref, czf_ref, czb_ref, cv_ref, lb_ref, g_ref,
                    y_ref, of_scr, ob_scr, sf_scr, sb_scr, *, n_chunks, n_ctx_chunks):
    def forget(z, lower):
        f = lower + (1.0 - lower) * _sigmoid(z)
        return 1.0 - f, jnp.log(f)

    def loader(q_r, z_r, v_r, row):
        lower = lb_ref[row:row + 1, :]

        def load(rows):
            sl = pl.ds(rows, GRID_W)
            k, lg = forget(z_r[0, sl, :], lower)
            q = q_r[0, sl, :] if q_r is not None else None
            return q, k, lg, v_r[0, sl, :]
        return load

    _scan_core(loader(q_ref, zf_ref, v_ref, 0), loader(q_ref, zb_ref, v_ref, 1),
               loader(None, czf_ref, cv_ref, 0), loader(None, czb_ref, cv_ref, 1),
               gate_ref, g_ref, y_ref, of_scr, ob_scr, sf_scr, sb_scr, n_chunks, n_ctx_chunks, HG_HEAD_V)


def _gla_scan_body(q_ref, k_ref, lf_ref, lbk_ref, v_ref, gate_ref, ck_ref, clf_ref, clb_ref, cv_ref, g_ref,
                   y_ref, of_scr, ob_scr, sf_scr, sb_scr, *, n_chunks, n_ctx_chunks):
    def loader(q_r, k_r, l_r, v_r):
        def load(rows):
            sl = pl.ds(rows, GRID_W)
            q = q_r[0, sl, :] if q_r is not None else None
            return q, k_r[0, sl, :], l_r[0, sl, :], v_r[0, sl, :]
        return load

    _scan_core(loader(q_ref, k_ref, lf_ref, v_ref), loader(q_ref, k_ref, lbk_ref, v_ref),
               loader(None, ck_ref, clf_ref, cv_ref), loader(None, ck_ref, clb_ref, cv_ref),
               gate_ref, g_ref, y_ref, of_scr, ob_scr, sf_scr, sb_scr, n_chunks, n_ctx_chunks, GLA_HEAD_V)


def _scan_call(body, n_heads, dv, lat, ctx, small, bsz, t, t_ctx, name):
    head = lambda rows, w: pl.BlockSpec((1, rows, w), lambda b, h: (b, 0, h))
    in_specs = ([head(t, w) for _, w in lat] + [head(t_ctx, w) for _, w in ctx]
                + [pl.BlockSpec(bs, im) for _, bs, im in small])
    return pl.pallas_call(
        functools.partial(body, n_chunks=t // GRID_W, n_ctx_chunks=t_ctx // CTX_CHUNK),
        grid=(bsz, n_heads),
        in_specs=in_specs,
        out_specs=head(t, dv),
        out_shape=jax.ShapeDtypeStruct((bsz, t, n_heads * dv), F32),
        scratch_shapes=[pltpu.VMEM((t, dv), F32), pltpu.VMEM((t, dv), F32),
                        pltpu.VMEM((dv, HEAD_K), F32), pltpu.VMEM((dv, HEAD_K), F32)],
        compiler_params=_params(2),
        name=name,
    )(*[a for a, _ in lat], *[a for a, _ in ctx], *[a for a, _, _ in small])


def _mixer_out_body(yh_ref, yg_ref, mh_ref, mg_ref, x_ref, gate1_ref, shift2_ref, scale2_ref, g2_ref,
                    wbh_ref, wbg_ref, wo_ref, wq_ref, k1_ref, k2_ref,
                    x1_ref, h2_ref, st_ref):
    yh = _dot(yh_ref[0].astype(BF16), wbh_ref[...])
    yg = _dot(yg_ref[0].astype(BF16), wbg_ref[...])
    y = _sigmoid(mh_ref[0]) * yh + _sigmoid(mg_ref[0]) * yg
    x1 = x_ref[0] + gate1_ref[0] * _dot(y.astype(BF16), wo_ref[...])
    x1_ref[0] = x1
    h2 = _rms(x1) * g2_ref[...] * (1.0 + scale2_ref[0]) + shift2_ref[0]
    h2b = h2.astype(BF16)
    h2_ref[0] = h2b
    q = _dot(h2b, wq_ref[...]).astype(BF16)
    for hh in range(2 * PEER_HEADS):
        keys = k1_ref if hh % 2 == 0 else k2_ref
        st_ref[hh] = lax.dot_general(keys[...], q[:, hh * PEER_SUB_DIM:(hh + 1) * PEER_SUB_DIM],
                                     (((1,), (1,)), ((), ())), preferred_element_type=F32)


def _mixer_out(y_hg, y_gla, m_hg, m_gla, x, gate1, shift2, scale2, g2, wbh, wbg, wo, wq, k1, k2):
    bsz, t, d = x.shape
    tm = ROW_TILE
    row = pl.BlockSpec((1, tm, d), lambda b, i: (b, i, 0))
    per_batch = pl.BlockSpec((1, 1, d), lambda b, i: (b, 0, 0))
    tiles_per_batch = t // tm
    n = bsz * t
    return pl.pallas_call(
        _mixer_out_body,
        grid=(bsz, tiles_per_batch),
        in_specs=[row, row, row, row, row, per_batch, per_batch, per_batch, _resident((1, d)),
                  _resident(wbh.shape), _resident(wbg.shape), _resident(wo.shape), _resident(wq.shape),
                  _resident(k1.shape), _resident(k2.shape)],
        out_specs=[row, row,
                   pl.BlockSpec((2 * PEER_HEADS, PEER_N_KEYS, tm), lambda b, i: (0, 0, b * tiles_per_batch + i))],
        out_shape=[jax.ShapeDtypeStruct((bsz, t, d), F32), jax.ShapeDtypeStruct((bsz, t, d), BF16),
                   jax.ShapeDtypeStruct((2 * PEER_HEADS, PEER_N_KEYS, n), F32)],
        compiler_params=_params(2),
        name="mixer_out",
    )(y_hg, y_gla, m_hg, m_gla, x, gate1, shift2, scale2, g2, wbh, wbg, wo, wq, k1, k2)


_CAND_GROUPS = (("a", 0, 0), ("a", 0, 8), ("a", 1, 0), ("b", 0, 8),
                ("b", 0, 0), ("b", 1, 0), ("b", 2, 0), ("b", 3, 0), ("b", 4, 0))
_CAND_ROWS = len(_CAND_GROUPS) * SUBLANES


def _candidate_tables():
    flat = np.full((_CAND_ROWS, 1), -1, np.int32)
    seen = set()
    for g, (kind, fixed, start) in enumerate(_CAND_GROUPS):
        for r in range(SUBLANES):
            a, b = (fixed, start + r) if kind == "a" else (start + r, fixed)
            if (a + 1) * (b + 1) <= PEER_TOPK and (a, b) not in seen:
                seen.add((a, b))
                flat[g * SUBLANES + r, 0] = a * PEER_TOPK + b
    needed = {(a, b) for a in range(PEER_TOPK) for b in range(PEER_TOPK) if (a + 1) * (b + 1) <= PEER_TOPK}
    assert seen == needed
    return flat


def _extract_top(x, key, rounds):
    big = np.int32(2 ** 30)
    vals, keys = [], []
    for _ in range(rounds):
        m = jnp.max(x, axis=0, keepdims=True)
        sel = jnp.min(jnp.where(x == m, key, big), axis=0, keepdims=True)
        vals.append(m)
        keys.append(sel)
        x = jnp.where(key == sel, -jnp.inf, x)
    return vals, keys


def _pick_row(table, row_ids, sel):
    return jnp.sum(jnp.where(row_ids == sel, table, 0), axis=0, keepdims=True)


def _route_body(st_ref, flat_ref, key1_ref, key2_ref, gate_ref):
    tn = st_ref.shape[-1]
    key_ids = lax.broadcasted_iota(jnp.int32, (PEER_N_KEYS, tn), 0)
    rank_ids = lax.broadcasted_iota(jnp.int32, (PEER_TOPK, tn), 0)
    flat = jnp.broadcast_to(flat_ref[...], (_CAND_ROWS, tn))
    valid = flat >= 0
    keys1, keys2, gates = [], [], []
    for h in range(PEER_HEADS):
        v1, i1 = _extract_top(st_ref[2 * h], key_ids, PEER_TOPK)
        v2, i2 = _extract_top(st_ref[2 * h + 1], key_ids, PEER_TOPK)
        v1c, i1c = jnp.concatenate(v1, axis=0), jnp.concatenate(i1, axis=0)
        v2c, i2c = jnp.concatenate(v2, axis=0), jnp.concatenate(i2, axis=0)
        cand = jnp.concatenate(
            [v1[fixed] + v2c[start:start + SUBLANES] if kind == "a" else v1c[start:start + SUBLANES] + v2[fixed]
             for kind, fixed, start in _CAND_GROUPS], axis=0)
        cand = jnp.where(valid, cand, -jnp.inf)
        top_s, top_c = _extract_top(cand, flat, PEER_TOPK)
        m = top_s[0]
        ex = [jnp.exp(s - m) for s in top_s]
        denom = functools.reduce(lambda a, b: a + b, ex)
        for s_k, c_k in zip(ex, top_c):
            a = c_k >> 4
            b = c_k & (PEER_TOPK - 1)
            keys1.append(_pick_row(i1c, rank_ids, a))
            keys2.append(_pick_row(i2c, rank_ids, b))
            gates.append(s_k / denom)
    key1_ref[...] = jnp.concatenate(keys1, axis=0).T
    key2_ref[...] = jnp.concatenate(keys2, axis=0).T
    gate_ref[...] = jnp.concatenate(gates, axis=0).T


def _route(scores_t):
    n = scores_t.shape[-1]
    tn = ROUTE_TILE
    flat = jnp.asarray(_candidate_tables())
    return pl.pallas_call(
        _route_body,
        grid=(n // tn,),
        in_specs=[pl.BlockSpec((2 * PEER_HEADS, PEER_N_KEYS, tn), lambda i: (0, 0, i)),
                  pl.BlockSpec(flat.shape, lambda i: (0, 0))],
        out_specs=[pl.BlockSpec((tn, N_SEL), lambda i: (i, 0))] * 3,
        out_shape=[jax.ShapeDtypeStruct((n, N_SEL), jnp.int32), jax.ShapeDtypeStruct((n, N_SEL), jnp.int32),
                   jax.ShapeDtypeStruct((n, N_SEL), F32)],
        compiler_params=_params(1),
        name="peer_route",
    )(scores_t, flat)


def _peer_dense_body(h_ref, key1_ref, key2_ref, gate_ref, x1_ref, gate2_ref, fg_ref, ut_ref, v_ref, o_ref, g_scr,
                     *, tm, blocks_per_step):
    k = pl.program_id(1)
    slab = tm + SUBLANES

    @pl.when(k == 0)
    def _():
        o_ref[...] = jnp.zeros_like(o_ref)
        sub = lax.broadcasted_iota(jnp.int32, (PEER_N_KEYS, N_SEL), 0)

        def tokens(i, carry):
            for j in range(PEER_GATE_TOKENS_PER_TRIP):
                t = i * PEER_GATE_TOKENS_PER_TRIP + j
                key1 = key1_ref[pl.ds(t, 1), :]
                gate = gate_ref[pl.ds(t, 1), :]
                hi = gate.astype(BF16).astype(F32)
                on1 = sub == key1
                lhs = jnp.concatenate([jnp.where(on1, hi, 0.0), jnp.where(on1, gate - hi, 0.0)], axis=1).astype(BF16)
                on2 = jnp.where(sub == key2_ref[pl.ds(t, 1), :], 1.0, 0.0).astype(BF16)
                g_t = lax.dot_general(lhs, jnp.concatenate([on2, on2], axis=1), (((1,), (1,)), ((), ())),
                                      preferred_element_type=F32)
                g_scr[pl.ds(t, PEER_N_KEYS, stride=slab), :] = g_t
            return carry

        lax.fori_loop(0, tm // PEER_GATE_TOKENS_PER_TRIP, tokens, 0)

    act = _dot(h_ref[...], ut_ref[...])
    g_blk = jnp.concatenate(
        [g_scr[pl.ds(pl.multiple_of((k * blocks_per_step + j) * slab, SUBLANES), tm), :]
         for j in range(blocks_per_step)], axis=1)
    coef = (g_blk * _gelu(act)).astype(BF16)
    o_ref[...] += _dot(coef, v_ref[...])

    @pl.when(k == pl.num_programs(1) - 1)
    def _():
        x2 = x1_ref[...] + gate2_ref[0] * o_ref[...]
        o_ref[...] = _rms(x2) * fg_ref[...]


def _peer_dense(h2_bf16, key1, key2, gate, x1, gate2, final_g, u_t, v, tokens_per_batch):
    n, d = x1.shape
    n_experts = v.shape[0]
    tm, eb = PEER_ROW_TILE, PEER_EXPERT_TILE
    tiles_per_batch = tokens_per_batch // tm
    row = lambda w: pl.BlockSpec((tm, w), lambda i, k: (i, 0))
    once_per_tile = lambda w: pl.BlockSpec((tm, w), lambda i, k: (i, 0), pipeline_mode=pl.Buffered(1))
    return pl.pallas_call(
        functools.partial(_peer_dense_body, tm=tm, blocks_per_step=eb // PEER_N_KEYS),
        grid=(n // tm, n_experts // eb),
        in_specs=[once_per_tile(d), row(N_SEL), row(N_SEL), row(N_SEL), once_per_tile(d),
                  pl.BlockSpec((1, 1, d), lambda i, k: (i // tiles_per_batch, 0, 0)),
                  pl.BlockSpec((1, d), lambda i, k: (0, 0)),
                  pl.BlockSpec((d, eb), lambda i, k: (0, k)),
                  pl.BlockSpec((eb, d), lambda i, k: (k, 0))],
        out_specs=row(d),
        out_shape=jax.ShapeDtypeStruct((n, d), F32),
        scratch_shapes=[pltpu.VMEM((PEER_N_KEYS * (tm + SUBLANES), PEER_N_KEYS), F32)],
        compiler_params=_params(2),
        name="peer_dense",
    )(h2_bf16, key1, key2, gate, x1, gate2, final_g, u_t, v)


def kernel(x, c, ctx, c_ctx, ada_w, ada_b, norm_mix_g, w_in, hgrn_lb_logits, hgrn_norm_g, gla_gk_w, gla_gk_b, gla_norm_g, w_branch_hgrn, w_branch_gla, w_out, norm_ffn_g, peer_wq, peer_k1, peer_k2, peer_u, peer_v, final_g):
    bsz, seq, d = x.shape
    t_ctx = ctx.shape[1]
    depth = ada_w.shape[0]
    assert depth == 1 and d == D_MODEL and w_in.shape[-1] == D_IN
    l = 0
    row2 = lambda a: a.reshape(1, -1)

    lower = jnp.cumsum(jax.nn.softmax(hgrn_lb_logits, axis=0), axis=0)[l]
    lower = jnp.concatenate([lower, jnp.zeros((SUBLANES - 2, HG_KEY), F32)], axis=0)
    cvec = jnp.concatenate([c, c_ctx[None], jnp.zeros((SUBLANES - 1, d), F32)], axis=0)
    mod_all = _adaln(cvec, ada_w[l], ada_b[l])
    mod = [m.reshape(bsz, 1, d) for m in jnp.split(mod_all[:bsz], N_MOD, axis=-1)]
    mod_c = [jnp.broadcast_to(m.reshape(1, 1, d), (bsz, 1, d)) for m in jnp.split(mod_all[bsz:bsz + 1], N_MOD, axis=-1)]
    w_in_b = w_in[l].astype(BF16)
    zr = jnp.zeros((GLA_GATE_RANK, GLA_KEY), F32)
    wrank = jnp.concatenate([jnp.concatenate([gla_gk_w[l, 0], zr], axis=1),
                             jnp.concatenate([zr, gla_gk_w[l, 1]], axis=1)], axis=0).astype(BF16)
    brank = gla_gk_b[l].reshape(1, 2 * GLA_KEY)

    lat = _inproj(x, mod[0], mod[1], row2(norm_mix_g[l]), w_in_b, wrank, brank)
    cx = _inproj(ctx, mod_c[0], mod_c[1], row2(norm_mix_g[l]), w_in_b, wrank, brank)
    (hq, zf, zb, hi, hgate, gq, gk, gv, gg, glf, glb, m_hg, m_gla) = lat
    (_, czf, czb, chi, _, _, cgk, cgv, _, cglf, cglb, _, _) = cx

    y_hg = _scan_call(
        _hgrn_scan_body, HG_HEADS, HG_HEAD_V,
        [(hq, HEAD_K), (zf, HEAD_K), (zb, HEAD_K), (hi, HG_HEAD_V), (hgate, HG_HEAD_V)],
        [(czf, HEAD_K), (czb, HEAD_K), (chi, HG_HEAD_V)],
        [(lower, (SUBLANES, HEAD_K), lambda b, h: (0, h)), (row2(hgrn_norm_g[l]), (1, HG_HEAD_V), lambda b, h: (0, 0))],
        bsz, seq, t_ctx, "hgrn_scan")
    y_gla = _scan_call(
        _gla_scan_body, GLA_HEADS, GLA_HEAD_V,
        [(gq, HEAD_K), (gk, HEAD_K), (glf, HEAD_K), (glb, HEAD_K), (gv, GLA_HEAD_V), (gg, GLA_HEAD_V)],
        [(cgk, HEAD_K), (cglf, HEAD_K), (cglb, HEAD_K), (cgv, GLA_HEAD_V)],
        [(row2(gla_norm_g[l]), (1, GLA_HEAD_V), lambda b, h: (0, 0))],
        bsz, seq, t_ctx, "gla_scan")

    x1, h2, scores_t = _mixer_out(
        y_hg, y_gla, m_hg, m_gla, x, mod[2], mod[3], mod[4], row2(norm_ffn_g[l]),
        w_branch_hgrn[l].astype(BF16), w_branch_gla[l].astype(BF16), w_out[l].astype(BF16),
        peer_wq[l].astype(BF16), peer_k1[l].astype(BF16), peer_k2[l].astype(BF16))

    key1, key2, gate = _route(scores_t)
    n = bsz * seq
    out = _peer_dense(h2.reshape(n, d), key1, key2, gate, x1.reshape(n, d), mod[5], row2(final_g),
                      peer_u[l].astype(BF16).T, peer_v[l].astype(BF16), seq)
    return out.reshape(bsz, seq, d)
```

```python
import functools
import math

import jax
import jax.numpy as jnp
import numpy as np
from jax import lax
from jax.experimental import pallas as pl
from jax.experimental.pallas import tpu as pltpu

F32 = jnp.float32
BF16 = jnp.bfloat16

D_MODEL = 1024
GRID_W = 64
CTX_CHUNK = 64
N_MOD = 6
EPS = 1e-6

HG_HEADS = 8
HEAD_K = 128
HG_KEY = HG_HEADS * HEAD_K
HG_VAL = D_MODEL
HG_HEAD_V = HG_VAL // HG_HEADS
GLA_HEADS = 4
GLA_KEY = D_MODEL // 2
GLA_VAL = D_MODEL
GLA_HEAD_V = GLA_VAL // GLA_HEADS
GLA_GATE_RANK = 16
GLA_GATE_NORMALIZER = 16.0

C_HQ, C_ZF, C_ZB, C_HI, C_HGATE = 0, 1024, 2048, 3072, 4096
C_GQ, C_GK, C_GV, C_GG = 5120, 5632, 6144, 7168
C_RANK = 8192
C_MHG = C_RANK + 2 * GLA_GATE_RANK
C_MGLA = C_MHG + D_MODEL
D_IN = C_MGLA + D_MODEL

PEER_HEADS = 8
PEER_N_KEYS = 128
PEER_SUB_DIM = 128
PEER_TOPK = 16
N_SEL = PEER_HEADS * PEER_TOPK

SUBLANES = 8
VMEM_LIMIT = 56 * 1024 * 1024

ROW_TILE = 256
INPROJ_ROW_TILE = 256
ROUTE_TILE = 256
PEER_ROW_TILE = 512
PEER_EXPERT_TILE = 512
PEER_GATE_TOKENS_PER_TRIP = 32
SCAN_CHUNKS_PER_TRIP = 16


def _resident(shape):
    return pl.BlockSpec(shape, lambda *_: (0,) * len(shape), pipeline_mode=pl.Buffered(1))


def _params(n_axes):
    return pltpu.CompilerParams(dimension_semantics=("arbitrary",) * n_axes, vmem_limit_bytes=VMEM_LIMIT)


def _split3(a):
    p1 = a.astype(BF16)
    r1 = a - p1.astype(F32)
    p2 = r1.astype(BF16)
    p3 = (r1 - p2.astype(F32)).astype(BF16)
    return p1, p2, p3


def _dot(a, b):
    return jnp.dot(a, b, preferred_element_type=F32)


def _rms(x):
    return x * lax.rsqrt(jnp.mean(x * x, axis=-1, keepdims=True) + EPS)


def _gelu(x):
    return 0.5 * x * (1.0 + lax.erf(x * np.float32(2.0 ** -0.5)))


def _sigmoid(x):
    return 1.0 / (1.0 + jnp.exp(-x))


def _log_sigmoid(x):
    return jnp.minimum(x, 0.0) - jnp.log1p(jnp.exp(-jnp.abs(x)))


def _adaln_body(c_ref, w_ref, b_ref, o_ref):
    c = c_ref[...]
    s = (c * _sigmoid(c)).astype(BF16)
    o_ref[...] = _dot(s, w_ref[...].astype(BF16)) + b_ref[...]


def _adaln(cvec, w, b):
    rows, d = cvec.shape
    n_out = w.shape[1]
    return pl.pallas_call(
        _adaln_body,
        grid=(n_out // d,),
        in_specs=[pl.BlockSpec((rows, d), lambda j: (0, 0)),
                  pl.BlockSpec((d, d), lambda j: (0, j)),
                  pl.BlockSpec((1, d), lambda j: (0, j))],
        out_specs=pl.BlockSpec((rows, d), lambda j: (0, j)),
        out_shape=jax.ShapeDtypeStruct((rows, n_out), F32),
        compiler_params=_params(1),
        name="adaln",
    )(cvec, w, b.reshape(1, n_out))


def _inproj_body(x_ref, shift_ref, scale_ref, g_ref, w_ref, wrank_ref, brank_ref,
                 hq_ref, zf_ref, zb_ref, hi_ref, hgate_ref, gq_ref, gk_ref, gv_ref, gg_ref,
                 glf_ref, glb_ref, mhg_ref, mgla_ref):
    h = _rms(x_ref[0]) * g_ref[...] * (1.0 + scale_ref[0]) + shift_ref[0]
    hb = h.astype(BF16)

    def cols(start, width):
        return _dot(hb, w_ref[:, start:start + width])

    hq_ref[0] = cols(C_HQ, HG_KEY) * np.float32(HEAD_K ** -0.5)
    zf_ref[0] = cols(C_ZF, HG_KEY)
    zb_ref[0] = cols(C_ZB, HG_KEY)
    hi_ref[0] = cols(C_HI, HG_VAL)
    hgate_ref[0] = cols(C_HGATE, HG_VAL)
    gq_ref[0] = cols(C_GQ, GLA_KEY) * np.float32(HEAD_K ** -0.5)
    gk_ref[0] = cols(C_GK, GLA_KEY)
    gv_ref[0] = cols(C_GV, GLA_VAL)
    gg_ref[0] = cols(C_GG, GLA_VAL)
    mhg_ref[0] = cols(C_MHG, D_MODEL)
    mgla_ref[0] = cols(C_MGLA, D_MODEL)
    rank = cols(C_RANK, 2 * GLA_GATE_RANK).astype(BF16)
    pre = _dot(rank, wrank_ref[...]) + brank_ref[...]
    lg = _log_sigmoid(pre) / GLA_GATE_NORMALIZER
    glf_ref[0] = lg[:, :GLA_KEY]
    glb_ref[0] = lg[:, GLA_KEY:]


def _inproj(x, shift, scale, g, w_bf16, wrank, brank):
    bsz, t, d = x.shape
    tm = INPROJ_ROW_TILE
    assert t % tm == 0
    widths = (HG_KEY, HG_KEY, HG_KEY, HG_VAL, HG_VAL, GLA_KEY, GLA_KEY, GLA_VAL, GLA_VAL,
              GLA_KEY, GLA_KEY, D_MODEL, D_MODEL)
    row = lambda w: pl.BlockSpec((1, tm, w), lambda b, i: (b, i, 0))
    per_batch = pl.BlockSpec((1, 1, d), lambda b, i: (b, 0, 0))
    return pl.pallas_call(
        _inproj_body,
        grid=(bsz, t // tm),
        in_specs=[row(d), per_batch, per_batch, _resident((1, d)), _resident(w_bf16.shape),
                  _resident(wrank.shape), _resident(brank.shape)],
        out_specs=[row(w) for w in widths],
        out_shape=[jax.ShapeDtypeStruct((bsz, t, w), F32) for w in widths],
        compiler_params=_params(2),
        name="inproj",
    )(x, shift, scale, g, w_bf16, wrank, brank)


def _cumulative_decays(lgs, tri3):
    wide = jnp.concatenate(lgs, axis=1)
    b = _dot(tri3, jnp.concatenate(_split3(wide), axis=0))
    return [b[:, j * HEAD_K:(j + 1) * HEAD_K] for j in range(len(lgs))]


def _chunk_step(q, k, b, v, state, mask, ref_row, last_row, want_out):
    b_ref = b[ref_row:ref_row + 1]
    b_last = b[last_row:last_row + 1]
    vb = v.astype(BF16)
    o = None
    if want_out:
        qd = (q * jnp.exp(b - b_ref)).astype(BF16)
        kd = (k * jnp.exp(b_ref - b)).astype(BF16)
        scores = lax.dot_general(qd, kd, (((1,), (1,)), ((), ())), preferred_element_type=F32)
        scores = jnp.where(mask, scores, 0.0).astype(BF16)
        qs = (q * jnp.exp(b)).astype(BF16)
        o = _dot(scores, vb) + lax.dot_general(qs, state.astype(BF16), (((1,), (1,)), ((), ())),
                                               preferred_element_type=F32)
    kl = (k * jnp.exp(b_last - b)).astype(BF16)
    new_state = state * jnp.exp(b_last) + lax.dot_general(vb, kl, (((0,), (0,)), ((), ())),
                                                          preferred_element_type=F32)
    return o, new_state


def _scan_core(load_fwd, load_bwd, load_ctx_fwd, load_ctx_bwd, gate_ref, g_ref, y_ref,
               of_scr, ob_scr, sf_scr, sb_scr, n_chunks, n_ctx_chunks, dv):
    c = GRID_W
    ri = lax.broadcasted_iota(jnp.int32, (c, c), 0)
    ci = lax.broadcasted_iota(jnp.int32, (c, c), 1)
    lower = ci <= ri
    upper = ci >= ri
    tri_f = jnp.concatenate([lower.astype(BF16)] * 3, axis=1)
    tri_b = jnp.concatenate([upper.astype(BF16)] * 3, axis=1)
    fwd = dict(mask=lower, ref_row=c // 2, last_row=c - 1)
    bwd = dict(mask=upper, ref_row=c - 1 - c // 2, last_row=0)

    def run(load, rows, tri3, state, out_scr, kind):
        loaded = [load(r) for r in rows]
        bs = _cumulative_decays([lg for _, _, lg, _ in loaded], tri3)
        for r, (q, k, _, v), b in zip(rows, loaded, bs):
            o, state = _chunk_step(q, k, b, v, state, want_out=out_scr is not None, **kind)
            if out_scr is not None:
                out_scr[pl.ds(r, c), :] = o
        return state

    zero = jnp.zeros((dv, HEAD_K), F32)
    ctx_rows = [i * CTX_CHUNK for i in range(n_ctx_chunks)]
    sf_scr[...] = run(load_ctx_fwd, ctx_rows, tri_f, zero, None, fwd)
    sb_scr[...] = run(load_ctx_bwd, ctx_rows[::-1], tri_b, zero, None, bwd)

    per_trip = math.gcd(n_chunks, SCAN_CHUNKS_PER_TRIP)

    def step(i, carry):
        ns = [i * per_trip + j for j in range(per_trip)]
        rows_f = [pl.multiple_of(n * c, c) for n in ns]
        rows_b = [pl.multiple_of((n_chunks - 1 - n) * c, c) for n in ns]
        sf_scr[...] = run(load_fwd, rows_f, tri_f, sf_scr[...], of_scr, fwd)
        sb_scr[...] = run(load_bwd, rows_b, tri_b, sb_scr[...], ob_scr, bwd)
        return carry

    lax.fori_loop(0, n_chunks // per_trip, step, 0)

    o = of_scr[...] + ob_scr[...]
    gate = gate_ref[0]
    y_ref[0] = _rms(o) * g_ref[...] * (gate * _sigmoid(gate))


def _hgrn_scan_body(q_ref, zf_ref, zb_ref, v_ref, gate_ref, czf_ref, czb_ref, cv_ref, lb_ref, g_ref,
                    y_ref, of_scr, ob_scr, sf_scr, sb_scr, *, n_chunks, n_ctx_chunks):
    def forget(z, lower):
        f = lower + (1.0 - lower) * _sigmoid(z)
        return 1.0 - f, jnp.log(f)

    def loader(q_r, z_r, v_r, row):
        lower = lb_ref[row:row + 1, :]

        def load(rows):
            sl = pl.ds(rows, GRID_W)
            k, lg = forget(z_r[0, sl, :], lower)
            q = q_r[0, sl, :] if q_r is not None else None
            return q, k, lg, v_r[0, sl, :]
        return load

    _scan_core(loader(q_ref, zf_ref, v_ref, 0), loader(q_ref, zb_ref, v_ref, 1),
               loader(None, czf_ref, cv_ref, 0), loader(None, czb_ref, cv_ref, 1),
               gate_ref, g_ref, y_ref, of_scr, ob_scr, sf_scr, sb_scr, n_chunks, n_ctx_chunks, HG_HEAD_V)


def _gla_scan_body(q_ref, k_ref, lf_ref, lbk_ref, v_ref, gate_ref, ck_ref, clf_ref, clb_ref, cv_ref, g_ref,
                   y_ref, of_scr, ob_scr, sf_scr, sb_scr, *, n_chunks, n_ctx_chunks):
    def loader(q_r, k_r, l_r, v_r):
        def load(rows):
            sl = pl.ds(rows, GRID_W)
            q = q_r[0, sl, :] if q_r is not None else None
            return q, k_r[0, sl, :], l_r[0, sl, :], v_r[0, sl, :]
        return load

    _scan_core(loader(q_ref, k_ref, lf_ref, v_ref), loader(q_ref, k_ref, lbk_ref, v_ref),
               loader(None, ck_ref, clf_ref, cv_ref), loader(None, ck_ref, clb_ref, cv_ref),
               gate_ref, g_ref, y_ref, of_scr, ob_scr, sf_scr, sb_scr, n_chunks, n_ctx_chunks, GLA_HEAD_V)


def _scan_call(body, n_heads, dv, lat, ctx, small, bsz, t, t_ctx, name):
    head = lambda rows, w: pl.BlockSpec((1, rows, w), lambda b, h: (b, 0, h))
    in_specs = ([head(t, w) for _, w in lat] + [head(t_ctx, w) for _, w in ctx]
                + [pl.BlockSpec(bs, im) for _, bs, im in small])
    return pl.pallas_call(
        functools.partial(body, n_chunks=t // GRID_W, n_ctx_chunks=t_ctx // CTX_CHUNK),
        grid=(bsz, n_heads),
        in_specs=in_specs,
        out_specs=head(t, dv),
        out_shape=jax.ShapeDtypeStruct((bsz, t, n_heads * dv), F32),
        scratch_shapes=[pltpu.VMEM((t, dv), F32), pltpu.VMEM((t, dv), F32),
                        pltpu.VMEM((dv, HEAD_K), F32), pltpu.VMEM((dv, HEAD_K), F32)],
        compiler_params=_params(2),
        name=name,
    )(*[a for a, _ in lat], *[a for a, _ in ctx], *[a for a, _, _ in small])


def _mixer_out_body(yh_ref, yg_ref, mh_ref, mg_ref, x_ref, gate1_ref, shift2_ref, scale2_ref, g2_ref,
                    wbh_ref, wbg_ref, wo_ref, wq_ref, k1_ref, k2_ref,
                    x1_ref, h2_ref, st_ref):
    yh = _dot(yh_ref[0].astype(BF16), wbh_ref[...])
    yg = _dot(yg_ref[0].astype(BF16), wbg_ref[...])
    y = _sigmoid(mh_ref[0]) * yh + _sigmoid(mg_ref[0]) * yg
    x1 = x_ref[0] + gate1_ref[0] * _dot(y.astype(BF16), wo_ref[...])
    x1_ref[0] = x1
    h2 = _rms(x1) * g2_ref[...] * (1.0 + scale2_ref[0]) + shift2_ref[0]
    h2b = h2.astype(BF16)
    h2_ref[0] = h2b
    q = _dot(h2b, wq_ref[...]).astype(BF16)
    for hh in range(2 * PEER_HEADS):
        keys = k1_ref if hh % 2 == 0 else k2_ref
        st_ref[hh] = lax.dot_general(keys[...], q[:, hh * PEER_SUB_DIM:(hh + 1) * PEER_SUB_DIM],
                                     (((1,), (1,)), ((), ())), preferred_element_type=F32)


def _mixer_out(y_hg, y_gla, m_hg, m_gla, x, gate1, shift2, scale2, g2, wbh, wbg, wo, wq, k1, k2):
    bsz, t, d = x.shape
    tm = ROW_TILE
    assert t % tm == 0
    row = pl.BlockSpec((1, tm, d), lambda b, i: (b, i, 0))
    per_batch = pl.BlockSpec((1, 1, d), lambda b, i: (b, 0, 0))
    tiles_per_batch = t // tm
    n = bsz * t
    return pl.pallas_call(
        _mixer_out_body,
        grid=(bsz, tiles_per_batch),
        in_specs=[row, row, row, row, row, per_batch, per_batch, per_batch, _resident((1, d)),
                  _resident(wbh.shape), _resident(wbg.shape), _resident(wo.shape), _resident(wq.shape),
                  _resident(k1.shape), _resident(k2.shape)],
        out_specs=[row, row,
                   pl.BlockSpec((2 * PEER_HEADS, PEER_N_KEYS, tm), lambda b, i: (0, 0, b * tiles_per_batch + i))],
        out_shape=[jax.ShapeDtypeStruct((bsz, t, d), F32), jax.ShapeDtypeStruct((bsz, t, d), BF16),
                   jax.ShapeDtypeStruct((2 * PEER_HEADS, PEER_N_KEYS, n), F32)],
        compiler_params=_params(2),
        name="mixer_out",
    )(y_hg, y_gla, m_hg, m_gla, x, gate1, shift2, scale2, g2, wbh, wbg, wo, wq, k1, k2)


_CAND_GROUPS = (("a", 0, 0), ("a", 0, 8), ("a", 1, 0), ("b", 0, 8),
                ("b", 0, 0), ("b", 1, 0), ("b", 2, 0), ("b", 3, 0), ("b", 4, 0))
_CAND_ROWS = len(_CAND_GROUPS) * SUBLANES


def _candidate_tables():
    flat = np.full((_CAND_ROWS, 1), -1, np.float32)
    seen = set()
    for g, (kind, fixed, start) in enumerate(_CAND_GROUPS):
        for r in range(SUBLANES):
            a, b = (fixed, start + r) if kind == "a" else (start + r, fixed)
            if (a + 1) * (b + 1) <= PEER_TOPK and (a, b) not in seen:
                seen.add((a, b))
                flat[g * SUBLANES + r, 0] = a * PEER_TOPK + b
    needed = {(a, b) for a in range(PEER_TOPK) for b in range(PEER_TOPK) if (a + 1) * (b + 1) <= PEER_TOPK}
    assert seen == needed
    return flat


def _extract_top(x, key, rounds):
    big = np.float32(2 ** 30)
    vals, keys = [], []
    for _ in range(rounds):
        m = jnp.max(x, axis=0, keepdims=True)
        sel = jnp.min(jnp.where(x == m, key, big), axis=0, keepdims=True)
        vals.append(m)
        keys.append(sel)
        x = jnp.where(key == sel, -jnp.inf, x)
    return vals, keys


def _pick_row(table, row_ids, sel):
    return jnp.sum(jnp.where(row_ids == sel, table, 0), axis=0, keepdims=True)


def _route_body(st_ref, flat_ref, key1_ref, key2_ref, gate_ref):
    tn = st_ref.shape[-1]
    key_ids = lax.broadcasted_iota(jnp.int32, (PEER_N_KEYS, tn), 0).astype(F32)
    rank_ids = lax.broadcasted_iota(jnp.int32, (PEER_TOPK, tn), 0).astype(F32)
    flat = jnp.broadcast_to(flat_ref[...], (_CAND_ROWS, tn))
    valid = flat >= 0.0
    keys1, keys2, gates = [], [], []
    for h in range(PEER_HEADS):
        v1, i1 = _extract_top(st_ref[2 * h], key_ids, PEER_TOPK)
        v2, i2 = _extract_top(st_ref[2 * h + 1], key_ids, PEER_TOPK)
        v1c, i1c = jnp.concatenate(v1, axis=0), jnp.concatenate(i1, axis=0)
        v2c, i2c = jnp.concatenate(v2, axis=0), jnp.concatenate(i2, axis=0)
        cand = jnp.concatenate(
            [v1[fixed] + v2c[start:start + SUBLANES] if kind == "a" else v1c[start:start + SUBLANES] + v2[fixed]
             for kind, fixed, start in _CAND_GROUPS], axis=0)
        cand = jnp.where(valid, cand, -jnp.inf)
        top_s, top_c = _extract_top(cand, flat, PEER_TOPK)
        m = top_s[0]
        ex = [jnp.exp(s - m) for s in top_s]
        denom = functools.reduce(lambda a, b: a + b, ex)
        for s_k, c_k in zip(ex, top_c):
            c_int = c_k.astype(jnp.int32)
            a = (c_int >> 4).astype(F32)
            b = (c_int & (PEER_TOPK - 1)).astype(F32)
            keys1.append(_pick_row(i1c, rank_ids, a))
            keys2.append(_pick_row(i2c, rank_ids, b))
            gates.append(s_k / denom)
    key1_ref[...] = jnp.concatenate(keys1, axis=0).T
    key2_ref[...] = jnp.concatenate(keys2, axis=0).T
    gate_ref[...] = jnp.concatenate(gates, axis=0).T


def _route(scores_t):
    n = scores_t.shape[-1]
    tn = ROUTE_TILE
    assert n % tn == 0
    flat = jnp.asarray(_candidate_tables())
    return pl.pallas_call(
        _route_body,
        grid=(n // tn,),
        in_specs=[pl.BlockSpec((2 * PEER_HEADS, PEER_N_KEYS, tn), lambda i: (0, 0, i)),
                  pl.BlockSpec(flat.shape, lambda i: (0, 0))],
        out_specs=[pl.BlockSpec((tn, N_SEL), lambda i: (i, 0))] * 3,
        out_shape=[jax.ShapeDtypeStruct((n, N_SEL), F32)] * 3,
        compiler_params=_params(1),
        name="peer_route",
    )(scores_t, flat)


def _peer_dense_body(h_ref, key1_ref, key2_ref, gate_ref, x1_ref, gate2_ref, fg_ref, ut_ref, v_ref, o_ref, g_scr,
                     *, tm, blocks_per_step):
    k = pl.program_id(1)
    slab = tm + SUBLANES

    @pl.when(k == 0)
    def _():
        o_ref[...] = jnp.zeros_like(o_ref)
        sub = lax.broadcasted_iota(jnp.int32, (PEER_N_KEYS, N_SEL), 0).astype(F32).astype(BF16)
        zero = jnp.zeros((PEER_N_KEYS, N_SEL), BF16)
        one = jnp.ones((PEER_N_KEYS, N_SEL), BF16)

        def tokens(i, carry):
            for j in range(PEER_GATE_TOKENS_PER_TRIP):
                t = i * PEER_GATE_TOKENS_PER_TRIP + j
                key1 = key1_ref[pl.ds(t, 1), :].astype(BF16)
                key2 = key2_ref[pl.ds(t, 1), :].astype(BF16)
                gate = gate_ref[pl.ds(t, 1), :]
                hi = gate.astype(BF16)
                lo = (gate - hi.astype(F32)).astype(BF16)
                on1 = sub == key1
                lhs = jnp.concatenate([jnp.where(on1, hi, zero), jnp.where(on1, lo, zero)], axis=1)
                on2 = jnp.where(sub == key2, one, zero)
                g_t = lax.dot_general(lhs, jnp.concatenate([on2, on2], axis=1), (((1,), (1,)), ((), ())),
                                      preferred_element_type=F32)
                g_scr[pl.ds(t, PEER_N_KEYS, stride=slab), :] = g_t
            return carry

        lax.fori_loop(0, tm // PEER_GATE_TOKENS_PER_TRIP, tokens, 0)

    act = _dot(h_ref[...], ut_ref[...])
    g_blk = jnp.concatenate(
        [g_scr[pl.ds(pl.multiple_of((k * blocks_per_step + j) * slab, SUBLANES), tm), :]
         for j in range(blocks_per_step)], axis=1)
    coef = (g_blk * _gelu(act)).astype(BF16)
    o_ref[...] += _dot(coef, v_ref[...])

    @pl.when(k == pl.num_programs(1) - 1)
    def _():
        x2 = x1_ref[...] + gate2_ref[0] * o_ref[...]
        o_ref[...] = _rms(x2) * fg_ref[...]


def _peer_dense(h2_bf16, key1, key2, gate, x1, gate2, final_g, u_t, v, tokens_per_batch):
    n, d = x1.shape
    n_experts = v.shape[0]
    tm, eb = PEER_ROW_TILE, PEER_EXPERT_TILE
    assert tokens_per_batch % tm == 0 and n_experts % eb == 0 and eb % PEER_N_KEYS == 0
    tiles_per_batch = tokens_per_batch // tm
    n_blocks = n_experts // eb
    row = lambda w: pl.BlockSpec((tm, w), lambda i, k: (i, 0))
    once_per_tile = lambda w: pl.BlockSpec((tm, w), lambda i, k: (i, 0), pipeline_mode=pl.Buffered(1))
    return pl.pallas_call(
        functools.partial(_peer_dense_body, tm=tm, blocks_per_step=eb // PEER_N_KEYS),
        grid=(n // tm, n_blocks),
        in_specs=[once_per_tile(d), row(N_SEL), row(N_SEL), row(N_SEL), once_per_tile(d),
                  pl.BlockSpec((1, 1, d), lambda i, k: (i // tiles_per_batch, 0, 0)),
                  pl.BlockSpec((1, d), lambda i, k: (0, 0)),
                  pl.BlockSpec((d, eb), lambda i, k: (0, k)),
                  pl.BlockSpec((eb, d), lambda i, k: (k, 0))],
        out_specs=row(d),
        out_shape=jax.ShapeDtypeStruct((n, d), F32),
        scratch_shapes=[pltpu.VMEM((PEER_N_KEYS * (tm + SUBLANES), PEER_N_KEYS), F32)],
        compiler_params=_params(2),
        name="peer_dense",
    )(h2_bf16, key1, key2, gate, x1, gate2, final_g, u_t, v)


def kernel(x, c, ctx, c_ctx, ada_w, ada_b, norm_mix_g, w_in, hgrn_lb_logits, hgrn_norm_g, gla_gk_w, gla_gk_b, gla_norm_g, w_branch_hgrn, w_branch_gla, w_out, norm_ffn_g, peer_wq, peer_k1, peer_k2, peer_u, peer_v, final_g):
    bsz, seq, d = x.shape
    t_ctx = ctx.shape[1]
    depth = ada_w.shape[0]
    assert depth == 1 and d == D_MODEL and w_in.shape[-1] == D_IN
    l = 0
    row2 = lambda a: a.reshape(1, -1)

    lower = jnp.cumsum(jax.nn.softmax(hgrn_lb_logits, axis=0), axis=0)[l]
    lower = jnp.concatenate([lower, jnp.zeros((SUBLANES - 2, HG_KEY), F32)], axis=0)
    cvec = jnp.concatenate([c, c_ctx[None], jnp.zeros((SUBLANES - 1, d), F32)], axis=0)
    mod_all = _adaln(cvec, ada_w[l], ada_b[l])
    mod = [m.reshape(bsz, 1, d) for m in jnp.split(mod_all[:bsz], N_MOD, axis=-1)]
    mod_c = [jnp.broadcast_to(m.reshape(1, 1, d), (bsz, 1, d)) for m in jnp.split(mod_all[bsz:bsz + 1], N_MOD, axis=-1)]
    w_in_b = w_in[l].astype(BF16)
    zr = jnp.zeros((GLA_GATE_RANK, GLA_KEY), F32)
    wrank = jnp.concatenate([jnp.concatenate([gla_gk_w[l, 0], zr], axis=1),
                             jnp.concatenate([zr, gla_gk_w[l, 1]], axis=1)], axis=0).astype(BF16)
    brank = gla_gk_b[l].reshape(1, 2 * GLA_KEY)

    lat = _inproj(x, mod[0], mod[1], row2(norm_mix_g[l]), w_in_b, wrank, brank)
    cx = _inproj(ctx, mod_c[0], mod_c[1], row2(norm_mix_g[l]), w_in_b, wrank, brank)
    (hq, zf, zb, hi, hgate, gq, gk, gv, gg, glf, glb, m_hg, m_gla) = lat
    (_, czf, czb, chi, _, _, cgk, cgv, _, cglf, cglb, _, _) = cx

    y_hg = _scan_call(
        _hgrn_scan_body, HG_HEADS, HG_HEAD_V,
        [(hq, HEAD_K), (zf, HEAD_K), (zb, HEAD_K), (hi, HG_HEAD_V), (hgate, HG_HEAD_V)],
        [(czf, HEAD_K), (czb, HEAD_K), (chi, HG_HEAD_V)],
        [(lower, (SUBLANES, HEAD_K), lambda b, h: (0, h)), (row2(hgrn_norm_g[l]), (1, HG_HEAD_V), lambda b, h: (0, 0))],
        bsz, seq, t_ctx, "hgrn_scan")
    y_gla = _scan_call(
        _gla_scan_body, GLA_HEADS, GLA_HEAD_V,
        [(gq, HEAD_K), (gk, HEAD_K), (glf, HEAD_K), (glb, HEAD_K), (gv, GLA_HEAD_V), (gg, GLA_HEAD_V)],
        [(cgk, HEAD_K), (cglf, HEAD_K), (cglb, HEAD_K), (cgv, GLA_HEAD_V)],
        [(row2(gla_norm_g[l]), (1, GLA_HEAD_V), lambda b, h: (0, 0))],
        bsz, seq, t_ctx, "gla_scan")

    x1, h2, scores_t = _mixer_out(
        y_hg, y_gla, m_hg, m_gla, x, mod[2], mod[3], mod[4], row2(norm_ffn_g[l]),
        w_branch_hgrn[l].astype(BF16), w_branch_gla[l].astype(BF16), w_out[l].astype(BF16),
        peer_wq[l].astype(BF16), peer_k1[l].astype(BF16), peer_k2[l].astype(BF16))

    key1, key2, gate = _route(scores_t)
    n = bsz * seq
    out = _peer_dense(h2.reshape(n, d), key1, key2, gate, x1.reshape(n, d), mod[5], row2(final_g),
                      peer_u[l].astype(BF16).T, peer_v[l].astype(BF16), seq)
    return out.reshape(bsz, seq, d)
```

```python
import functools
import math

import jax
import jax.numpy as jnp
import numpy as np
from jax import lax
from jax.experimental import pallas as pl
from jax.experimental.pallas import tpu as pltpu

F32 = jnp.float32
BF16 = jnp.bfloat16

D_MODEL = 1024
GRID_W = 64
CTX_CHUNK = 64
N_MOD = 6
EPS = 1e-6

HG_HEADS = 8
HEAD_K = 128
HG_KEY = HG_HEADS * HEAD_K
HG_VAL = D_MODEL
HG_HEAD_V = HG_VAL // HG_HEADS
GLA_HEADS = 4
GLA_KEY = D_MODEL // 2
GLA_VAL = D_MODEL
GLA_HEAD_V = GLA_VAL // GLA_HEADS
GLA_GATE_RANK = 16
GLA_GATE_NORMALIZER = 16.0

C_HQ, C_ZF, C_ZB, C_HI, C_HGATE = 0, 1024, 2048, 3072, 4096
C_GQ, C_GK, C_GV, C_GG = 5120, 5632, 6144, 7168
C_RANK = 8192
C_MHG = C_RANK + 2 * GLA_GATE_RANK
C_MGLA = C_MHG + D_MODEL
D_IN = C_MGLA + D_MODEL

PEER_HEADS = 8
PEER_N_KEYS = 128
PEER_SUB_DIM = 128
PEER_TOPK = 16
N_SEL = PEER_HEADS * PEER_TOPK

SUBLANES = 8
VMEM_LIMIT = 56 * 1024 * 1024

ROW_TILE = 256
INPROJ_ROW_TILE = 256
ROUTE_TILE = 256
PEER_ROW_TILE = 512
PEER_EXPERT_TILE = 512
PEER_GATE_TOKENS_PER_TRIP = 32
SCAN_CHUNKS_PER_TRIP = 32


def _resident(shape):
    return pl.BlockSpec(shape, lambda *_: (0,) * len(shape), pipeline_mode=pl.Buffered(1))


def _params(n_axes):
    return pltpu.CompilerParams(dimension_semantics=("arbitrary",) * n_axes, vmem_limit_bytes=VMEM_LIMIT)


def _split3(a):
    p1 = a.astype(BF16)
    r1 = a - p1.astype(F32)
    p2 = r1.astype(BF16)
    p3 = (r1 - p2.astype(F32)).astype(BF16)
    return p1, p2, p3


def _dot(a, b):
    return jnp.dot(a, b, preferred_element_type=F32)


def _rms(x):
    return x * lax.rsqrt(jnp.mean(x * x, axis=-1, keepdims=True) + EPS)


def _gelu(x):
    return 0.5 * x * (1.0 + lax.erf(x * np.float32(2.0 ** -0.5)))


def _sigmoid(x):
    return 1.0 / (1.0 + jnp.exp(-x))


def _log_sigmoid(x):
    return jnp.minimum(x, 0.0) - jnp.log1p(jnp.exp(-jnp.abs(x)))


def _adaln_body(c_ref, w_ref, b_ref, o_ref):
    c = c_ref[...]
    s = (c * _sigmoid(c)).astype(BF16)
    o_ref[...] = _dot(s, w_ref[...].astype(BF16)) + b_ref[...]


def _adaln(cvec, w, b):
    rows, d = cvec.shape
    n_out = w.shape[1]
    return pl.pallas_call(
        _adaln_body,
        grid=(n_out // d,),
        in_specs=[pl.BlockSpec((rows, d), lambda j: (0, 0)),
                  pl.BlockSpec((d, d), lambda j: (0, j)),
                  pl.BlockSpec((1, d), lambda j: (0, j))],
        out_specs=pl.BlockSpec((rows, d), lambda j: (0, j)),
        out_shape=jax.ShapeDtypeStruct((rows, n_out), F32),
        compiler_params=_params(1),
        name="adaln",
    )(cvec, w, b.reshape(1, n_out))


def _inproj_body(x_ref, shift_ref, scale_ref, g_ref, w_ref, wrank_ref, brank_ref,
                 hq_ref, zf_ref, zb_ref, hi_ref, hgate_ref, gq_ref, gk_ref, gv_ref, gg_ref,
                 glf_ref, glb_ref, mhg_ref, mgla_ref):
    h = _rms(x_ref[0]) * g_ref[...] * (1.0 + scale_ref[0]) + shift_ref[0]
    hb = h.astype(BF16)

    def cols(start, width):
        return _dot(hb, w_ref[:, start:start + width])

    hq_ref[0] = cols(C_HQ, HG_KEY) * np.float32(HEAD_K ** -0.5)
    zf_ref[0] = cols(C_ZF, HG_KEY)
    zb_ref[0] = cols(C_ZB, HG_KEY)
    hi_ref[0] = cols(C_HI, HG_VAL)
    hgate_ref[0] = cols(C_HGATE, HG_VAL)
    gq_ref[0] = cols(C_GQ, GLA_KEY) * np.float32(HEAD_K ** -0.5)
    gk_ref[0] = cols(C_GK, GLA_KEY)
    gv_ref[0] = cols(C_GV, GLA_VAL)
    gg_ref[0] = cols(C_GG, GLA_VAL)
    mhg_ref[0] = cols(C_MHG, D_MODEL)
    mgla_ref[0] = cols(C_MGLA, D_MODEL)
    rank = cols(C_RANK, 2 * GLA_GATE_RANK).astype(BF16)
    pre = _dot(rank, wrank_ref[...]) + brank_ref[...]
    lg = _log_sigmoid(pre) / GLA_GATE_NORMALIZER
    glf_ref[0] = lg[:, :GLA_KEY]
    glb_ref[0] = lg[:, GLA_KEY:]


def _inproj(x, shift, scale, g, w_bf16, wrank, brank):
    bsz, t, d = x.shape
    tm = INPROJ_ROW_TILE
    assert t % tm == 0
    widths = (HG_KEY, HG_KEY, HG_KEY, HG_VAL, HG_VAL, GLA_KEY, GLA_KEY, GLA_VAL, GLA_VAL,
              GLA_KEY, GLA_KEY, D_MODEL, D_MODEL)
    row = lambda w: pl.BlockSpec((1, tm, w), lambda b, i: (b, i, 0))
    per_batch = pl.BlockSpec((1, 1, d), lambda b, i: (b, 0, 0))
    return pl.pallas_call(
        _inproj_body,
        grid=(bsz, t // tm),
        in_specs=[row(d), per_batch, per_batch, _resident((1, d)), _resident(w_bf16.shape),
                  _resident(wrank.shape), _resident(brank.shape)],
        out_specs=[row(w) for w in widths],
        out_shape=[jax.ShapeDtypeStruct((bsz, t, w), F32) for w in widths],
        compiler_params=_params(2),
        name="inproj",
    )(x, shift, scale, g, w_bf16, wrank, brank)


def _cumulative_decays(lgs, tri3):
    wide = jnp.concatenate(lgs, axis=1)
    b = _dot(tri3, jnp.concatenate(_split3(wide), axis=0))
    return [b[:, j * HEAD_K:(j + 1) * HEAD_K] for j in range(len(lgs))]


def _chunk_step(q, k, b, v, state, mask, ref_row, last_row, want_out):
    b_ref = b[ref_row:ref_row + 1]
    b_last = b[last_row:last_row + 1]
    vb = v.astype(BF16)
    o = None
    if want_out:
        qd = (q * jnp.exp(b - b_ref)).astype(BF16)
        kd = (k * jnp.exp(b_ref - b)).astype(BF16)
        scores = lax.dot_general(qd, kd, (((1,), (1,)), ((), ())), preferred_element_type=F32)
        scores = jnp.where(mask, scores, 0.0).astype(BF16)
        qs = (q * jnp.exp(b)).astype(BF16)
        o = _dot(scores, vb) + lax.dot_general(qs, state.astype(BF16), (((1,), (1,)), ((), ())),
                                               preferred_element_type=F32)
    kl = (k * jnp.exp(b_last - b)).astype(BF16)
    new_state = state * jnp.exp(b_last) + lax.dot_general(vb, kl, (((0,), (0,)), ((), ())),
                                                          preferred_element_type=F32)
    return o, new_state


def _scan_core(load_fwd, load_bwd, load_ctx_fwd, load_ctx_bwd, gate_ref, g_ref, y_ref,
               of_scr, ob_scr, sf_scr, sb_scr, n_chunks, n_ctx_chunks, dv):
    c = GRID_W
    ri = lax.broadcasted_iota(jnp.int32, (c, c), 0)
    ci = lax.broadcasted_iota(jnp.int32, (c, c), 1)
    lower = ci <= ri
    upper = ci >= ri
    tri_f = jnp.concatenate([lower.astype(BF16)] * 3, axis=1)
    tri_b = jnp.concatenate([upper.astype(BF16)] * 3, axis=1)
    fwd = dict(mask=lower, ref_row=c // 2, last_row=c - 1)
    bwd = dict(mask=upper, ref_row=c - 1 - c // 2, last_row=0)

    def run(load, rows, tri3, state, out_scr, kind):
        loaded = [load(r) for r in rows]
        bs = _cumulative_decays([lg for _, _, lg, _ in loaded], tri3)
        for r, (q, k, _, v), b in zip(rows, loaded, bs):
            o, state = _chunk_step(q, k, b, v, state, want_out=out_scr is not None, **kind)
            if out_scr is not None:
                out_scr[pl.ds(r, c), :] = o
        return state

    zero = jnp.zeros((dv, HEAD_K), F32)
    ctx_rows = [i * CTX_CHUNK for i in range(n_ctx_chunks)]
    sf_scr[...] = run(load_ctx_fwd, ctx_rows, tri_f, zero, None, fwd)
    sb_scr[...] = run(load_ctx_bwd, ctx_rows[::-1], tri_b, zero, None, bwd)

    per_trip = math.gcd(n_chunks, SCAN_CHUNKS_PER_TRIP)

    def step(i, carry):
        ns = [i * per_trip + j for j in range(per_trip)]
        rows_f = [pl.multiple_of(n * c, c) for n in ns]
        rows_b = [pl.multiple_of((n_chunks - 1 - n) * c, c) for n in ns]
        sf_scr[...] = run(load_fwd, rows_f, tri_f, sf_scr[...], of_scr, fwd)
        sb_scr[...] = run(load_bwd, rows_b, tri_b, sb_scr[...], ob_scr, bwd)
        return carry

    lax.fori_loop(0, n_chunks // per_trip, step, 0)

    o = of_scr[...] + ob_scr[...]
    gate = gate_ref[0]
    y_ref[0] = _rms(o) * g_ref[...] * (gate * _sigmoid(gate))


def _hgrn_scan_body(q_ref, zf_ref, zb_ref, v_ref, gate_ref, czf_ref, czb_ref, cv_ref, lb_ref, g_ref,
                    y_ref, of_scr, ob_scr, sf_scr, sb_scr, *, n_chunks, n_ctx_chunks):
    def forget(z, lower):
        f = lower + (1.0 - lower) * _sigmoid(z)
        return 1.0 - f, jnp.log(f)

    def loader(q_r, z_r, v_r, row):
        lower = lb_ref[row:row + 1, :]

        def load(rows):
            sl = pl.ds(rows, GRID_W)
            k, lg = forget(z_r[0, sl, :], lower)
            q = q_r[0, sl, :] if q_r is not None else None
            return q, k, lg, v_r[0, sl, :]
        return load

    _scan_core(loader(q_ref, zf_ref, v_ref, 0), loader(q_ref, zb_ref, v_ref, 1),
               loader(None, czf_ref, cv_ref, 0), loader(None, czb_ref, cv_ref, 1),
               gate_ref, g_ref, y_ref, of_scr, ob_scr, sf_scr, sb_scr, n_chunks, n_ctx_chunks, HG_HEAD_V)


def _gla_scan_body(q_ref, k_ref, lf_ref, lbk_ref, v_ref, gate_ref, ck_ref, clf_ref, clb_ref, cv_ref, g_ref,
                   y_ref, of_scr, ob_scr, sf_scr, sb_scr, *, n_chunks, n_ctx_chunks):
    def loader(q_r, k_r, l_r, v_r):
        def load(rows):
            sl = pl.ds(rows, GRID_W)
            q = q_r[0, sl, :] if q_r is not None else None
            return q, k_r[0, sl, :], l_r[0, sl, :], v_r[0, sl, :]
        return load

    _scan_core(loader(q_ref, k_ref, lf_ref, v_ref), loader(q_ref, k_ref, lbk_ref, v_ref),
               loader(None, ck_ref, clf_ref, cv_ref), loader(None, ck_ref, clb_ref, cv_ref),
               gate_ref, g_ref, y_ref, of_scr, ob_scr, sf_scr, sb_scr, n_chunks, n_ctx_chunks, GLA_HEAD_V)


def _scan_call(body, n_heads, dv, lat, ctx, small, bsz, t, t_ctx, name):
    head = lambda rows, w: pl.BlockSpec((1, rows, w), lambda b, h: (b, 0, h))
    in_specs = ([head(t, w) for _, w in lat] + [head(t_ctx, w) for _, w in ctx]
                + [pl.BlockSpec(bs, im) for _, bs, im in small])
    return pl.pallas_call(
        functools.partial(body, n_chunks=t // GRID_W, n_ctx_chunks=t_ctx // CTX_CHUNK),
        grid=(bsz, n_heads),
        in_specs=in_specs,
        out_specs=head(t, dv),
        out_shape=jax.ShapeDtypeStruct((bsz, t, n_heads * dv), F32),
        scratch_shapes=[pltpu.VMEM((t, dv), F32), pltpu.VMEM((t, dv), F32),
                        pltpu.VMEM((dv, HEAD_K), F32), pltpu.VMEM((dv, HEAD_K), F32)],
        compiler_params=_params(2),
        name=name,
    )(*[a for a, _ in lat], *[a for a, _ in ctx], *[a for a, _, _ in small])


def _mixer_out_body(yh_ref, yg_ref, mh_ref, mg_ref, x_ref, gate1_ref, shift2_ref, scale2_ref, g2_ref,
                    wbh_ref, wbg_ref, wo_ref, wq_ref, k1_ref, k2_ref,
                    x1_ref, h2_ref, st_ref):
    yh = _dot(yh_ref[0].astype(BF16), wbh_ref[...])
    yg = _dot(yg_ref[0].astype(BF16), wbg_ref[...])
    y = _sigmoid(mh_ref[0]) * yh + _sigmoid(mg_ref[0]) * yg
    x1 = x_ref[0] + gate1_ref[0] * _dot(y.astype(BF16), wo_ref[...])
    x1_ref[0] = x1
    h2 = _rms(x1) * g2_ref[...] * (1.0 + scale2_ref[0]) + shift2_ref[0]
    h2b = h2.astype(BF16)
    h2_ref[0] = h2b
    q = _dot(h2b, wq_ref[...]).astype(BF16)
    for hh in range(2 * PEER_HEADS):
        keys = k1_ref if hh % 2 == 0 else k2_ref
        st_ref[hh] = lax.dot_general(keys[...], q[:, hh * PEER_SUB_DIM:(hh + 1) * PEER_SUB_DIM],
                                     (((1,), (1,)), ((), ())), preferred_element_type=F32)


def _mixer_out(y_hg, y_gla, m_hg, m_gla, x, gate1, shift2, scale2, g2, wbh, wbg, wo, wq, k1, k2):
    bsz, t, d = x.shape
    tm = ROW_TILE
    assert t % tm == 0
    row = pl.BlockSpec((1, tm, d), lambda b, i: (b, i, 0))
    per_batch = pl.BlockSpec((1, 1, d), lambda b, i: (b, 0, 0))
    tiles_per_batch = t // tm
    n = bsz * t
    return pl.pallas_call(
        _mixer_out_body,
        grid=(bsz, tiles_per_batch),
        in_specs=[row, row, row, row, row, per_batch, per_batch, per_batch, _resident((1, d)),
                  _resident(wbh.shape), _resident(wbg.shape), _resident(wo.shape), _resident(wq.shape),
                  _resident(k1.shape), _resident(k2.shape)],
        out_specs=[row, row,
                   pl.BlockSpec((2 * PEER_HEADS, PEER_N_KEYS, tm), lambda b, i: (0, 0, b * tiles_per_batch + i))],
        out_shape=[jax.ShapeDtypeStruct((bsz, t, d), F32), jax.ShapeDtypeStruct((bsz, t, d), BF16),
                   jax.ShapeDtypeStruct((2 * PEER_HEADS, PEER_N_KEYS, n), F32)],
        compiler_params=_params(2),
        name="mixer_out",
    )(y_hg, y_gla, m_hg, m_gla, x, gate1, shift2, scale2, g2, wbh, wbg, wo, wq, k1, k2)


_CAND_GROUPS = (("a", 0, 0), ("a", 0, 8), ("a", 1, 0), ("b", 0, 8),
                ("b", 0, 0), ("b", 1, 0), ("b", 2, 0), ("b", 3, 0), ("b", 4, 0))
_CAND_ROWS = len(_CAND_GROUPS) * SUBLANES


def _candidate_tables():
    flat = np.full((_CAND_ROWS, 1), -1, np.float32)
    seen = set()
    for g, (kind, fixed, start) in enumerate(_CAND_GROUPS):
        for r in range(SUBLANES):
            a, b = (fixed, start + r) if kind == "a" else (start + r, fixed)
            if (a + 1) * (b + 1) <= PEER_TOPK and (a, b) not in seen:
                seen.add((a, b))
                flat[g * SUBLANES + r, 0] = a * PEER_TOPK + b
    needed = {(a, b) for a in range(PEER_TOPK) for b in range(PEER_TOPK) if (a + 1) * (b + 1) <= PEER_TOPK}
    assert seen == needed
    return flat


def _extract_top(x, key, rounds):
    big = np.float32(2 ** 30)
    vals, keys = [], []
    for _ in range(rounds):
        m = jnp.max(x, axis=0, keepdims=True)
        sel = jnp.min(jnp.where(x == m, key, big), axis=0, keepdims=True)
        vals.append(m)
        keys.append(sel)
        x = jnp.where(key == sel, -jnp.inf, x)
    return vals, keys


def _pick_row(table, row_ids, sel):
    return jnp.sum(jnp.where(row_ids == sel, table, 0), axis=0, keepdims=True)


def _route_body(st_ref, flat_ref, key1_ref, key2_ref, gate_ref):
    tn = st_ref.shape[-1]
    key_ids = lax.broadcasted_iota(jnp.int32, (PEER_N_KEYS, tn), 0).astype(F32)
    rank_ids = lax.broadcasted_iota(jnp.int32, (PEER_TOPK, tn), 0).astype(F32)
    flat = jnp.broadcast_to(flat_ref[...], (_CAND_ROWS, tn))
    valid = flat >= 0.0
    keys1, keys2, gates = [], [], []
    for h in range(PEER_HEADS):
        v1, i1 = _extract_top(st_ref[2 * h], key_ids, PEER_TOPK)
        v2, i2 = _extract_top(st_ref[2 * h + 1], key_ids, PEER_TOPK)
        v1c, i1c = jnp.concatenate(v1, axis=0), jnp.concatenate(i1, axis=0)
        v2c, i2c = jnp.concatenate(v2, axis=0), jnp.concatenate(i2, axis=0)
        cand = jnp.concatenate(
            [v1[fixed] + v2c[start:start + SUBLANES] if kind == "a" else v1c[start:start + SUBLANES] + v2[fixed]
             for kind, fixed, start in _CAND_GROUPS], axis=0)
        cand = jnp.where(valid, cand, -jnp.inf)
        top_s, top_c = _extract_top(cand, flat, PEER_TOPK)
        m = top_s[0]
        ex = [jnp.exp(s - m) for s in top_s]
        denom = functools.reduce(lambda a, b: a + b, ex)
        for s_k, c_k in zip(ex, top_c):
            c_int = c_k.astype(jnp.int32)
            a = (c_int >> 4).astype(F32)
            b = (c_int & (PEER_TOPK - 1)).astype(F32)
            keys1.append(_pick_row(i1c, rank_ids, a))
            keys2.append(_pick_row(i2c, rank_ids, b))
            gates.append(s_k / denom)
    key1_ref[...] = jnp.concatenate(keys1, axis=0).T
    key2_ref[...] = jnp.concatenate(keys2, axis=0).T
    gate_ref[...] = jnp.concatenate(gates, axis=0).T


def _route(scores_t):
    n = scores_t.shape[-1]
    tn = ROUTE_TILE
    assert n % tn == 0
    flat = jnp.asarray(_candidate_tables())
    return pl.pallas_call(
        _route_body,
        grid=(n // tn,),
        in_specs=[pl.BlockSpec((2 * PEER_HEADS, PEER_N_KEYS, tn), lambda i: (0, 0, i)),
                  pl.BlockSpec(flat.shape, lambda i: (0, 0))],
        out_specs=[pl.BlockSpec((tn, N_SEL), lambda i: (i, 0))] * 3,
        out_shape=[jax.ShapeDtypeStruct((n, N_SEL), F32)] * 3,
        compiler_params=_params(1),
        name="peer_route",
    )(scores_t, flat)


def _peer_dense_body(h_ref, key1_ref, key2_ref, gate_ref, x1_ref, gate2_ref, fg_ref, ut_ref, v_ref, o_ref, g_scr,
                     *, tm, blocks_per_step):
    k = pl.program_id(1)
    slab = tm + SUBLANES

    @pl.when(k == 0)
    def _():
        o_ref[...] = jnp.zeros_like(o_ref)
        sub = lax.broadcasted_iota(jnp.int32, (PEER_N_KEYS, N_SEL), 0).astype(F32).astype(BF16)
        zero = jnp.zeros((PEER_N_KEYS, N_SEL), BF16)
        one = jnp.ones((PEER_N_KEYS, N_SEL), BF16)

        def tokens(i, carry):
            for j in range(PEER_GATE_TOKENS_PER_TRIP):
                t = i * PEER_GATE_TOKENS_PER_TRIP + j
                key1 = key1_ref[pl.ds(t, 1), :].astype(BF16)
                key2 = key2_ref[pl.ds(t, 1), :].astype(BF16)
                gate = gate_ref[pl.ds(t, 1), :]
                hi = gate.astype(BF16)
                lo = (gate - hi.astype(F32)).astype(BF16)
                on1 = sub == key1
                lhs = jnp.concatenate([jnp.where(on1, hi, zero), jnp.where(on1, lo, zero)], axis=1)
                on2 = jnp.where(sub == key2, one, zero)
                g_t = lax.dot_general(lhs, jnp.concatenate([on2, on2], axis=1), (((1,), (1,)), ((), ())),
                                      preferred_element_type=F32)
                g_scr[pl.ds(t, PEER_N_KEYS, stride=slab), :] = g_t
            return carry

        lax.fori_loop(0, tm // PEER_GATE_TOKENS_PER_TRIP, tokens, 0)

    act = _dot(h_ref[...], ut_ref[...])
    g_blk = jnp.concatenate(
        [g_scr[pl.ds(pl.multiple_of((k * blocks_per_step + j) * slab, SUBLANES), tm), :]
         for j in range(blocks_per_step)], axis=1)
    coef = (g_blk * _gelu(act)).astype(BF16)
    o_ref[...] += _dot(coef, v_ref[...])

    @pl.when(k == pl.num_programs(1) - 1)
    def _():
        x2 = x1_ref[...] + gate2_ref[0] * o_ref[...]
        o_ref[...] = _rms(x2) * fg_ref[...]


def _peer_dense(h2_bf16, key1, key2, gate, x1, gate2, final_g, u_t, v, tokens_per_batch):
    n, d = x1.shape
    n_experts = v.shape[0]
    tm, eb = PEER_ROW_TILE, PEER_EXPERT_TILE
    assert tokens_per_batch % tm == 0 and n_experts % eb == 0 and eb % PEER_N_KEYS == 0
    tiles_per_batch = tokens_per_batch // tm
    n_blocks = n_experts // eb
    row = lambda w: pl.BlockSpec((tm, w), lambda i, k: (i, 0))
    once_per_tile = lambda w: pl.BlockSpec((tm, w), lambda i, k: (i, 0), pipeline_mode=pl.Buffered(1))
    return pl.pallas_call(
        functools.partial(_peer_dense_body, tm=tm, blocks_per_step=eb // PEER_N_KEYS),
        grid=(n // tm, n_blocks),
        in_specs=[once_per_tile(d), row(N_SEL), row(N_SEL), row(N_SEL), once_per_tile(d),
                  pl.BlockSpec((1, 1, d), lambda i, k: (i // tiles_per_batch, 0, 0)),
                  pl.BlockSpec((1, d), lambda i, k: (0, 0)),
                  pl.BlockSpec((d, eb), lambda i, k: (0, k)),
                  pl.BlockSpec((eb, d), lambda i, k: (k, 0))],
        out_specs=row(d),
        out_shape=jax.ShapeDtypeStruct((n, d), F32),
        scratch_shapes=[pltpu.VMEM((PEER_N_KEYS * (tm + SUBLANES), PEER_N_KEYS), F32)],
        compiler_params=_params(2),
        name="peer_dense",
    )(h2_bf16, key1, key2, gate, x1, gate2, final_g, u_t, v)


def kernel(x, c, ctx, c_ctx, ada_w, ada_b, norm_mix_g, w_in, hgrn_lb_logits, hgrn_norm_g, gla_gk_w, gla_gk_b, gla_norm_g, w_branch_hgrn, w_branch_gla, w_out, norm_ffn_g, peer_wq, peer_k1, peer_k2, peer_u, peer_v, final_g):
    bsz, seq, d = x.shape
    t_ctx = ctx.shape[1]
    depth = ada_w.shape[0]
    assert depth == 1 and d == D_MODEL and w_in.shape[-1] == D_IN
    l = 0
    row2 = lambda a: a.reshape(1, -1)

    lower = jnp.cumsum(jax.nn.softmax(hgrn_lb_logits, axis=0), axis=0)[l]
    lower = jnp.concatenate([lower, jnp.zeros((SUBLANES - 2, HG_KEY), F32)], axis=0)
    cvec = jnp.concatenate([c, c_ctx[None], jnp.zeros((SUBLANES - 1, d), F32)], axis=0)
    mod_all = _adaln(cvec, ada_w[l], ada_b[l])
    mod = [m.reshape(bsz, 1, d) for m in jnp.split(mod_all[:bsz], N_MOD, axis=-1)]
    mod_c = [jnp.broadcast_to(m.reshape(1, 1, d), (bsz, 1, d)) for m in jnp.split(mod_all[bsz:bsz + 1], N_MOD, axis=-1)]
    w_in_b = w_in[l].astype(BF16)
    zr = jnp.zeros((GLA_GATE_RANK, GLA_KEY), F32)
    wrank = jnp.concatenate([jnp.concatenate([gla_gk_w[l, 0], zr], axis=1),
                             jnp.concatenate([zr, gla_gk_w[l, 1]], axis=1)], axis=0).astype(BF16)
    brank = gla_gk_b[l].reshape(1, 2 * GLA_KEY)

    lat = _inproj(x, mod[0], mod[1], row2(norm_mix_g[l]), w_in_b, wrank, brank)
    cx = _inproj(ctx, mod_c[0], mod_c[1], row2(norm_mix_g[l]), w_in_b, wrank, brank)
    (hq, zf, zb, hi, hgate, gq, gk, gv, gg, glf, glb, m_hg, m_gla) = lat
    (_, czf, czb, chi, _, _, cgk, cgv, _, cglf, cglb, _, _) = cx

    y_hg = _scan_call(
        _hgrn_scan_body, HG_HEADS, HG_HEAD_V,
        [(hq, HEAD_K), (zf, HEAD_K), (zb, HEAD_K), (hi, HG_HEAD_V), (hgate, HG_HEAD_V)],
        [(czf, HEAD_K), (czb, HEAD_K), (chi, HG_HEAD_V)],
        [(lower, (SUBLANES, HEAD_K), lambda b, h: (0, h)), (row2(hgrn_norm_g[l]), (1, HG_HEAD_V), lambda b, h: (0, 0))],
        bsz, seq, t_ctx, "hgrn_scan")
    y_gla = _scan_call(
        _gla_scan_body, GLA_HEADS, GLA_HEAD_V,
        [(gq, HEAD_K), (gk, HEAD_K), (glf, HEAD_K), (glb, HEAD_K), (gv, GLA_HEAD_V), (gg, GLA_HEAD_V)],
        [(cgk, HEAD_K), (cglf, HEAD_K), (cglb, HEAD_K), (cgv, GLA_HEAD_V)],
        [(row2(gla_norm_g[l]), (1, GLA_HEAD_V), lambda b, h: (0, 0))],
        bsz, seq, t_ctx, "gla_scan")

    x1, h2, scores_t = _mixer_out(
        y_hg, y_gla, m_hg, m_gla, x, mod[2], mod[3], mod[4], row2(norm_ffn_g[l]),
        w_branch_hgrn[l].astype(BF16), w_branch_gla[l].astype(BF16), w_out[l].astype(BF16),
        peer_wq[l].astype(BF16), peer_k1[l].astype(BF16), peer_k2[l].astype(BF16))

    key1, key2, gate = _route(scores_t)
    n = bsz * seq
    out = _peer_dense(h2.reshape(n, d), key1, key2, gate, x1.reshape(n, d), mod[5], row2(final_g),
                      peer_u[l].astype(BF16).T, peer_v[l].astype(BF16), seq)
    return out.reshape(bsz, seq, d)
```

```python
import functools
import math

import jax
import jax.numpy as jnp
import numpy as np
from jax import lax
from jax.experimental import pallas as pl
from jax.experimental.pallas import tpu as pltpu

F32 = jnp.float32
BF16 = jnp.bfloat16

D_MODEL = 1024
GRID_W = 64
CTX_CHUNK = 64
N_MOD = 6
EPS = 1e-6

HG_HEADS = 8
HEAD_K = 128
HG_KEY = HG_HEADS * HEAD_K
HG_VAL = D_MODEL
HG_HEAD_V = HG_VAL // HG_HEADS
GLA_HEADS = 4
GLA_KEY = D_MODEL // 2
GLA_VAL = D_MODEL
GLA_HEAD_V = GLA_VAL // GLA_HEADS
GLA_GATE_RANK = 16
GLA_GATE_NORMALIZER = 16.0

C_HQ, C_ZF, C_ZB, C_HI, C_HGATE = 0, 1024, 2048, 3072, 4096
C_GQ, C_GK, C_GV, C_GG = 5120, 5632, 6144, 7168
C_RANK = 8192
C_MHG = C_RANK + 2 * GLA_GATE_RANK
C_MGLA = C_MHG + D_MODEL
D_IN = C_MGLA + D_MODEL

PEER_HEADS = 8
PEER_N_KEYS = 128
PEER_SUB_DIM = 128
PEER_TOPK = 16
N_SEL = PEER_HEADS * PEER_TOPK

SUBLANES = 8
VMEM_LIMIT = 56 * 1024 * 1024

ROW_TILE = 256
INPROJ_ROW_TILE = 256
ROUTE_TILE = 256
PEER_ROW_TILE = 512
PEER_EXPERT_TILE = 512
PEER_GATE_TOKENS_PER_TRIP = 32
SCAN_CHUNKS_PER_TRIP = 32


def _resident(shape):
    return pl.BlockSpec(shape, lambda *_: (0,) * len(shape), pipeline_mode=pl.Buffered(1))


def _params(n_axes):
    return pltpu.CompilerParams(dimension_semantics=("arbitrary",) * n_axes, vmem_limit_bytes=VMEM_LIMIT)


def _split3(a):
    p1 = a.astype(BF16)
    r1 = a - p1.astype(F32)
    p2 = r1.astype(BF16)
    p3 = (r1 - p2.astype(F32)).astype(BF16)
    return p1, p2, p3


def _dot(a, b):
    return jnp.dot(a, b, preferred_element_type=F32)


def _rms(x):
    return x * lax.rsqrt(jnp.mean(x * x, axis=-1, keepdims=True) + EPS)


def _gelu(x):
    return 0.5 * x * (1.0 + lax.erf(x * np.float32(2.0 ** -0.5)))


def _sigmoid(x):
    return 1.0 / (1.0 + jnp.exp(-x))


def _log_sigmoid(x):
    return jnp.minimum(x, 0.0) - jnp.log1p(jnp.exp(-jnp.abs(x)))


def _adaln_body(c_ref, w_ref, b_ref, o_ref):
    c = c_ref[...]
    s = (c * _sigmoid(c)).astype(BF16)
    o_ref[...] = _dot(s, w_ref[...].astype(BF16)) + b_ref[...]


def _adaln(cvec, w, b):
    rows, d = cvec.shape
    n_out = w.shape[1]
    return pl.pallas_call(
        _adaln_body,
        grid=(n_out // d,),
        in_specs=[pl.BlockSpec((rows, d), lambda j: (0, 0)),
                  pl.BlockSpec((d, d), lambda j: (0, j)),
                  pl.BlockSpec((1, d), lambda j: (0, j))],
        out_specs=pl.BlockSpec((rows, d), lambda j: (0, j)),
        out_shape=jax.ShapeDtypeStruct((rows, n_out), F32),
        compiler_params=_params(1),
        name="adaln",
    )(cvec, w, b.reshape(1, n_out))


def _inproj_body(x_ref, shift_ref, scale_ref, g_ref, w_ref, wrank_ref, brank_ref,
                 hq_ref, zf_ref, zb_ref, hi_ref, hgate_ref, gq_ref, gk_ref, gv_ref, gg_ref,
                 glf_ref, glb_ref, mhg_ref, mgla_ref):
    h = _rms(x_ref[0]) * g_ref[...] * (1.0 + scale_ref[0]) + shift_ref[0]
    hb = h.astype(BF16)

    def cols(start, width):
        return _dot(hb, w_ref[:, start:start + width])

    hq_ref[0] = cols(C_HQ, HG_KEY) * np.float32(HEAD_K ** -0.5)
    zf_ref[0] = cols(C_ZF, HG_KEY)
    zb_ref[0] = cols(C_ZB, HG_KEY)
    hi_ref[0] = cols(C_HI, HG_VAL)
    hgate_ref[0] = cols(C_HGATE, HG_VAL)
    gq_ref[0] = cols(C_GQ, GLA_KEY) * np.float32(HEAD_K ** -0.5)
    gk_ref[0] = cols(C_GK, GLA_KEY)
    gv_ref[0] = cols(C_GV, GLA_VAL)
    gg_ref[0] = cols(C_GG, GLA_VAL)
    mhg_ref[0] = cols(C_MHG, D_MODEL)
    mgla_ref[0] = cols(C_MGLA, D_MODEL)
    rank = cols(C_RANK, 2 * GLA_GATE_RANK).astype(BF16)
    pre = _dot(rank, wrank_ref[...]) + brank_ref[...]
    lg = _log_sigmoid(pre) / GLA_GATE_NORMALIZER
    glf_ref[0] = lg[:, :GLA_KEY]
    glb_ref[0] = lg[:, GLA_KEY:]


def _inproj(x, shift, scale, g, w_bf16, wrank, brank):
    bsz, t, d = x.shape
    tm = INPROJ_ROW_TILE
    assert t % tm == 0
    widths = (HG_KEY, HG_KEY, HG_KEY, HG_VAL, HG_VAL, GLA_KEY, GLA_KEY, GLA_VAL, GLA_VAL,
              GLA_KEY, GLA_KEY, D_MODEL, D_MODEL)
    row = lambda w: pl.BlockSpec((1, tm, w), lambda b, i: (b, i, 0))
    per_batch = pl.BlockSpec((1, 1, d), lambda b, i: (b, 0, 0))
    return pl.pallas_call(
        _inproj_body,
        grid=(bsz, t // tm),
        in_specs=[row(d), per_batch, per_batch, _resident((1, d)), _resident(w_bf16.shape),
                  _resident(wrank.shape), _resident(brank.shape)],
        out_specs=[row(w) for w in widths],
        out_shape=[jax.ShapeDtypeStruct((bsz, t, w), F32) for w in widths],
        compiler_params=_params(2),
        name="inproj",
    )(x, shift, scale, g, w_bf16, wrank, brank)


def _cumulative_decays(lgs, tri3):
    wide = jnp.concatenate(lgs, axis=1)
    b = _dot(tri3, jnp.concatenate(_split3(wide), axis=0))
    return [b[:, j * HEAD_K:(j + 1) * HEAD_K] for j in range(len(lgs))]


def _chunk_step(q, k, b, v, state, mask, ref_row, last_row, want_out):
    b_ref = b[ref_row:ref_row + 1]
    b_last = b[last_row:last_row + 1]
    vb = v.astype(BF16)
    o = None
    if want_out:
        qd = (q * jnp.exp(b - b_ref)).astype(BF16)
        kd = (k * jnp.exp(b_ref - b)).astype(BF16)
        scores = lax.dot_general(qd, kd, (((1,), (1,)), ((), ())), preferred_element_type=F32)
        scores = jnp.where(mask, scores, 0.0).astype(BF16)
        qs = (q * jnp.exp(b)).astype(BF16)
        o = _dot(scores, vb) + lax.dot_general(qs, state.astype(BF16), (((1,), (1,)), ((), ())),
                                               preferred_element_type=F32)
    kl = (k * jnp.exp(b_last - b)).astype(BF16)
    new_state = state * jnp.exp(b_last) + lax.dot_general(vb, kl, (((0,), (0,)), ((), ())),
                                                          preferred_element_type=F32)
    return o, new_state


def _scan_core(load_fwd, load_bwd, load_ctx_fwd, load_ctx_bwd, gate_ref, g_ref, y_ref,
               of_scr, ob_scr, sf_scr, sb_scr, n_chunks, n_ctx_chunks, dv):
    c = GRID_W
    ri = lax.broadcasted_iota(jnp.int32, (c, c), 0)
    ci = lax.broadcasted_iota(jnp.int32, (c, c), 1)
    lower = ci <= ri
    upper = ci >= ri
    tri_f = jnp.concatenate([lower.astype(BF16)] * 3, axis=1)
    tri_b = jnp.concatenate([upper.astype(BF16)] * 3, axis=1)
    fwd = dict(mask=lower, ref_row=c // 2, last_row=c - 1)
    bwd = dict(mask=upper, ref_row=c - 1 - c // 2, last_row=0)

    def run(load, rows, tri3, state, out_scr, kind):
        loaded = [load(r) for r in rows]
        bs = _cumulative_decays([lg for _, _, lg, _ in loaded], tri3)
        for r, (q, k, _, v), b in zip(rows, loaded, bs):
            o, state = _chunk_step(q, k, b, v, state, want_out=out_scr is not None, **kind)
            if out_scr is not None:
                out_scr[pl.ds(r, c), :] = o
        return state

    zero = jnp.zeros((dv, HEAD_K), F32)
    ctx_rows = [i * CTX_CHUNK for i in range(n_ctx_chunks)]
    sf_scr[...] = run(load_ctx_fwd, ctx_rows, tri_f, zero, None, fwd)
    sb_scr[...] = run(load_ctx_bwd, ctx_rows[::-1], tri_b, zero, None, bwd)

    per_trip = math.gcd(n_chunks, SCAN_CHUNKS_PER_TRIP)

    def step(i, carry):
        ns = [i * per_trip + j for j in range(per_trip)]
        rows_f = [pl.multiple_of(n * c, c) for n in ns]
        rows_b = [pl.multiple_of((n_chunks - 1 - n) * c, c) for n in ns]
        sf_scr[...] = run(load_fwd, rows_f, tri_f, sf_scr[...], of_scr, fwd)
        sb_scr[...] = run(load_bwd, rows_b, tri_b, sb_scr[...], ob_scr, bwd)
        return carry

    lax.fori_loop(0, n_chunks // per_trip, step, 0)

    o = of_scr[...] + ob_scr[...]
    gate = gate_ref[0]
    y_ref[0] = _rms(o) * g_ref[...] * (gate * _sigmoid(gate))


def _hgrn_scan_body(q_ref, zf_ref, zb_ref, v_ref, gate_ref, czf_ref, czb_ref, cv_ref, lb_ref, g_ref,
                    y_ref, of_scr, ob_scr, sf_scr, sb_scr, *, n_chunks, n_ctx_chunks):
    def forget(z, lower):
        f = lower + (1.0 - lower) * _sigmoid(z)
        return 1.0 - f, jnp.log(f)

    def loader(q_r, z_r, v_r, row):
        lower = lb_ref[row:row + 1, :]

        def load(rows):
            sl = pl.ds(rows, GRID_W)
            k, lg = forget(z_r[0, sl, :], lower)
            q = q_r[0, sl, :] if q_r is not None else None
            return q, k, lg, v_r[0, sl, :]
        return load

    _scan_core(loader(q_ref, zf_ref, v_ref, 0), loader(q_ref, zb_ref, v_ref, 1),
               loader(None, czf_ref, cv_ref, 0), loader(None, czb_ref, cv_ref, 1),
               gate_ref, g_ref, y_ref, of_scr, ob_scr, sf_scr, sb_scr, n_chunks, n_ctx_chunks, HG_HEAD_V)


def _gla_scan_body(q_ref, k_ref, lf_ref, lbk_ref, v_ref, gate_ref, ck_ref, clf_ref, clb_ref, cv_ref, g_ref,
                   y_ref, of_scr, ob_scr, sf_scr, sb_scr, *, n_chunks, n_ctx_chunks):
    def loader(q_r, k_r, l_r, v_r):
        def load(rows):
            sl = pl.ds(rows, GRID_W)
            q = q_r[0, sl, :] if q_r is not None else None
            return q, k_r[0, sl, :], l_r[0, sl, :], v_r[0, sl, :]
        return load

    _scan_core(loader(q_ref, k_ref, lf_ref, v_ref), loader(q_ref, k_ref, lbk_ref, v_ref),
               loader(None, ck_ref, clf_ref, cv_ref), loader(None, ck_ref, clb_ref, cv_ref),
               gate_ref, g_ref, y_ref, of_scr, ob_scr, sf_scr, sb_scr, n_chunks, n_ctx_chunks, GLA_HEAD_V)


def _scan_call(body, n_heads, dv, lat, ctx, small, bsz, t, t_ctx, name):
    head = lambda rows, w: pl.BlockSpec((1, rows, w), lambda b, h: (b, 0, h))
    in_specs = ([head(t, w) for _, w in lat] + [head(t_ctx, w) for _, w in ctx]
                + [pl.BlockSpec(bs, im) for _, bs, im in small])
    return pl.pallas_call(
        functools.partial(body, n_chunks=t // GRID_W, n_ctx_chunks=t_ctx // CTX_CHUNK),
        grid=(bsz, n_heads),
        in_specs=in_specs,
        out_specs=head(t, dv),
        out_shape=jax.ShapeDtypeStruct((bsz, t, n_heads * dv), F32),
        scratch_shapes=[pltpu.VMEM((t, dv), F32), pltpu.VMEM((t, dv), F32),
                        pltpu.VMEM((dv, HEAD_K), F32), pltpu.VMEM((dv, HEAD_K), F32)],
        compiler_params=_params(2),
        name=name,
    )(*[a for a, _ in lat], *[a for a, _ in ctx], *[a for a, _, _ in small])


def _mixer_out_body(yh_ref, yg_ref, mh_ref, mg_ref, x_ref, gate1_ref, shift2_ref, scale2_ref, g2_ref,
                    wbh_ref, wbg_ref, wo_ref, wq_ref, k1_ref, k2_ref,
                    x1_ref, h2_ref, st_ref):
    yh = _dot(yh_ref[0].astype(BF16), wbh_ref[...])
    yg = _dot(yg_ref[0].astype(BF16), wbg_ref[...])
    y = _sigmoid(mh_ref[0]) * yh + _sigmoid(mg_ref[0]) * yg
    x1 = x_ref[0] + gate1_ref[0] * _dot(y.astype(BF16), wo_ref[...])
    x1_ref[0] = x1
    h2 = _rms(x1) * g2_ref[...] * (1.0 + scale2_ref[0]) + shift2_ref[0]
    h2b = h2.astype(BF16)
    h2_ref[0] = h2b
    q = _dot(h2b, wq_ref[...]).astype(BF16)
    for hh in range(2 * PEER_HEADS):
        keys = k1_ref if hh % 2 == 0 else k2_ref
        st_ref[hh] = lax.dot_general(keys[...], q[:, hh * PEER_SUB_DIM:(hh + 1) * PEER_SUB_DIM],
                                     (((1,), (1,)), ((), ())), preferred_element_type=F32)


def _mixer_out(y_hg, y_gla, m_hg, m_gla, x, gate1, shift2, scale2, g2, wbh, wbg, wo, wq, k1, k2):
    bsz, t, d = x.shape
    tm = ROW_TILE
    assert t % tm == 0
    row = pl.BlockSpec((1, tm, d), lambda b, i: (b, i, 0))
    per_batch = pl.BlockSpec((1, 1, d), lambda b, i: (b, 0, 0))
    tiles_per_batch = t // tm
    n = bsz * t
    return pl.pallas_call(
        _mixer_out_body,
        grid=(bsz, tiles_per_batch),
        in_specs=[row, row, row, row, row, per_batch, per_batch, per_batch, _resident((1, d)),
                  _resident(wbh.shape), _resident(wbg.shape), _resident(wo.shape), _resident(wq.shape),
                  _resident(k1.shape), _resident(k2.shape)],
        out_specs=[row, row,
                   pl.BlockSpec((2 * PEER_HEADS, PEER_N_KEYS, tm), lambda b, i: (0, 0, b * tiles_per_batch + i))],
        out_shape=[jax.ShapeDtypeStruct((bsz, t, d), F32), jax.ShapeDtypeStruct((bsz, t, d), BF16),
                   jax.ShapeDtypeStruct((2 * PEER_HEADS, PEER_N_KEYS, n), F32)],
        compiler_params=_params(2),
        name="mixer_out",
    )(y_hg, y_gla, m_hg, m_gla, x, gate1, shift2, scale2, g2, wbh, wbg, wo, wq, k1, k2)


_CAND_GROUPS = (("a", 0, 0), ("a", 0, 8), ("a", 1, 0), ("b", 0, 8),
                ("b", 0, 0), ("b", 1, 0), ("b", 2, 0), ("b", 3, 0), ("b", 4, 0))
_CAND_ROWS = len(_CAND_GROUPS) * SUBLANES


def _candidate_tables():
    flat = np.full((_CAND_ROWS, 1), -1, np.float32)
    seen = set()
    for g, (kind, fixed, start) in enumerate(_CAND_GROUPS):
        for r in range(SUBLANES):
            a, b = (fixed, start + r) if kind == "a" else (start + r, fixed)
            if (a + 1) * (b + 1) <= PEER_TOPK and (a, b) not in seen:
                seen.add((a, b))
                flat[g * SUBLANES + r, 0] = a * PEER_TOPK + b
    needed = {(a, b) for a in range(PEER_TOPK) for b in range(PEER_TOPK) if (a + 1) * (b + 1) <= PEER_TOPK}
    assert seen == needed
    return flat


def _extract_top(x, key, rounds):
    big = np.float32(2 ** 30)
    vals, keys = [], []
    for _ in range(rounds):
        m = jnp.max(x, axis=0, keepdims=True)
        sel = jnp.min(jnp.where(x == m, key, big), axis=0, keepdims=True)
        vals.append(m)
        keys.append(sel)
        x = jnp.where(key == sel, -jnp.inf, x)
    return vals, keys


def _extract_top_paired(x, rounds):
    half = x.shape[0] // 2
    big = np.float32(2 ** 30)
    ids = lax.broadcasted_iota(jnp.int32, (half, x.shape[1]), 0).astype(F32)
    a, b = x[:half], x[half:]
    first = a >= b
    win, lose = jnp.where(first, a, b), jnp.where(first, b, a)
    win_id, lose_id = jnp.where(first, ids, ids + half), jnp.where(first, ids + half, ids)
    vals, keys = [], []
    for _ in range(rounds):
        m = jnp.max(win, axis=0, keepdims=True)
        sel = jnp.min(jnp.where(win == m, win_id, big), axis=0, keepdims=True)
        vals.append(m)
        keys.append(sel)
        taken = win_id == sel
        win = jnp.where(taken, lose, win)
        win_id = jnp.where(taken, lose_id, win_id)
        lose = jnp.where(taken, -jnp.inf, lose)
    return vals, keys


def _pick_row(table, row_ids, sel):
    return jnp.sum(jnp.where(row_ids == sel, table, 0), axis=0, keepdims=True)


def _route_body(st_ref, flat_ref, key1_ref, key2_ref, gate_ref):
    tn = st_ref.shape[-1]
    rank_ids = lax.broadcasted_iota(jnp.int32, (PEER_TOPK, tn), 0).astype(F32)
    flat = jnp.broadcast_to(flat_ref[...], (_CAND_ROWS, tn))
    valid = flat >= 0.0
    keys1, keys2, gates = [], [], []
    for h in range(PEER_HEADS):
        v1, i1 = _extract_top_paired(st_ref[2 * h], PEER_TOPK)
        v2, i2 = _extract_top_paired(st_ref[2 * h + 1], PEER_TOPK)
        v1c, i1c = jnp.concatenate(v1, axis=0), jnp.concatenate(i1, axis=0)
        v2c, i2c = jnp.concatenate(v2, axis=0), jnp.concatenate(i2, axis=0)
        cand = jnp.concatenate(
            [v1[fixed] + v2c[start:start + SUBLANES] if kind == "a" else v1c[start:start + SUBLANES] + v2[fixed]
             for kind, fixed, start in _CAND_GROUPS], axis=0)
        cand = jnp.where(valid, cand, -jnp.inf)
        top_s, top_c = _extract_top(cand, flat, PEER_TOPK)
        m = top_s[0]
        ex = [jnp.exp(s - m) for s in top_s]
        denom = functools.reduce(lambda a, b: a + b, ex)
        for s_k, c_k in zip(ex, top_c):
            c_int = c_k.astype(jnp.int32)
            a = (c_int >> 4).astype(F32)
            b = (c_int & (PEER_TOPK - 1)).astype(F32)
            keys1.append(_pick_row(i1c, rank_ids, a))
            keys2.append(_pick_row(i2c, rank_ids, b))
            gates.append(s_k / denom)
    key1_ref[...] = jnp.concatenate(keys1, axis=0).T
    key2_ref[...] = jnp.concatenate(keys2, axis=0).T
    gate_ref[...] = jnp.concatenate(gates, axis=0).T


def _route(scores_t):
    n = scores_t.shape[-1]
    tn = ROUTE_TILE
    assert n % tn == 0
    flat = jnp.asarray(_candidate_tables())
    return pl.pallas_call(
        _route_body,
        grid=(n // tn,),
        in_specs=[pl.BlockSpec((2 * PEER_HEADS, PEER_N_KEYS, tn), lambda i: (0, 0, i)),
                  pl.BlockSpec(flat.shape, lambda i: (0, 0))],
        out_specs=[pl.BlockSpec((tn, N_SEL), lambda i: (i, 0))] * 3,
        out_shape=[jax.ShapeDtypeStruct((n, N_SEL), F32)] * 3,
        compiler_params=_params(1),
        name="peer_route",
    )(scores_t, flat)


def _peer_dense_body(h_ref, key1_ref, key2_ref, gate_ref, x1_ref, gate2_ref, fg_ref, ut_ref, v_ref, o_ref, g_scr,
                     *, tm, blocks_per_step):
    k = pl.program_id(1)
    slab = tm + SUBLANES

    @pl.when(k == 0)
    def _():
        o_ref[...] = jnp.zeros_like(o_ref)
        sub = lax.broadcasted_iota(jnp.int32, (PEER_N_KEYS, N_SEL), 0).astype(F32).astype(BF16)
        zero = jnp.zeros((PEER_N_KEYS, N_SEL), BF16)
        one = jnp.ones((PEER_N_KEYS, N_SEL), BF16)

        def tokens(i, carry):
            for j in range(PEER_GATE_TOKENS_PER_TRIP):
                t = i * PEER_GATE_TOKENS_PER_TRIP + j
                key1 = key1_ref[pl.ds(t, 1), :].astype(BF16)
                key2 = key2_ref[pl.ds(t, 1), :].astype(BF16)
                gate = gate_ref[pl.ds(t, 1), :]
                hi = gate.astype(BF16)
                lo = (gate - hi.astype(F32)).astype(BF16)
                on1 = sub == key1
                lhs = jnp.concatenate([jnp.where(on1, hi, zero), jnp.where(on1, lo, zero)], axis=1)
                on2 = jnp.where(sub == key2, one, zero)
                g_t = lax.dot_general(lhs, jnp.concatenate([on2, on2], axis=1), (((1,), (1,)), ((), ())),
                                      preferred_element_type=F32)
                g_scr[pl.ds(t, PEER_N_KEYS, stride=slab), :] = g_t
            return carry

        lax.fori_loop(0, tm // PEER_GATE_TOKENS_PER_TRIP, tokens, 0)

    act = _dot(h_ref[...], ut_ref[...])
    g_blk = jnp.concatenate(
        [g_scr[pl.ds(pl.multiple_of((k * blocks_per_step + j) * slab, SUBLANES), tm), :]
         for j in range(blocks_per_step)], axis=1)
    coef = (g_blk * _gelu(act)).astype(BF16)
    o_ref[...] += _dot(coef, v_ref[...])

    @pl.when(k == pl.num_programs(1) - 1)
    def _():
        x2 = x1_ref[...] + gate2_ref[0] * o_ref[...]
        o_ref[...] = _rms(x2) * fg_ref[...]


def _peer_dense(h2_bf16, key1, key2, gate, x1, gate2, final_g, u_t, v, tokens_per_batch):
    n, d = x1.shape
    n_experts = v.shape[0]
    tm, eb = PEER_ROW_TILE, PEER_EXPERT_TILE
    assert tokens_per_batch % tm == 0 and n_experts % eb == 0 and eb % PEER_N_KEYS == 0
    tiles_per_batch = tokens_per_batch // tm
    n_blocks = n_experts // eb
    row = lambda w: pl.BlockSpec((tm, w), lambda i, k: (i, 0))
    once_per_tile = lambda w: pl.BlockSpec((tm, w), lambda i, k: (i, 0), pipeline_mode=pl.Buffered(1))
    return pl.pallas_call(
        functools.partial(_peer_dense_body, tm=tm, blocks_per_step=eb // PEER_N_KEYS),
        grid=(n // tm, n_blocks),
        in_specs=[once_per_tile(d), row(N_SEL), row(N_SEL), row(N_SEL), once_per_tile(d),
                  pl.BlockSpec((1, 1, d), lambda i, k: (i // tiles_per_batch, 0, 0)),
                  pl.BlockSpec((1, d), lambda i, k: (0, 0)),
                  pl.BlockSpec((d, eb), lambda i, k: (0, k)),
                  pl.BlockSpec((eb, d), lambda i, k: (k, 0))],
        out_specs=row(d),
        out_shape=jax.ShapeDtypeStruct((n, d), F32),
        scratch_shapes=[pltpu.VMEM((PEER_N_KEYS * (tm + SUBLANES), PEER_N_KEYS), F32)],
        compiler_params=_params(2),
        name="peer_dense",
    )(h2_bf16, key1, key2, gate, x1, gate2, final_g, u_t, v)


def kernel(x, c, ctx, c_ctx, ada_w, ada_b, norm_mix_g, w_in, hgrn_lb_logits, hgrn_norm_g, gla_gk_w, gla_gk_b, gla_norm_g, w_branch_hgrn, w_branch_gla, w_out, norm_ffn_g, peer_wq, peer_k1, peer_k2, peer_u, peer_v, final_g):
    bsz, seq, d = x.shape
    t_ctx = ctx.shape[1]
    depth = ada_w.shape[0]
    assert depth == 1 and d == D_MODEL and w_in.shape[-1] == D_IN
    l = 0
    row2 = lambda a: a.reshape(1, -1)

    lower = jnp.cumsum(jax.nn.softmax(hgrn_lb_logits, axis=0), axis=0)[l]
    lower = jnp.concatenate([lower, jnp.zeros((SUBLANES - 2, HG_KEY), F32)], axis=0)
    cvec = jnp.concatenate([c, c_ctx[None], jnp.zeros((SUBLANES - 1, d), F32)], axis=0)
    mod_all = _adaln(cvec, ada_w[l], ada_b[l])
    mod = [m.reshape(bsz, 1, d) for m in jnp.split(mod_all[:bsz], N_MOD, axis=-1)]
    mod_c = [jnp.broadcast_to(m.reshape(1, 1, d), (bsz, 1, d)) for m in jnp.split(mod_all[bsz:bsz + 1], N_MOD, axis=-1)]
    w_in_b = w_in[l].astype(BF16)
    zr = jnp.zeros((GLA_GATE_RANK, GLA_KEY), F32)
    wrank = jnp.concatenate([jnp.concatenate([gla_gk_w[l, 0], zr], axis=1),
                             jnp.concatenate([zr, gla_gk_w[l, 1]], axis=1)], axis=0).astype(BF16)
    brank = gla_gk_b[l].reshape(1, 2 * GLA_KEY)

    lat = _inproj(x, mod[0], mod[1], row2(norm_mix_g[l]), w_in_b, wrank, brank)
    cx = _inproj(ctx, mod_c[0], mod_c[1], row2(norm_mix_g[l]), w_in_b, wrank, brank)
    (hq, zf, zb, hi, hgate, gq, gk, gv, gg, glf, glb, m_hg, m_gla) = lat
    (_, czf, czb, chi, _, _, cgk, cgv, _, cglf, cglb, _, _) = cx

    y_hg = _scan_call(
        _hgrn_scan_body, HG_HEADS, HG_HEAD_V,
        [(hq, HEAD_K), (zf, HEAD_K), (zb, HEAD_K), (hi, HG_HEAD_V), (hgate, HG_HEAD_V)],
        [(czf, HEAD_K), (czb, HEAD_K), (chi, HG_HEAD_V)],
        [(lower, (SUBLANES, HEAD_K), lambda b, h: (0, h)), (row2(hgrn_norm_g[l]), (1, HG_HEAD_V), lambda b, h: (0, 0))],
        bsz, seq, t_ctx, "hgrn_scan")
    y_gla = _scan_call(
        _gla_scan_body, GLA_HEADS, GLA_HEAD_V,
        [(gq, HEAD_K), (gk, HEAD_K), (glf, HEAD_K), (glb, HEAD_K), (gv, GLA_HEAD_V), (gg, GLA_HEAD_V)],
        [(cgk, HEAD_K), (cglf, HEAD_K), (cglb, HEAD_K), (cgv, GLA_HEAD_V)],
        [(row2(gla_norm_g[l]), (1, GLA_HEAD_V), lambda b, h: (0, 0))],
        bsz, seq, t_ctx, "gla_scan")

    x1, h2, scores_t = _mixer_out(
        y_hg, y_gla, m_hg, m_gla, x, mod[2], mod[3], mod[4], row2(norm_ffn_g[l]),
        w_branch_hgrn[l].astype(BF16), w_branch_gla[l].astype(BF16), w_out[l].astype(BF16),
        peer_wq[l].astype(BF16), peer_k1[l].astype(BF16), peer_k2[l].astype(BF16))

    key1, key2, gate = _route(scores_t)
    n = bsz * seq
    out = _peer_dense(h2.reshape(n, d), key1, key2, gate, x1.reshape(n, d), mod[5], row2(final_g),
                      peer_u[l].astype(BF16).T, peer_v[l].astype(BF16), seq)
    return out.reshape(bsz, seq, d)
```

```python
import functools
import math

import jax
import jax.numpy as jnp
import numpy as np
from jax import lax
from jax.experimental import pallas as pl
from jax.experimental.pallas import tpu as pltpu

F32 = jnp.float32
BF16 = jnp.bfloat16

D_MODEL = 1024
GRID_W = 64
CTX_CHUNK = 64
N_MOD = 6
EPS = 1e-6

HG_HEADS = 8
HEAD_K = 128
HG_KEY = HG_HEADS * HEAD_K
HG_VAL = D_MODEL
HG_HEAD_V = HG_VAL // HG_HEADS
GLA_HEADS = 4
GLA_KEY = D_MODEL // 2
GLA_VAL = D_MODEL
GLA_HEAD_V = GLA_VAL // GLA_HEADS
GLA_GATE_RANK = 16
GLA_GATE_NORMALIZER = 16.0

C_HQ, C_ZF, C_ZB, C_HI, C_HGATE = 0, 1024, 2048, 3072, 4096
C_GQ, C_GK, C_GV, C_GG = 5120, 5632, 6144, 7168
C_RANK = 8192
C_MHG = C_RANK + 2 * GLA_GATE_RANK
C_MGLA = C_MHG + D_MODEL
D_IN = C_MGLA + D_MODEL

PEER_HEADS = 8
PEER_N_KEYS = 128
PEER_SUB_DIM = 128
PEER_TOPK = 16
N_SEL = PEER_HEADS * PEER_TOPK

SUBLANES = 8
VMEM_LIMIT = 56 * 1024 * 1024

ROW_TILE = 256
INPROJ_ROW_TILE = 256
ROUTE_TILE = 256
PEER_ROW_TILE = 512
PEER_EXPERT_TILE = 1024
PEER_GATE_TOKENS_PER_TRIP = 32
SCAN_CHUNKS_PER_TRIP = 32


def _resident(shape):
    return pl.BlockSpec(shape, lambda *_: (0,) * len(shape), pipeline_mode=pl.Buffered(1))


def _params(n_axes):
    return pltpu.CompilerParams(dimension_semantics=("arbitrary",) * n_axes, vmem_limit_bytes=VMEM_LIMIT)


def _split3(a):
    p1 = a.astype(BF16)
    r1 = a - p1.astype(F32)
    p2 = r1.astype(BF16)
    p3 = (r1 - p2.astype(F32)).astype(BF16)
    return p1, p2, p3


def _dot(a, b):
    return jnp.dot(a, b, preferred_element_type=F32)


def _rms(x):
    return x * lax.rsqrt(jnp.mean(x * x, axis=-1, keepdims=True) + EPS)


def _gelu(x):
    return 0.5 * x * (1.0 + lax.erf(x * np.float32(2.0 ** -0.5)))


def _sigmoid(x):
    return 1.0 / (1.0 + jnp.exp(-x))


def _log_sigmoid(x):
    return jnp.minimum(x, 0.0) - jnp.log1p(jnp.exp(-jnp.abs(x)))


def _adaln_body(c_ref, w_ref, b_ref, o_ref):
    c = c_ref[...]
    s = (c * _sigmoid(c)).astype(BF16)
    o_ref[...] = _dot(s, w_ref[...].astype(BF16)) + b_ref[...]


def _adaln(cvec, w, b):
    rows, d = cvec.shape
    n_out = w.shape[1]
    return pl.pallas_call(
        _adaln_body,
        grid=(n_out // d,),
        in_specs=[pl.BlockSpec((rows, d), lambda j: (0, 0)),
                  pl.BlockSpec((d, d), lambda j: (0, j)),
                  pl.BlockSpec((1, d), lambda j: (0, j))],
        out_specs=pl.BlockSpec((rows, d), lambda j: (0, j)),
        out_shape=jax.ShapeDtypeStruct((rows, n_out), F32),
        compiler_params=_params(1),
        name="adaln",
    )(cvec, w, b.reshape(1, n_out))


def _inproj_body(x_ref, shift_ref, scale_ref, g_ref, w_ref, wrank_ref, brank_ref,
                 hq_ref, zf_ref, zb_ref, hi_ref, hgate_ref, gq_ref, gk_ref, gv_ref, gg_ref,
                 glf_ref, glb_ref, mhg_ref, mgla_ref):
    h = _rms(x_ref[0]) * g_ref[...] * (1.0 + scale_ref[0]) + shift_ref[0]
    hb = h.astype(BF16)

    def cols(start, width):
        return _dot(hb, w_ref[:, start:start + width])

    hq_ref[0] = cols(C_HQ, HG_KEY) * np.float32(HEAD_K ** -0.5)
    zf_ref[0] = cols(C_ZF, HG_KEY)
    zb_ref[0] = cols(C_ZB, HG_KEY)
    hi_ref[0] = cols(C_HI, HG_VAL)
    hgate_ref[0] = cols(C_HGATE, HG_VAL)
    gq_ref[0] = cols(C_GQ, GLA_KEY) * np.float32(HEAD_K ** -0.5)
    gk_ref[0] = cols(C_GK, GLA_KEY)
    gv_ref[0] = cols(C_GV, GLA_VAL)
    gg_ref[0] = cols(C_GG, GLA_VAL)
    mhg_ref[0] = cols(C_MHG, D_MODEL)
    mgla_ref[0] = cols(C_MGLA, D_MODEL)
    rank = cols(C_RANK, 2 * GLA_GATE_RANK).astype(BF16)
    pre = _dot(rank, wrank_ref[...]) + brank_ref[...]
    lg = _log_sigmoid(pre) / GLA_GATE_NORMALIZER
    glf_ref[0] = lg[:, :GLA_KEY]
    glb_ref[0] = lg[:, GLA_KEY:]


def _inproj(x, shift, scale, g, w_bf16, wrank, brank):
    bsz, t, d = x.shape
    tm = INPROJ_ROW_TILE
    assert t % tm == 0
    widths = (HG_KEY, HG_KEY, HG_KEY, HG_VAL, HG_VAL, GLA_KEY, GLA_KEY, GLA_VAL, GLA_VAL,
              GLA_KEY, GLA_KEY, D_MODEL, D_MODEL)
    row = lambda w: pl.BlockSpec((1, tm, w), lambda b, i: (b, i, 0))
    per_batch = pl.BlockSpec((1, 1, d), lambda b, i: (b, 0, 0))
    return pl.pallas_call(
        _inproj_body,
        grid=(bsz, t // tm),
        in_specs=[row(d), per_batch, per_batch, _resident((1, d)), _resident(w_bf16.shape),
                  _resident(wrank.shape), _resident(brank.shape)],
        out_specs=[row(w) for w in widths],
        out_shape=[jax.ShapeDtypeStruct((bsz, t, w), F32) for w in widths],
        compiler_params=_params(2),
        name="inproj",
    )(x, shift, scale, g, w_bf16, wrank, brank)


def _cumulative_decays(lgs, tri3):
    wide = jnp.concatenate(lgs, axis=1)
    b = _dot(tri3, jnp.concatenate(_split3(wide), axis=0))
    return [b[:, j * HEAD_K:(j + 1) * HEAD_K] for j in range(len(lgs))]


def _chunk_step(q, k, b, v, state, mask, ref_row, last_row, want_out):
    b_ref = b[ref_row:ref_row + 1]
    b_last = b[last_row:last_row + 1]
    vb = v.astype(BF16)
    o = None
    if want_out:
        qd = (q * jnp.exp(b - b_ref)).astype(BF16)
        kd = (k * jnp.exp(b_ref - b)).astype(BF16)
        scores = lax.dot_general(qd, kd, (((1,), (1,)), ((), ())), preferred_element_type=F32)
        scores = jnp.where(mask, scores, 0.0).astype(BF16)
        qs = (q * jnp.exp(b)).astype(BF16)
        o = _dot(scores, vb) + lax.dot_general(qs, state.astype(BF16), (((1,), (1,)), ((), ())),
                                               preferred_element_type=F32)
    kl = (k * jnp.exp(b_last - b)).astype(BF16)
    new_state = state * jnp.exp(b_last) + lax.dot_general(vb, kl, (((0,), (0,)), ((), ())),
                                                          preferred_element_type=F32)
    return o, new_state


def _scan_core(load_fwd, load_bwd, load_ctx_fwd, load_ctx_bwd, gate_ref, g_ref, y_ref,
               of_scr, ob_scr, sf_scr, sb_scr, n_chunks, n_ctx_chunks, dv):
    c = GRID_W
    ri = lax.broadcasted_iota(jnp.int32, (c, c), 0)
    ci = lax.broadcasted_iota(jnp.int32, (c, c), 1)
    lower = ci <= ri
    upper = ci >= ri
    tri_f = jnp.concatenate([lower.astype(BF16)] * 3, axis=1)
    tri_b = jnp.concatenate([upper.astype(BF16)] * 3, axis=1)
    fwd = dict(mask=lower, ref_row=c // 2, last_row=c - 1)
    bwd = dict(mask=upper, ref_row=c - 1 - c // 2, last_row=0)

    def run(load, rows, tri3, state, out_scr, kind):
        loaded = [load(r) for r in rows]
        bs = _cumulative_decays([lg for _, _, lg, _ in loaded], tri3)
        for r, (q, k, _, v), b in zip(rows, loaded, bs):
            o, state = _chunk_step(q, k, b, v, state, want_out=out_scr is not None, **kind)
            if out_scr is not None:
                out_scr[pl.ds(r, c), :] = o
        return state

    zero = jnp.zeros((dv, HEAD_K), F32)
    ctx_rows = [i * CTX_CHUNK for i in range(n_ctx_chunks)]
    sf_scr[...] = run(load_ctx_fwd, ctx_rows, tri_f, zero, None, fwd)
    sb_scr[...] = run(load_ctx_bwd, ctx_rows[::-1], tri_b, zero, None, bwd)

    per_trip = math.gcd(n_chunks, SCAN_CHUNKS_PER_TRIP)

    def step(i, carry):
        ns = [i * per_trip + j for j in range(per_trip)]
        rows_f = [pl.multiple_of(n * c, c) for n in ns]
        rows_b = [pl.multiple_of((n_chunks - 1 - n) * c, c) for n in ns]
        sf_scr[...] = run(load_fwd, rows_f, tri_f, sf_scr[...], of_scr, fwd)
        sb_scr[...] = run(load_bwd, rows_b, tri_b, sb_scr[...], ob_scr, bwd)
        return carry

    lax.fori_loop(0, n_chunks // per_trip, step, 0)

    o = of_scr[...] + ob_scr[...]
    gate = gate_ref[0]
    y_ref[0] = _rms(o) * g_ref[...] * (gate * _sigmoid(gate))


def _hgrn_scan_body(q_ref, zf_ref, zb_ref, v_ref, gate_ref, czf_ref, czb_ref, cv_ref, lb_ref, g_ref,
                    y_ref, of_scr, ob_scr, sf_scr, sb_scr, *, n_chunks, n_ctx_chunks):
    def forget(z, lower):
        f = lower + (1.0 - lower) * _sigmoid(z)
        return 1.0 - f, jnp.log(f)

    def loader(q_r, z_r, v_r, row):
        lower = lb_ref[row:row + 1, :]

        def load(rows):
            sl = pl.ds(rows, GRID_W)
            k, lg = forget(z_r[0, sl, :], lower)
            q = q_r[0, sl, :] if q_r is not None else None
            return q, k, lg, v_r[0, sl, :]
        return load

    _scan_core(loader(q_ref, zf_ref, v_ref, 0), loader(q_ref, zb_ref, v_ref, 1),
               loader(None, czf_ref, cv_ref, 0), loader(None, czb_ref, cv_ref, 1),
               gate_ref, g_ref, y_ref, of_scr, ob_scr, sf_scr, sb_scr, n_chunks, n_ctx_chunks, HG_HEAD_V)


def _gla_scan_body(q_ref, k_ref, lf_ref, lbk_ref, v_ref, gate_ref, ck_ref, clf_ref, clb_ref, cv_ref, g_ref,
                   y_ref, of_scr, ob_scr, sf_scr, sb_scr, *, n_chunks, n_ctx_chunks):
    def loader(q_r, k_r, l_r, v_r):
        def load(rows):
            sl = pl.ds(rows, GRID_W)
            q = q_r[0, sl, :] if q_r is not None else None
            return q, k_r[0, sl, :], l_r[0, sl, :], v_r[0, sl, :]
        return load

    _scan_core(loader(q_ref, k_ref, lf_ref, v_ref), loader(q_ref, k_ref, lbk_ref, v_ref),
               loader(None, ck_ref, clf_ref, cv_ref), loader(None, ck_ref, clb_ref, cv_ref),
               gate_ref, g_ref, y_ref, of_scr, ob_scr, sf_scr, sb_scr, n_chunks, n_ctx_chunks, GLA_HEAD_V)


def _scan_call(body, n_heads, dv, lat, ctx, small, bsz, t, t_ctx, name):
    head = lambda rows, w: pl.BlockSpec((1, rows, w), lambda b, h: (b, 0, h))
    in_specs = ([head(t, w) for _, w in lat] + [head(t_ctx, w) for _, w in ctx]
                + [pl.BlockSpec(bs, im) for _, bs, im in small])
    return pl.pallas_call(
        functools.partial(body, n_chunks=t // GRID_W, n_ctx_chunks=t_ctx // CTX_CHUNK),
        grid=(bsz, n_heads),
        in_specs=in_specs,
        out_specs=head(t, dv),
        out_shape=jax.ShapeDtypeStruct((bsz, t, n_heads * dv), F32),
        scratch_shapes=[pltpu.VMEM((t, dv), F32), pltpu.VMEM((t, dv), F32),
                        pltpu.VMEM((dv, HEAD_K), F32), pltpu.VMEM((dv, HEAD_K), F32)],
        compiler_params=_params(2),
        name=name,
    )(*[a for a, _ in lat], *[a for a, _ in ctx], *[a for a, _, _ in small])


def _mixer_out_body(yh_ref, yg_ref, mh_ref, mg_ref, x_ref, gate1_ref, shift2_ref, scale2_ref, g2_ref,
                    wbh_ref, wbg_ref, wo_ref, wq_ref, k1_ref, k2_ref,
                    x1_ref, h2_ref, st_ref):
    yh = _dot(yh_ref[0].astype(BF16), wbh_ref[...])
    yg = _dot(yg_ref[0].astype(BF16), wbg_ref[...])
    y = _sigmoid(mh_ref[0]) * yh + _sigmoid(mg_ref[0]) * yg
    x1 = x_ref[0] + gate1_ref[0] * _dot(y.astype(BF16), wo_ref[...])
    x1_ref[0] = x1
    h2 = _rms(x1) * g2_ref[...] * (1.0 + scale2_ref[0]) + shift2_ref[0]
    h2b = h2.astype(BF16)
    h2_ref[0] = h2b
    q = _dot(h2b, wq_ref[...]).astype(BF16)
    for hh in range(2 * PEER_HEADS):
        keys = k1_ref if hh % 2 == 0 else k2_ref
        st_ref[hh] = lax.dot_general(keys[...], q[:, hh * PEER_SUB_DIM:(hh + 1) * PEER_SUB_DIM],
                                     (((1,), (1,)), ((), ())), preferred_element_type=F32)


def _mixer_out(y_hg, y_gla, m_hg, m_gla, x, gate1, shift2, scale2, g2, wbh, wbg, wo, wq, k1, k2):
    bsz, t, d = x.shape
    tm = ROW_TILE
    assert t % tm == 0
    row = pl.BlockSpec((1, tm, d), lambda b, i: (b, i, 0))
    per_batch = pl.BlockSpec((1, 1, d), lambda b, i: (b, 0, 0))
    tiles_per_batch = t // tm
    n = bsz * t
    return pl.pallas_call(
        _mixer_out_body,
        grid=(bsz, tiles_per_batch),
        in_specs=[row, row, row, row, row, per_batch, per_batch, per_batch, _resident((1, d)),
                  _resident(wbh.shape), _resident(wbg.shape), _resident(wo.shape), _resident(wq.shape),
                  _resident(k1.shape), _resident(k2.shape)],
        out_specs=[row, row,
                   pl.BlockSpec((2 * PEER_HEADS, PEER_N_KEYS, tm), lambda b, i: (0, 0, b * tiles_per_batch + i))],
        out_shape=[jax.ShapeDtypeStruct((bsz, t, d), F32), jax.ShapeDtypeStruct((bsz, t, d), BF16),
                   jax.ShapeDtypeStruct((2 * PEER_HEADS, PEER_N_KEYS, n), F32)],
        compiler_params=_params(2),
        name="mixer_out",
    )(y_hg, y_gla, m_hg, m_gla, x, gate1, shift2, scale2, g2, wbh, wbg, wo, wq, k1, k2)


_CAND_GROUPS = (("a", 0, 0), ("a", 0, 8), ("a", 1, 0), ("b", 0, 8),
                ("b", 0, 0), ("b", 1, 0), ("b", 2, 0), ("b", 3, 0), ("b", 4, 0))
_CAND_ROWS = len(_CAND_GROUPS) * SUBLANES


def _candidate_tables():
    flat = np.full((_CAND_ROWS, 1), -1, np.float32)
    seen = set()
    for g, (kind, fixed, start) in enumerate(_CAND_GROUPS):
        for r in range(SUBLANES):
            a, b = (fixed, start + r) if kind == "a" else (start + r, fixed)
            if (a + 1) * (b + 1) <= PEER_TOPK and (a, b) not in seen:
                seen.add((a, b))
                flat[g * SUBLANES + r, 0] = a * PEER_TOPK + b
    needed = {(a, b) for a in range(PEER_TOPK) for b in range(PEER_TOPK) if (a + 1) * (b + 1) <= PEER_TOPK}
    assert seen == needed
    return flat


def _extract_top(x, key, rounds):
    big = np.float32(2 ** 30)
    vals, keys = [], []
    for _ in range(rounds):
        m = jnp.max(x, axis=0, keepdims=True)
        sel = jnp.min(jnp.where(x == m, key, big), axis=0, keepdims=True)
        vals.append(m)
        keys.append(sel)
        x = jnp.where(key == sel, -jnp.inf, x)
    return vals, keys


def _extract_top_paired(x, rounds):
    half = x.shape[0] // 2
    big = np.float32(2 ** 30)
    ids = lax.broadcasted_iota(jnp.int32, (half, x.shape[1]), 0).astype(F32)
    a, b = x[:half], x[half:]
    first = a >= b
    win, lose = jnp.where(first, a, b), jnp.where(first, b, a)
    win_id, lose_id = jnp.where(first, ids, ids + half), jnp.where(first, ids + half, ids)
    vals, keys = [], []
    for _ in range(rounds):
        m = jnp.max(win, axis=0, keepdims=True)
        sel = jnp.min(jnp.where(win == m, win_id, big), axis=0, keepdims=True)
        vals.append(m)
        keys.append(sel)
        taken = win_id == sel
        win = jnp.where(taken, lose, win)
        win_id = jnp.where(taken, lose_id, win_id)
        lose = jnp.where(taken, -jnp.inf, lose)
    return vals, keys


def _pick_row(table, row_ids, sel):
    return jnp.sum(jnp.where(row_ids == sel, table, 0), axis=0, keepdims=True)


def _route_body(st_ref, flat_ref, key1_ref, key2_ref, gate_ref):
    tn = st_ref.shape[-1]
    rank_ids = lax.broadcasted_iota(jnp.int32, (PEER_TOPK, tn), 0).astype(F32)
    flat = jnp.broadcast_to(flat_ref[...], (_CAND_ROWS, tn))
    valid = flat >= 0.0
    keys1, keys2, gates = [], [], []
    for h in range(PEER_HEADS):
        v1, i1 = _extract_top_paired(st_ref[2 * h], PEER_TOPK)
        v2, i2 = _extract_top_paired(st_ref[2 * h + 1], PEER_TOPK)
        v1c, i1c = jnp.concatenate(v1, axis=0), jnp.concatenate(i1, axis=0)
        v2c, i2c = jnp.concatenate(v2, axis=0), jnp.concatenate(i2, axis=0)
        cand = jnp.concatenate(
            [v1[fixed] + v2c[start:start + SUBLANES] if kind == "a" else v1c[start:start + SUBLANES] + v2[fixed]
             for kind, fixed, start in _CAND_GROUPS], axis=0)
        cand = jnp.where(valid, cand, -jnp.inf)
        top_s, top_c = _extract_top(cand, flat, PEER_TOPK)
        m = top_s[0]
        ex = [jnp.exp(s - m) for s in top_s]
        denom = functools.reduce(lambda a, b: a + b, ex)
        for s_k, c_k in zip(ex, top_c):
            c_int = c_k.astype(jnp.int32)
            a = (c_int >> 4).astype(F32)
            b = (c_int & (PEER_TOPK - 1)).astype(F32)
            keys1.append(_pick_row(i1c, rank_ids, a))
            keys2.append(_pick_row(i2c, rank_ids, b))
            gates.append(s_k / denom)
    key1_ref[...] = jnp.concatenate(keys1, axis=0).T
    key2_ref[...] = jnp.concatenate(keys2, axis=0).T
    gate_ref[...] = jnp.concatenate(gates, axis=0).T


def _route(scores_t):
    n = scores_t.shape[-1]
    tn = ROUTE_TILE
    assert n % tn == 0
    flat = jnp.asarray(_candidate_tables())
    return pl.pallas_call(
        _route_body,
        grid=(n // tn,),
        in_specs=[pl.BlockSpec((2 * PEER_HEADS, PEER_N_KEYS, tn), lambda i: (0, 0, i)),
                  pl.BlockSpec(flat.shape, lambda i: (0, 0))],
        out_specs=[pl.BlockSpec((tn, N_SEL), lambda i: (i, 0))] * 3,
        out_shape=[jax.ShapeDtypeStruct((n, N_SEL), F32)] * 3,
        compiler_params=_params(1),
        name="peer_route",
    )(scores_t, flat)


def _peer_dense_body(h_ref, key1_ref, key2_ref, gate_ref, x1_ref, gate2_ref, fg_ref, ut_ref, v_ref, o_ref, g_scr,
                     *, tm, blocks_per_step):
    k = pl.program_id(1)
    slab = tm + SUBLANES

    @pl.when(k == 0)
    def _():
        o_ref[...] = jnp.zeros_like(o_ref)
        sub = lax.broadcasted_iota(jnp.int32, (PEER_N_KEYS, N_SEL), 0).astype(F32).astype(BF16)
        zero = jnp.zeros((PEER_N_KEYS, N_SEL), BF16)
        one = jnp.ones((PEER_N_KEYS, N_SEL), BF16)

        def tokens(i, carry):
            for j in range(PEER_GATE_TOKENS_PER_TRIP):
                t = i * PEER_GATE_TOKENS_PER_TRIP + j
                key1 = key1_ref[pl.ds(t, 1), :].astype(BF16)
                key2 = key2_ref[pl.ds(t, 1), :].astype(BF16)
                gate = gate_ref[pl.ds(t, 1), :]
                hi = gate.astype(BF16)
                lo = (gate - hi.astype(F32)).astype(BF16)
                on1 = sub == key1
                lhs = jnp.concatenate([jnp.where(on1, hi, zero), jnp.where(on1, lo, zero)], axis=1)
                on2 = jnp.where(sub == key2, one, zero)
                g_t = lax.dot_general(lhs, jnp.concatenate([on2, on2], axis=1), (((1,), (1,)), ((), ())),
                                      preferred_element_type=F32)
                g_scr[pl.ds(t, PEER_N_KEYS, stride=slab), :] = g_t
            return carry

        lax.fori_loop(0, tm // PEER_GATE_TOKENS_PER_TRIP, tokens, 0)

    act = _dot(h_ref[...], ut_ref[...])
    g_blk = jnp.concatenate(
        [g_scr[pl.ds(pl.multiple_of((k * blocks_per_step + j) * slab, SUBLANES), tm), :]
         for j in range(blocks_per_step)], axis=1)
    coef = (g_blk * _gelu(act)).astype(BF16)
    o_ref[...] += _dot(coef, v_ref[...])

    @pl.when(k == pl.num_programs(1) - 1)
    def _():
        x2 = x1_ref[...] + gate2_ref[0] * o_ref[...]
        o_ref[...] = _rms(x2) * fg_ref[...]


def _peer_dense(h2_bf16, key1, key2, gate, x1, gate2, final_g, u_t, v, tokens_per_batch):
    n, d = x1.shape
    n_experts = v.shape[0]
    tm, eb = PEER_ROW_TILE, PEER_EXPERT_TILE
    assert tokens_per_batch % tm == 0 and n_experts % eb == 0 and eb % PEER_N_KEYS == 0
    tiles_per_batch = tokens_per_batch // tm
    n_blocks = n_experts // eb
    row = lambda w: pl.BlockSpec((tm, w), lambda i, k: (i, 0))
    once_per_tile = lambda w: pl.BlockSpec((tm, w), lambda i, k: (i, 0), pipeline_mode=pl.Buffered(1))
    return pl.pallas_call(
        functools.partial(_peer_dense_body, tm=tm, blocks_per_step=eb // PEER_N_KEYS),
        grid=(n // tm, n_blocks),
        in_specs=[once_per_tile(d), row(N_SEL), row(N_SEL), row(N_SEL), once_per_tile(d),
                  pl.BlockSpec((1, 1, d), lambda i, k: (i // tiles_per_batch, 0, 0)),
                  pl.BlockSpec((1, d), lambda i, k: (0, 0)),
                  pl.BlockSpec((d, eb), lambda i, k: (0, k)),
                  pl.BlockSpec((eb, d), lambda i, k: (k, 0))],
        out_specs=row(d),
        out_shape=jax.ShapeDtypeStruct((n, d), F32),
        scratch_shapes=[pltpu.VMEM((PEER_N_KEYS * (tm + SUBLANES), PEER_N_KEYS), F32)],
        compiler_params=_params(2),
        name="peer_dense",
    )(h2_bf16, key1, key2, gate, x1, gate2, final_g, u_t, v)


def kernel(x, c, ctx, c_ctx, ada_w, ada_b, norm_mix_g, w_in, hgrn_lb_logits, hgrn_norm_g, gla_gk_w, gla_gk_b, gla_norm_g, w_branch_hgrn, w_branch_gla, w_out, norm_ffn_g, peer_wq, peer_k1, peer_k2, peer_u, peer_v, final_g):
    bsz, seq, d = x.shape
    t_ctx = ctx.shape[1]
    depth = ada_w.shape[0]
    assert depth == 1 and d == D_MODEL and w_in.shape[-1] == D_IN
    l = 0
    row2 = lambda a: a.reshape(1, -1)

    lower = jnp.cumsum(jax.nn.softmax(hgrn_lb_logits, axis=0), axis=0)[l]
    lower = jnp.concatenate([lower, jnp.zeros((SUBLANES - 2, HG_KEY), F32)], axis=0)
    cvec = jnp.concatenate([c, c_ctx[None], jnp.zeros((SUBLANES - 1, d), F32)], axis=0)
    mod_all = _adaln(cvec, ada_w[l], ada_b[l])
    mod = [m.reshape(bsz, 1, d) for m in jnp.split(mod_all[:bsz], N_MOD, axis=-1)]
    mod_c = [jnp.broadcast_to(m.reshape(1, 1, d), (bsz, 1, d)) for m in jnp.split(mod_all[bsz:bsz + 1], N_MOD, axis=-1)]
    w_in_b = w_in[l].astype(BF16)
    zr = jnp.zeros((GLA_GATE_RANK, GLA_KEY), F32)
    wrank = jnp.concatenate([jnp.concatenate([gla_gk_w[l, 0], zr], axis=1),
                             jnp.concatenate([zr, gla_gk_w[l, 1]], axis=1)], axis=0).astype(BF16)
    brank = gla_gk_b[l].reshape(1, 2 * GLA_KEY)

    lat = _inproj(x, mod[0], mod[1], row2(norm_mix_g[l]), w_in_b, wrank, brank)
    cx = _inproj(ctx, mod_c[0], mod_c[1], row2(norm_mix_g[l]), w_in_b, wrank, brank)
    (hq, zf, zb, hi, hgate, gq, gk, gv, gg, glf, glb, m_hg, m_gla) = lat
    (_, czf, czb, chi, _, _, cgk, cgv, _, cglf, cglb, _, _) = cx

    y_hg = _scan_call(
        _hgrn_scan_body, HG_HEADS, HG_HEAD_V,
        [(hq, HEAD_K), (zf, HEAD_K), (zb, HEAD_K), (hi, HG_HEAD_V), (hgate, HG_HEAD_V)],
        [(czf, HEAD_K), (czb, HEAD_K), (chi, HG_HEAD_V)],
        [(lower, (SUBLANES, HEAD_K), lambda b, h: (0, h)), (row2(hgrn_norm_g[l]), (1, HG_HEAD_V), lambda b, h: (0, 0))],
        bsz, seq, t_ctx, "hgrn_scan")
    y_gla = _scan_call(
        _gla_scan_body, GLA_HEADS, GLA_HEAD_V,
        [(gq, HEAD_K), (gk, HEAD_K), (glf, HEAD_K), (glb, HEAD_K), (gv, GLA_HEAD_V), (gg, GLA_HEAD_V)],
        [(cgk, HEAD_K), (cglf, HEAD_K), (cglb, HEAD_K), (cgv, GLA_HEAD_V)],
        [(row2(gla_norm_g[l]), (1, GLA_HEAD_V), lambda b, h: (0, 0))],
        bsz, seq, t_ctx, "gla_scan")

    x1, h2, scores_t = _mixer_out(
        y_hg, y_gla, m_hg, m_gla, x, mod[2], mod[3], mod[4], row2(norm_ffn_g[l]),
        w_branch_hgrn[l].astype(BF16), w_branch_gla[l].astype(BF16), w_out[l].astype(BF16),
        peer_wq[l].astype(BF16), peer_k1[l].astype(BF16), peer_k2[l].astype(BF16))

    key1, key2, gate = _route(scores_t)
    n = bsz * seq
    out = _peer_dense(h2.reshape(n, d), key1, key2, gate, x1.reshape(n, d), mod[5], row2(final_g),
                      peer_u[l].astype(BF16).T, peer_v[l].astype(BF16), seq)
    return out.reshape(bsz, seq, d)
```

```python
import functools
import math

import jax
import jax.numpy as jnp
import numpy as np
from jax import lax
from jax.experimental import pallas as pl
from jax.experimental.pallas import tpu as pltpu

F32 = jnp.float32
BF16 = jnp.bfloat16

D_MODEL = 1024
GRID_W = 64
CTX_CHUNK = 64
N_MOD = 6
EPS = 1e-6

HG_HEADS = 8
HEAD_K = 128
HG_KEY = HG_HEADS * HEAD_K
HG_VAL = D_MODEL
HG_HEAD_V = HG_VAL // HG_HEADS
GLA_HEADS = 4
GLA_KEY = D_MODEL // 2
GLA_VAL = D_MODEL
GLA_HEAD_V = GLA_VAL // GLA_HEADS
GLA_GATE_RANK = 16
GLA_GATE_NORMALIZER = 16.0

C_HQ, C_ZF, C_ZB, C_HI, C_HGATE = 0, 1024, 2048, 3072, 4096
C_GQ, C_GK, C_GV, C_GG = 5120, 5632, 6144, 7168
C_RANK = 8192
C_MHG = C_RANK + 2 * GLA_GATE_RANK
C_MGLA = C_MHG + D_MODEL
D_IN = C_MGLA + D_MODEL

PEER_HEADS = 8
PEER_N_KEYS = 128
PEER_SUB_DIM = 128
PEER_TOPK = 16
N_SEL = PEER_HEADS * PEER_TOPK

SUBLANES = 8
VMEM_LIMIT = 56 * 1024 * 1024

ROW_TILE = 256
INPROJ_ROW_TILE = 256
ROUTE_TILE = 256
PEER_ROW_TILE = 512
PEER_EXPERT_TILE = 1024
PEER_GATE_TOKENS_PER_TRIP = 64
SCAN_CHUNKS_PER_TRIP = 32


def _resident(shape):
    return pl.BlockSpec(shape, lambda *_: (0,) * len(shape), pipeline_mode=pl.Buffered(1))


def _params(n_axes):
    return pltpu.CompilerParams(dimension_semantics=("arbitrary",) * n_axes, vmem_limit_bytes=VMEM_LIMIT)


def _split3(a):
    p1 = a.astype(BF16)
    r1 = a - p1.astype(F32)
    p2 = r1.astype(BF16)
    p3 = (r1 - p2.astype(F32)).astype(BF16)
    return p1, p2, p3


def _dot(a, b):
    return jnp.dot(a, b, preferred_element_type=F32)


def _rms(x):
    return x * lax.rsqrt(jnp.mean(x * x, axis=-1, keepdims=True) + EPS)


def _gelu(x):
    return 0.5 * x * (1.0 + lax.erf(x * np.float32(2.0 ** -0.5)))


def _sigmoid(x):
    return 1.0 / (1.0 + jnp.exp(-x))


def _log_sigmoid(x):
    return jnp.minimum(x, 0.0) - jnp.log1p(jnp.exp(-jnp.abs(x)))


def _adaln_body(c_ref, w_ref, b_ref, o_ref):
    c = c_ref[...]
    s = (c * _sigmoid(c)).astype(BF16)
    o_ref[...] = _dot(s, w_ref[...].astype(BF16)) + b_ref[...]


def _adaln(cvec, w, b):
    rows, d = cvec.shape
    n_out = w.shape[1]
    return pl.pallas_call(
        _adaln_body,
        grid=(n_out // d,),
        in_specs=[pl.BlockSpec((rows, d), lambda j: (0, 0)),
                  pl.BlockSpec((d, d), lambda j: (0, j)),
                  pl.BlockSpec((1, d), lambda j: (0, j))],
        out_specs=pl.BlockSpec((rows, d), lambda j: (0, j)),
        out_shape=jax.ShapeDtypeStruct((rows, n_out), F32),
        compiler_params=_params(1),
        name="adaln",
    )(cvec, w, b.reshape(1, n_out))


def _inproj_body(x_ref, shift_ref, scale_ref, g_ref, w_ref, wrank_ref, brank_ref,
                 hq_ref, zf_ref, zb_ref, hi_ref, hgate_ref, gq_ref, gk_ref, gv_ref, gg_ref,
                 glf_ref, glb_ref, mhg_ref, mgla_ref):
    h = _rms(x_ref[0]) * g_ref[...] * (1.0 + scale_ref[0]) + shift_ref[0]
    hb = h.astype(BF16)

    def cols(start, width):
        return _dot(hb, w_ref[:, start:start + width])

    hq_ref[0] = cols(C_HQ, HG_KEY) * np.float32(HEAD_K ** -0.5)
    zf_ref[0] = cols(C_ZF, HG_KEY)
    zb_ref[0] = cols(C_ZB, HG_KEY)
    hi_ref[0] = cols(C_HI, HG_VAL)
    hgate_ref[0] = cols(C_HGATE, HG_VAL)
    gq_ref[0] = cols(C_GQ, GLA_KEY) * np.float32(HEAD_K ** -0.5)
    gk_ref[0] = cols(C_GK, GLA_KEY)
    gv_ref[0] = cols(C_GV, GLA_VAL)
    gg_ref[0] = cols(C_GG, GLA_VAL)
    mhg_ref[0] = cols(C_MHG, D_MODEL)
    mgla_ref[0] = cols(C_MGLA, D_MODEL)
    rank = cols(C_RANK, 2 * GLA_GATE_RANK).astype(BF16)
    pre = _dot(rank, wrank_ref[...]) + brank_ref[...]
    lg = _log_sigmoid(pre) / GLA_GATE_NORMALIZER
    glf_ref[0] = lg[:, :GLA_KEY]
    glb_ref[0] = lg[:, GLA_KEY:]


def _inproj(x, shift, scale, g, w_bf16, wrank, brank):
    bsz, t, d = x.shape
    tm = INPROJ_ROW_TILE
    assert t % tm == 0
    widths = (HG_KEY, HG_KEY, HG_KEY, HG_VAL, HG_VAL, GLA_KEY, GLA_KEY, GLA_VAL, GLA_VAL,
              GLA_KEY, GLA_KEY, D_MODEL, D_MODEL)
    row = lambda w: pl.BlockSpec((1, tm, w), lambda b, i: (b, i, 0))
    per_batch = pl.BlockSpec((1, 1, d), lambda b, i: (b, 0, 0))
    return pl.pallas_call(
        _inproj_body,
        grid=(bsz, t // tm),
        in_specs=[row(d), per_batch, per_batch, _resident((1, d)), _resident(w_bf16.shape),
                  _resident(wrank.shape), _resident(brank.shape)],
        out_specs=[row(w) for w in widths],
        out_shape=[jax.ShapeDtypeStruct((bsz, t, w), F32) for w in widths],
        compiler_params=_params(2),
        name="inproj",
    )(x, shift, scale, g, w_bf16, wrank, brank)


def _cumulative_decays(lgs, tri3):
    wide = jnp.concatenate(lgs, axis=1)
    b = _dot(tri3, jnp.concatenate(_split3(wide), axis=0))
    return [b[:, j * HEAD_K:(j + 1) * HEAD_K] for j in range(len(lgs))]


def _chunk_step(q, k, b, v, state, mask, ref_row, last_row, want_out):
    b_ref = b[ref_row:ref_row + 1]
    b_last = b[last_row:last_row + 1]
    vb = v.astype(BF16)
    o = None
    if want_out:
        qd = (q * jnp.exp(b - b_ref)).astype(BF16)
        kd = (k * jnp.exp(b_ref - b)).astype(BF16)
        scores = lax.dot_general(qd, kd, (((1,), (1,)), ((), ())), preferred_element_type=F32)
        scores = jnp.where(mask, scores, 0.0).astype(BF16)
        qs = (q * jnp.exp(b)).astype(BF16)
        o = _dot(scores, vb) + lax.dot_general(qs, state.astype(BF16), (((1,), (1,)), ((), ())),
                                               preferred_element_type=F32)
    kl = (k * jnp.exp(b_last - b)).astype(BF16)
    new_state = state * jnp.exp(b_last) + lax.dot_general(vb, kl, (((0,), (0,)), ((), ())),
                                                          preferred_element_type=F32)
    return o, new_state


def _scan_core(load_fwd, load_bwd, load_ctx_fwd, load_ctx_bwd, gate_ref, g_ref, y_ref,
               of_scr, ob_scr, sf_scr, sb_scr, n_chunks, n_ctx_chunks, dv):
    c = GRID_W
    ri = lax.broadcasted_iota(jnp.int32, (c, c), 0)
    ci = lax.broadcasted_iota(jnp.int32, (c, c), 1)
    lower = ci <= ri
    upper = ci >= ri
    tri_f = jnp.concatenate([lower.astype(BF16)] * 3, axis=1)
    tri_b = jnp.concatenate([upper.astype(BF16)] * 3, axis=1)
    fwd = dict(mask=lower, ref_row=c // 2, last_row=c - 1)
    bwd = dict(mask=upper, ref_row=c - 1 - c // 2, last_row=0)

    def run(load, rows, tri3, state, out_scr, kind):
        loaded = [load(r) for r in rows]
        bs = _cumulative_decays([lg for _, _, lg, _ in loaded], tri3)
        for r, (q, k, _, v), b in zip(rows, loaded, bs):
            o, state = _chunk_step(q, k, b, v, state, want_out=out_scr is not None, **kind)
            if out_scr is not None:
                out_scr[pl.ds(r, c), :] = o
        return state

    zero = jnp.zeros((dv, HEAD_K), F32)
    ctx_rows = [i * CTX_CHUNK for i in range(n_ctx_chunks)]
    sf_scr[...] = run(load_ctx_fwd, ctx_rows, tri_f, zero, None, fwd)
    sb_scr[...] = run(load_ctx_bwd, ctx_rows[::-1], tri_b, zero, None, bwd)

    per_trip = math.gcd(n_chunks, SCAN_CHUNKS_PER_TRIP)

    def step(i, carry):
        ns = [i * per_trip + j for j in range(per_trip)]
        rows_f = [pl.multiple_of(n * c, c) for n in ns]
        rows_b = [pl.multiple_of((n_chunks - 1 - n) * c, c) for n in ns]
        sf_scr[...] = run(load_fwd, rows_f, tri_f, sf_scr[...], of_scr, fwd)
        sb_scr[...] = run(load_bwd, rows_b, tri_b, sb_scr[...], ob_scr, bwd)
        return carry

    lax.fori_loop(0, n_chunks // per_trip, step, 0)

    o = of_scr[...] + ob_scr[...]
    gate = gate_ref[0]
    y_ref[0] = _rms(o) * g_ref[...] * (gate * _sigmoid(gate))


def _hgrn_scan_body(q_ref, zf_ref, zb_ref, v_ref, gate_ref, czf_ref, czb_ref, cv_ref, lb_ref, g_ref,
                    y_ref, of_scr, ob_scr, sf_scr, sb_scr, *, n_chunks, n_ctx_chunks):
    def forget(z, lower):
        f = lower + (1.0 - lower) * _sigmoid(z)
        return 1.0 - f, jnp.log(f)

    def loader(q_r, z_r, v_r, row):
        lower = lb_ref[row:row + 1, :]

        def load(rows):
            sl = pl.ds(rows, GRID_W)
            k, lg = forget(z_r[0, sl, :], lower)
            q = q_r[0, sl, :] if q_r is not None else None
            return q, k, lg, v_r[0, sl, :]
        return load

    _scan_core(loader(q_ref, zf_ref, v_ref, 0), loader(q_ref, zb_ref, v_ref, 1),
               loader(None, czf_ref, cv_ref, 0), loader(None, czb_ref, cv_ref, 1),
               gate_ref, g_ref, y_ref, of_scr, ob_scr, sf_scr, sb_scr, n_chunks, n_ctx_chunks, HG_HEAD_V)


def _gla_scan_body(q_ref, k_ref, lf_ref, lbk_ref, v_ref, gate_ref, ck_ref, clf_ref, clb_ref, cv_ref, g_ref,
                   y_ref, of_scr, ob_scr, sf_scr, sb_scr, *, n_chunks, n_ctx_chunks):
    def loader(q_r, k_r, l_r, v_r):
        def load(rows):
            sl = pl.ds(rows, GRID_W)
            q = q_r[0, sl, :] if q_r is not None else None
            return q, k_r[0, sl, :], l_r[0, sl, :], v_r[0, sl, :]
        return load

    _scan_core(loader(q_ref, k_ref, lf_ref, v_ref), loader(q_ref, k_ref, lbk_ref, v_ref),
               loader(None, ck_ref, clf_ref, cv_ref), loader(None, ck_ref, clb_ref, cv_ref),
               gate_ref, g_ref, y_ref, of_scr, ob_scr, sf_scr, sb_scr, n_chunks, n_ctx_chunks, GLA_HEAD_V)


def _scan_call(body, n_heads, dv, lat, ctx, small, bsz, t, t_ctx, name):
    head = lambda rows, w: pl.BlockSpec((1, rows, w), lambda b, h: (b, 0, h))
    in_specs = ([head(t, w) for _, w in lat] + [head(t_ctx, w) for _, w in ctx]
                + [pl.BlockSpec(bs, im) for _, bs, im in small])
    return pl.pallas_call(
        functools.partial(body, n_chunks=t // GRID_W, n_ctx_chunks=t_ctx // CTX_CHUNK),
        grid=(bsz, n_heads),
        in_specs=in_specs,
        out_specs=head(t, dv),
        out_shape=jax.ShapeDtypeStruct((bsz, t, n_heads * dv), F32),
        scratch_shapes=[pltpu.VMEM((t, dv), F32), pltpu.VMEM((t, dv), F32),
                        pltpu.VMEM((dv, HEAD_K), F32), pltpu.VMEM((dv, HEAD_K), F32)],
        compiler_params=_params(2),
        name=name,
    )(*[a for a, _ in lat], *[a for a, _ in ctx], *[a for a, _, _ in small])


def _mixer_out_body(yh_ref, yg_ref, mh_ref, mg_ref, x_ref, gate1_ref, shift2_ref, scale2_ref, g2_ref,
                    wbh_ref, wbg_ref, wo_ref, wq_ref, k1_ref, k2_ref,
                    x1_ref, h2_ref, st_ref):
    yh = _dot(yh_ref[0].astype(BF16), wbh_ref[...])
    yg = _dot(yg_ref[0].astype(BF16), wbg_ref[...])
    y = _sigmoid(mh_ref[0]) * yh + _sigmoid(mg_ref[0]) * yg
    x1 = x_ref[0] + gate1_ref[0] * _dot(y.astype(BF16), wo_ref[...])
    x1_ref[0] = x1
    h2 = _rms(x1) * g2_ref[...] * (1.0 + scale2_ref[0]) + shift2_ref[0]
    h2b = h2.astype(BF16)
    h2_ref[0] = h2b
    q = _dot(h2b, wq_ref[...]).astype(BF16)
    for hh in range(2 * PEER_HEADS):
        keys = k1_ref if hh % 2 == 0 else k2_ref
        st_ref[hh] = lax.dot_general(keys[...], q[:, hh * PEER_SUB_DIM:(hh + 1) * PEER_SUB_DIM],
                                     (((1,), (1,)), ((), ())), preferred_element_type=F32)


def _mixer_out(y_hg, y_gla, m_hg, m_gla, x, gate1, shift2, scale2, g2, wbh, wbg, wo, wq, k1, k2):
    bsz, t, d = x.shape
    tm = ROW_TILE
    assert t % tm == 0
    row = pl.BlockSpec((1, tm, d), lambda b, i: (b, i, 0))
    per_batch = pl.BlockSpec((1, 1, d), lambda b, i: (b, 0, 0))
    tiles_per_batch = t // tm
    n = bsz * t
    return pl.pallas_call(
        _mixer_out_body,
        grid=(bsz, tiles_per_batch),
        in_specs=[row, row, row, row, row, per_batch, per_batch, per_batch, _resident((1, d)),
                  _resident(wbh.shape), _resident(wbg.shape), _resident(wo.shape), _resident(wq.shape),
                  _resident(k1.shape), _resident(k2.shape)],
        out_specs=[row, row,
                   pl.BlockSpec((2 * PEER_HEADS, PEER_N_KEYS, tm), lambda b, i: (0, 0, b * tiles_per_batch + i))],
        out_shape=[jax.ShapeDtypeStruct((bsz, t, d), F32), jax.ShapeDtypeStruct((bsz, t, d), BF16),
                   jax.ShapeDtypeStruct((2 * PEER_HEADS, PEER_N_KEYS, n), F32)],
        compiler_params=_params(2),
        name="mixer_out",
    )(y_hg, y_gla, m_hg, m_gla, x, gate1, shift2, scale2, g2, wbh, wbg, wo, wq, k1, k2)


_CAND_GROUPS = (("a", 0, 0), ("a", 0, 8), ("a", 1, 0), ("b", 0, 8), ("b", 0, 0),
                ("b", 1, 0), ("b", 2, 0), ("b", 3, 0), ("b", 4, 0), ("pad", 0, 0))
_CAND_ROWS = len(_CAND_GROUPS) * SUBLANES


def _candidate_tables():
    flat = np.full((_CAND_ROWS, 1), -1, np.float32)
    seen = set()
    for g, (kind, fixed, start) in enumerate(_CAND_GROUPS):
        for r in range(SUBLANES if kind != "pad" else 0):
            a, b = (fixed, start + r) if kind == "a" else (start + r, fixed)
            if (a + 1) * (b + 1) <= PEER_TOPK and (a, b) not in seen:
                seen.add((a, b))
                flat[g * SUBLANES + r, 0] = a * PEER_TOPK + b
    needed = {(a, b) for a in range(PEER_TOPK) for b in range(PEER_TOPK) if (a + 1) * (b + 1) <= PEER_TOPK}
    assert seen == needed
    return flat


def _extract_top(x, key, rounds):
    half = x.shape[0] // 2
    big = np.float32(2 ** 30)
    a, b = x[:half], x[half:]
    key_a, key_b = key
    first = (a > b) | ((a == b) & (key_a < key_b))
    win, lose = jnp.where(first, a, b), jnp.where(first, b, a)
    win_id, lose_id = jnp.where(first, key_a, key_b), jnp.where(first, key_b, key_a)
    vals, keys = [], []
    for _ in range(rounds):
        m = jnp.max(win, axis=0, keepdims=True)
        sel = jnp.min(jnp.where(win == m, win_id, big), axis=0, keepdims=True)
        vals.append(m)
        keys.append(sel)
        taken = win_id == sel
        win = jnp.where(taken, lose, win)
        win_id = jnp.where(taken, lose_id, win_id)
        lose = jnp.where(taken, -jnp.inf, lose)
    return vals, keys


def _pick_row(table, row_ids, sel):
    return jnp.sum(jnp.where(row_ids == sel, table, 0), axis=0, keepdims=True)


def _route_body(st_ref, flat_ref, key1_ref, key2_ref, gate_ref):
    tn = st_ref.shape[-1]
    key_ids = lax.broadcasted_iota(jnp.int32, (PEER_N_KEYS // 2, tn), 0).astype(F32)
    rank_ids = lax.broadcasted_iota(jnp.int32, (PEER_TOPK, tn), 0).astype(F32)
    flat = jnp.broadcast_to(flat_ref[...], (_CAND_ROWS, tn))
    valid = flat >= 0.0
    key_ids = (key_ids, key_ids + (PEER_N_KEYS // 2))
    flat_halves = tuple(jnp.broadcast_to(flat_ref[r:r + _CAND_ROWS // 2, :], (_CAND_ROWS // 2, tn))
                        for r in (0, _CAND_ROWS // 2))
    keys1, keys2, gates = [], [], []
    for h in range(PEER_HEADS):
        v1, i1 = _extract_top(st_ref[2 * h], key_ids, PEER_TOPK)
        v2, i2 = _extract_top(st_ref[2 * h + 1], key_ids, PEER_TOPK)
        v1c, i1c = jnp.concatenate(v1, axis=0), jnp.concatenate(i1, axis=0)
        v2c, i2c = jnp.concatenate(v2, axis=0), jnp.concatenate(i2, axis=0)
        groups = {"a": lambda fixed, start: v1[fixed] + v2c[start:start + SUBLANES],
                  "b": lambda fixed, start: v1c[start:start + SUBLANES] + v2[fixed],
                  "pad": lambda fixed, start: v2c[:SUBLANES]}
        cand = jnp.concatenate([groups[kind](fixed, start) for kind, fixed, start in _CAND_GROUPS], axis=0)
        cand = jnp.where(valid, cand, -jnp.inf)
        top_s, top_c = _extract_top(cand, flat_halves, PEER_TOPK)
        m = top_s[0]
        ex = [jnp.exp(s - m) for s in top_s]
        denom = functools.reduce(lambda a, b: a + b, ex)
        for s_k, c_k in zip(ex, top_c):
            c_int = c_k.astype(jnp.int32)
            a = (c_int >> 4).astype(F32)
            b = (c_int & (PEER_TOPK - 1)).astype(F32)
            keys1.append(_pick_row(i1c, rank_ids, a))
            keys2.append(_pick_row(i2c, rank_ids, b))
            gates.append(s_k / denom)
    key1_ref[...] = jnp.concatenate(keys1, axis=0).T
    key2_ref[...] = jnp.concatenate(keys2, axis=0).T
    gate_ref[...] = jnp.concatenate(gates, axis=0).T


def _route(scores_t):
    n = scores_t.shape[-1]
    tn = ROUTE_TILE
    assert n % tn == 0
    flat = jnp.asarray(_candidate_tables())
    return pl.pallas_call(
        _route_body,
        grid=(n // tn,),
        in_specs=[pl.BlockSpec((2 * PEER_HEADS, PEER_N_KEYS, tn), lambda i: (0, 0, i)),
                  pl.BlockSpec(flat.shape, lambda i: (0, 0))],
        out_specs=[pl.BlockSpec((tn, N_SEL), lambda i: (i, 0))] * 3,
        out_shape=[jax.ShapeDtypeStruct((n, N_SEL), F32)] * 3,
        compiler_params=_params(1),
        name="peer_route",
    )(scores_t, flat)


def _peer_dense_body(h_ref, key1_ref, key2_ref, gate_ref, x1_ref, gate2_ref, fg_ref, ut_ref, v_ref, o_ref, g_scr,
                     *, tm, blocks_per_step):
    k = pl.program_id(1)
    slab = tm + SUBLANES

    @pl.when(k == 0)
    def _():
        o_ref[...] = jnp.zeros_like(o_ref)
        sub = lax.broadcasted_iota(jnp.int32, (PEER_N_KEYS, N_SEL), 0).astype(F32).astype(BF16)
        zero = jnp.zeros((PEER_N_KEYS, N_SEL), BF16)
        one = jnp.ones((PEER_N_KEYS, N_SEL), BF16)

        def tokens(i, carry):
            for j in range(PEER_GATE_TOKENS_PER_TRIP):
                t = i * PEER_GATE_TOKENS_PER_TRIP + j
                key1 = key1_ref[pl.ds(t, 1), :].astype(BF16)
                key2 = key2_ref[pl.ds(t, 1), :].astype(BF16)
                gate = gate_ref[pl.ds(t, 1), :]
                hi = gate.astype(BF16)
                lo = (gate - hi.astype(F32)).astype(BF16)
                on1 = sub == key1
                lhs = jnp.concatenate([jnp.where(on1, hi, zero), jnp.where(on1, lo, zero)], axis=1)
                on2 = jnp.where(sub == key2, one, zero)
                g_t = lax.dot_general(lhs, jnp.concatenate([on2, on2], axis=1), (((1,), (1,)), ((), ())),
                                      preferred_element_type=F32)
                g_scr[pl.ds(t, PEER_N_KEYS, stride=slab), :] = g_t
            return carry

        lax.fori_loop(0, tm // PEER_GATE_TOKENS_PER_TRIP, tokens, 0)

    act = _dot(h_ref[...], ut_ref[...])
    g_blk = jnp.concatenate(
        [g_scr[pl.ds(pl.multiple_of((k * blocks_per_step + j) * slab, SUBLANES), tm), :]
         for j in range(blocks_per_step)], axis=1)
    coef = (g_blk * _gelu(act)).astype(BF16)
    o_ref[...] += _dot(coef, v_ref[...])

    @pl.when(k == pl.num_programs(1) - 1)
    def _():
        x2 = x1_ref[...] + gate2_ref[0] * o_ref[...]
        o_ref[...] = _rms(x2) * fg_ref[...]


def _peer_dense(h2_bf16, key1, key2, gate, x1, gate2, final_g, u_t, v, tokens_per_batch):
    n, d = x1.shape
    n_experts = v.shape[0]
    tm, eb = PEER_ROW_TILE, PEER_EXPERT_TILE
    assert tokens_per_batch % tm == 0 and n_experts % eb == 0 and eb % PEER_N_KEYS == 0
    tiles_per_batch = tokens_per_batch // tm
    n_blocks = n_experts // eb
    row = lambda w: pl.BlockSpec((tm, w), lambda i, k: (i, 0))
    once_per_tile = lambda w: pl.BlockSpec((tm, w), lambda i, k: (i, 0), pipeline_mode=pl.Buffered(1))
    return pl.pallas_call(
        functools.partial(_peer_dense_body, tm=tm, blocks_per_step=eb // PEER_N_KEYS),
        grid=(n // tm, n_blocks),
        in_specs=[once_per_tile(d), row(N_SEL), row(N_SEL), row(N_SEL), once_per_tile(d),
                  pl.BlockSpec((1, 1, d), lambda i, k: (i // tiles_per_batch, 0, 0)),
                  pl.BlockSpec((1, d), lambda i, k: (0, 0)),
                  pl.BlockSpec((d, eb), lambda i, k: (0, k)),
                  pl.BlockSpec((eb, d), lambda i, k: (k, 0))],
        out_specs=row(d),
        out_shape=jax.ShapeDtypeStruct((n, d), F32),
        scratch_shapes=[pltpu.VMEM((PEER_N_KEYS * (tm + SUBLANES), PEER_N_KEYS), F32)],
        compiler_params=_params(2),
        name="peer_dense",
    )(h2_bf16, key1, key2, gate, x1, gate2, final_g, u_t, v)


def kernel(x, c, ctx, c_ctx, ada_w, ada_b, norm_mix_g, w_in, hgrn_lb_logits, hgrn_norm_g, gla_gk_w, gla_gk_b, gla_norm_g, w_branch_hgrn, w_branch_gla, w_out, norm_ffn_g, peer_wq, peer_k1, peer_k2, peer_u, peer_v, final_g):
    bsz, seq, d = x.shape
    t_ctx = ctx.shape[1]
    depth = ada_w.shape[0]
    assert depth == 1 and d == D_MODEL and w_in.shape[-1] == D_IN
    l = 0
    row2 = lambda a: a.reshape(1, -1)

    lower = jnp.cumsum(jax.nn.softmax(hgrn_lb_logits, axis=0), axis=0)[l]
    lower = jnp.concatenate([lower, jnp.zeros((SUBLANES - 2, HG_KEY), F32)], axis=0)
    cvec = jnp.concatenate([c, c_ctx[None], jnp.zeros((SUBLANES - 1, d), F32)], axis=0)
    mod_all = _adaln(cvec, ada_w[l], ada_b[l])
    mod = [m.reshape(bsz, 1, d) for m in jnp.split(mod_all[:bsz], N_MOD, axis=-1)]
    mod_c = [jnp.broadcast_to(m.reshape(1, 1, d), (bsz, 1, d)) for m in jnp.split(mod_all[bsz:bsz + 1], N_MOD, axis=-1)]
    w_in_b = w_in[l].astype(BF16)
    zr = jnp.zeros((GLA_GATE_RANK, GLA_KEY), F32)
    wrank = jnp.concatenate([jnp.concatenate([gla_gk_w[l, 0], zr], axis=1),
                             jnp.concatenate([zr, gla_gk_w[l, 1]], axis=1)], axis=0).astype(BF16)
    brank = gla_gk_b[l].reshape(1, 2 * GLA_KEY)

    lat = _inproj(x, mod[0], mod[1], row2(norm_mix_g[l]), w_in_b, wrank, brank)
    cx = _inproj(ctx, mod_c[0], mod_c[1], row2(norm_mix_g[l]), w_in_b, wrank, brank)
    (hq, zf, zb, hi, hgate, gq, gk, gv, gg, glf, glb, m_hg, m_gla) = lat
    (_, czf, czb, chi, _, _, cgk, cgv, _, cglf, cglb, _, _) = cx

    y_hg = _scan_call(
        _hgrn_scan_body, HG_HEADS, HG_HEAD_V,
        [(hq, HEAD_K), (zf, HEAD_K), (zb, HEAD_K), (hi, HG_HEAD_V), (hgate, HG_HEAD_V)],
        [(czf, HEAD_K), (czb, HEAD_K), (chi, HG_HEAD_V)],
        [(lower, (SUBLANES, HEAD_K), lambda b, h: (0, h)), (row2(hgrn_norm_g[l]), (1, HG_HEAD_V), lambda b, h: (0, 0))],
        bsz, seq, t_ctx, "hgrn_scan")
    y_gla = _scan_call(
        _gla_scan_body, GLA_HEADS, GLA_HEAD_V,
        [(gq, HEAD_K), (gk, HEAD_K), (glf, HEAD_K), (glb, HEAD_K), (gv, GLA_HEAD_V), (gg, GLA_HEAD_V)],
        [(cgk, HEAD_K), (cglf, HEAD_K), (cglb, HEAD_K), (cgv, GLA_HEAD_V)],
        [(row2(gla_norm_g[l]), (1, GLA_HEAD_V), lambda b, h: (0, 0))],
        bsz, seq, t_ctx, "gla_scan")

    x1, h2, scores_t = _mixer_out(
        y_hg, y_gla, m_hg, m_gla, x, mod[2], mod[3], mod[4], row2(norm_ffn_g[l]),
        w_branch_hgrn[l].astype(BF16), w_branch_gla[l].astype(BF16), w_out[l].astype(BF16),
        peer_wq[l].astype(BF16), peer_k1[l].astype(BF16), peer_k2[l].astype(BF16))

    key1, key2, gate = _route(scores_t)
    n = bsz * seq
    out = _peer_dense(h2.reshape(n, d), key1, key2, gate, x1.reshape(n, d), mod[5], row2(final_g),
                      peer_u[l].astype(BF16).T, peer_v[l].astype(BF16), seq)
    return out.reshape(bsz, seq, d)
```

```python
import functools
import math

import jax
import jax.numpy as jnp
import numpy as np
from jax import lax
from jax.experimental import pallas as pl
from jax.experimental.pallas import tpu as pltpu

F32 = jnp.float32
BF16 = jnp.bfloat16

D_MODEL = 1024
GRID_W = 64
CTX_CHUNK = 64
N_MOD = 6
EPS = 1e-6

HG_HEADS = 8
HEAD_K = 128
HG_KEY = HG_HEADS * HEAD_K
HG_VAL = D_MODEL
HG_HEAD_V = HG_VAL // HG_HEADS
GLA_HEADS = 4
GLA_KEY = D_MODEL // 2
GLA_VAL = D_MODEL
GLA_HEAD_V = GLA_VAL // GLA_HEADS
GLA_GATE_RANK = 16
GLA_GATE_NORMALIZER = 16.0

C_HQ, C_ZF, C_ZB, C_HI, C_HGATE = 0, 1024, 2048, 3072, 4096
C_GQ, C_GK, C_GV, C_GG = 5120, 5632, 6144, 7168
C_RANK = 8192
C_MHG = C_RANK + 2 * GLA_GATE_RANK
C_MGLA = C_MHG + D_MODEL
D_IN = C_MGLA + D_MODEL

PEER_HEADS = 8
PEER_N_KEYS = 128
PEER_SUB_DIM = 128
PEER_TOPK = 16
N_SEL = PEER_HEADS * PEER_TOPK

SUBLANES = 8
VMEM_LIMIT = 56 * 1024 * 1024

ROW_TILE = 256
INPROJ_ROW_TILE = 256
ROUTE_TILE = 256
PEER_ROW_TILE = 512
PEER_EXPERT_TILE = 1024
PEER_GATE_TOKENS_PER_TRIP = 64
SCAN_CHUNKS_PER_TRIP = 32
SCAN_CUMSUM_GROUP = 16


def _resident(shape):
    return pl.BlockSpec(shape, lambda *_: (0,) * len(shape), pipeline_mode=pl.Buffered(1))


def _params(n_axes):
    return pltpu.CompilerParams(dimension_semantics=("arbitrary",) * n_axes, vmem_limit_bytes=VMEM_LIMIT)


def _split3(a):
    p1 = a.astype(BF16)
    r1 = a - p1.astype(F32)
    p2 = r1.astype(BF16)
    p3 = (r1 - p2.astype(F32)).astype(BF16)
    return p1, p2, p3


def _dot(a, b):
    return jnp.dot(a, b, preferred_element_type=F32)


def _rms(x):
    return x * lax.rsqrt(jnp.mean(x * x, axis=-1, keepdims=True) + EPS)


def _gelu(x):
    return 0.5 * x * (1.0 + lax.erf(x * np.float32(2.0 ** -0.5)))


def _sigmoid(x):
    return 1.0 / (1.0 + jnp.exp(-x))


def _log_sigmoid(x):
    return jnp.minimum(x, 0.0) - jnp.log1p(jnp.exp(-jnp.abs(x)))


def _adaln_body(c_ref, w_ref, b_ref, o_ref):
    c = c_ref[...]
    s = (c * _sigmoid(c)).astype(BF16)
    o_ref[...] = _dot(s, w_ref[...].astype(BF16)) + b_ref[...]


def _adaln(cvec, w, b):
    rows, d = cvec.shape
    n_out = w.shape[1]
    return pl.pallas_call(
        _adaln_body,
        grid=(n_out // d,),
        in_specs=[pl.BlockSpec((rows, d), lambda j: (0, 0)),
                  pl.BlockSpec((d, d), lambda j: (0, j)),
                  pl.BlockSpec((1, d), lambda j: (0, j))],
        out_specs=pl.BlockSpec((rows, d), lambda j: (0, j)),
        out_shape=jax.ShapeDtypeStruct((rows, n_out), F32),
        compiler_params=_params(1),
        name="adaln",
    )(cvec, w, b.reshape(1, n_out))


def _inproj_body(x_ref, shift_ref, scale_ref, g_ref, w_ref, wrank_ref, brank_ref,
                 hq_ref, zf_ref, zb_ref, hi_ref, hgate_ref, gq_ref, gk_ref, gv_ref, gg_ref,
                 glf_ref, glb_ref, mhg_ref, mgla_ref):
    h = _rms(x_ref[0]) * g_ref[...] * (1.0 + scale_ref[0]) + shift_ref[0]
    hb = h.astype(BF16)

    def cols(start, width):
        return _dot(hb, w_ref[:, start:start + width])

    hq_ref[0] = cols(C_HQ, HG_KEY) * np.float32(HEAD_K ** -0.5)
    zf_ref[0] = cols(C_ZF, HG_KEY)
    zb_ref[0] = cols(C_ZB, HG_KEY)
    hi_ref[0] = cols(C_HI, HG_VAL)
    hgate_ref[0] = cols(C_HGATE, HG_VAL)
    gq_ref[0] = cols(C_GQ, GLA_KEY) * np.float32(HEAD_K ** -0.5)
    gk_ref[0] = cols(C_GK, GLA_KEY)
    gv_ref[0] = cols(C_GV, GLA_VAL)
    gg_ref[0] = cols(C_GG, GLA_VAL)
    mhg_ref[0] = cols(C_MHG, D_MODEL)
    mgla_ref[0] = cols(C_MGLA, D_MODEL)
    rank = cols(C_RANK, 2 * GLA_GATE_RANK).astype(BF16)
    pre = _dot(rank, wrank_ref[...]) + brank_ref[...]
    lg = _log_sigmoid(pre) / GLA_GATE_NORMALIZER
    glf_ref[0] = lg[:, :GLA_KEY]
    glb_ref[0] = lg[:, GLA_KEY:]


def _inproj(x, shift, scale, g, w_bf16, wrank, brank):
    bsz, t, d = x.shape
    tm = INPROJ_ROW_TILE
    assert t % tm == 0
    widths = (HG_KEY, HG_KEY, HG_KEY, HG_VAL, HG_VAL, GLA_KEY, GLA_KEY, GLA_VAL, GLA_VAL,
              GLA_KEY, GLA_KEY, D_MODEL, D_MODEL)
    row = lambda w: pl.BlockSpec((1, tm, w), lambda b, i: (b, i, 0))
    per_batch = pl.BlockSpec((1, 1, d), lambda b, i: (b, 0, 0))
    return pl.pallas_call(
        _inproj_body,
        grid=(bsz, t // tm),
        in_specs=[row(d), per_batch, per_batch, _resident((1, d)), _resident(w_bf16.shape),
                  _resident(wrank.shape), _resident(brank.shape)],
        out_specs=[row(w) for w in widths],
        out_shape=[jax.ShapeDtypeStruct((bsz, t, w), F32) for w in widths],
        compiler_params=_params(2),
        name="inproj",
    )(x, shift, scale, g, w_bf16, wrank, brank)


def _cumulative_decays(lgs, tri3):
    wide = jnp.concatenate(lgs, axis=1)
    b = _dot(tri3, jnp.concatenate(_split3(wide), axis=0))
    return [b[:, j * HEAD_K:(j + 1) * HEAD_K] for j in range(len(lgs))]


def _chunk_group(chunks, state, mask, ref_row, last_row, want_out):
    prepared = []
    for q, k, b, v in chunks:
        b_ref = b[ref_row:ref_row + 1]
        b_last = b[last_row:last_row + 1]
        vb = v.astype(BF16)
        kl = (k * jnp.exp(b_last - b)).astype(BF16)
        increment = lax.dot_general(vb, kl, (((0,), (0,)), ((), ())), preferred_element_type=F32)
        scores = qs = None
        if want_out:
            qd = (q * jnp.exp(b - b_ref)).astype(BF16)
            kd = (k * jnp.exp(b_ref - b)).astype(BF16)
            scores = lax.dot_general(qd, kd, (((1,), (1,)), ((), ())), preferred_element_type=F32)
            qs = (q * jnp.exp(b)).astype(BF16)
        prepared.append((vb, scores, qs, jnp.exp(b_last), increment))
    intra = [_dot(jnp.where(mask, scores, 0.0).astype(BF16), vb) if want_out else None
             for vb, scores, _, _, _ in prepared]
    outs = []
    for (_, _, qs, decay, increment), o_intra in zip(prepared, intra):
        if want_out:
            outs.append(o_intra + lax.dot_general(qs, state.astype(BF16), (((1,), (1,)), ((), ())),
                                                  preferred_element_type=F32))
        state = state * decay + increment
    return outs, state


def _scan_core(load_fwd, load_bwd, load_ctx_fwd, load_ctx_bwd, gate_ref, g_ref, y_ref,
               of_scr, ob_scr, sf_scr, sb_scr, n_chunks, n_ctx_chunks, dv):
    c = GRID_W
    ri = lax.broadcasted_iota(jnp.int32, (c, c), 0)
    ci = lax.broadcasted_iota(jnp.int32, (c, c), 1)
    lower = ci <= ri
    upper = ci >= ri
    tri_f = jnp.concatenate([lower.astype(BF16)] * 3, axis=1)
    tri_b = jnp.concatenate([upper.astype(BF16)] * 3, axis=1)
    fwd = dict(mask=lower, ref_row=c // 2, last_row=c - 1)
    bwd = dict(mask=upper, ref_row=c - 1 - c // 2, last_row=0)

    def run(load, rows, tri3, state, out_scr, kind):
        lg_of, qkv_of = load
        for first in range(0, len(rows), SCAN_CUMSUM_GROUP):
            group = rows[first:first + SCAN_CUMSUM_GROUP]
            bs = _cumulative_decays([lg_of(r) for r in group], tri3)
            chunks = []
            for r, b in zip(group, bs):
                q, k, v = qkv_of(r)
                chunks.append((q, k, b, v))
            outs, state = _chunk_group(chunks, state, want_out=out_scr is not None, **kind)
            for r, o in zip(group, outs):
                out_scr[pl.ds(r, c), :] = o
        return state

    zero = jnp.zeros((dv, HEAD_K), F32)
    ctx_rows = [i * CTX_CHUNK for i in range(n_ctx_chunks)]
    sf_scr[...] = run(load_ctx_fwd, ctx_rows, tri_f, zero, None, fwd)
    sb_scr[...] = run(load_ctx_bwd, ctx_rows[::-1], tri_b, zero, None, bwd)

    per_trip = math.gcd(n_chunks, SCAN_CHUNKS_PER_TRIP)

    def step(i, carry):
        ns = [i * per_trip + j for j in range(per_trip)]
        rows_f = [pl.multiple_of(n * c, c) for n in ns]
        rows_b = [pl.multiple_of((n_chunks - 1 - n) * c, c) for n in ns]
        sf_scr[...] = run(load_fwd, rows_f, tri_f, sf_scr[...], of_scr, fwd)
        sb_scr[...] = run(load_bwd, rows_b, tri_b, sb_scr[...], ob_scr, bwd)
        return carry

    lax.fori_loop(0, n_chunks // per_trip, step, 0)

    o = of_scr[...] + ob_scr[...]
    gate = gate_ref[0]
    y_ref[0] = _rms(o) * g_ref[...] * (gate * _sigmoid(gate))


def _chunk_rows(ref, rows):
    sl = pl.ds(rows, GRID_W)
    return ref[0, sl, :] if len(ref.shape) == 3 else ref[sl, :]


def _chunk_loader(q_r, k_r, l_r, v_r):
    lg_of = lambda rows: _chunk_rows(l_r, rows)
    qkv_of = lambda rows: (None if q_r is None else _chunk_rows(q_r, rows), _chunk_rows(k_r, rows),
                           _chunk_rows(v_r, rows))
    return lg_of, qkv_of


def _hgrn_scan_body(q_ref, zf_ref, zb_ref, v_ref, gate_ref, czf_ref, czb_ref, cv_ref, lb_ref, g_ref,
                    y_ref, of_scr, ob_scr, sf_scr, sb_scr, kf, lf, kb, lb, ckf, clf, ckb, clb,
                    *, n_chunks, n_ctx_chunks):
    for z_r, row, k_scr, l_scr in ((zf_ref, 0, kf, lf), (zb_ref, 1, kb, lb), (czf_ref, 0, ckf, clf),
                                   (czb_ref, 1, ckb, clb)):
        lower = lb_ref[row:row + 1, :]
        f = lower + (1.0 - lower) * _sigmoid(z_r[0])
        k_scr[...] = 1.0 - f
        l_scr[...] = jnp.log(f)

    _scan_core(_chunk_loader(q_ref, kf, lf, v_ref), _chunk_loader(q_ref, kb, lb, v_ref),
               _chunk_loader(None, ckf, clf, cv_ref), _chunk_loader(None, ckb, clb, cv_ref),
               gate_ref, g_ref, y_ref, of_scr, ob_scr, sf_scr, sb_scr, n_chunks, n_ctx_chunks, HG_HEAD_V)


def _gla_scan_body(q_ref, k_ref, lf_ref, lbk_ref, v_ref, gate_ref, ck_ref, clf_ref, clb_ref, cv_ref, g_ref,
                   y_ref, of_scr, ob_scr, sf_scr, sb_scr, *, n_chunks, n_ctx_chunks):
    _scan_core(_chunk_loader(q_ref, k_ref, lf_ref, v_ref), _chunk_loader(q_ref, k_ref, lbk_ref, v_ref),
               _chunk_loader(None, ck_ref, clf_ref, cv_ref), _chunk_loader(None, ck_ref, clb_ref, cv_ref),
               gate_ref, g_ref, y_ref, of_scr, ob_scr, sf_scr, sb_scr, n_chunks, n_ctx_chunks, GLA_HEAD_V)


def _scan_call(body, n_heads, dv, lat, ctx, small, bsz, t, t_ctx, name, extra_scratch=()):
    head = lambda rows, w: pl.BlockSpec((1, rows, w), lambda b, h: (b, 0, h))
    in_specs = ([head(t, w) for _, w in lat] + [head(t_ctx, w) for _, w in ctx]
                + [pl.BlockSpec(bs, im) for _, bs, im in small])
    return pl.pallas_call(
        functools.partial(body, n_chunks=t // GRID_W, n_ctx_chunks=t_ctx // CTX_CHUNK),
        grid=(bsz, n_heads),
        in_specs=in_specs,
        out_specs=head(t, dv),
        out_shape=jax.ShapeDtypeStruct((bsz, t, n_heads * dv), F32),
        scratch_shapes=[pltpu.VMEM((t, dv), F32), pltpu.VMEM((t, dv), F32),
                        pltpu.VMEM((dv, HEAD_K), F32), pltpu.VMEM((dv, HEAD_K), F32), *extra_scratch],
        compiler_params=_params(2),
        name=name,
    )(*[a for a, _ in lat], *[a for a, _ in ctx], *[a for a, _, _ in small])


def _mixer_out_body(yh_ref, yg_ref, mh_ref, mg_ref, x_ref, gate1_ref, shift2_ref, scale2_ref, g2_ref,
                    wbh_ref, wbg_ref, wo_ref, wq_ref, k1_ref, k2_ref,
                    x1_ref, h2_ref, st_ref):
    yh = _dot(yh_ref[0].astype(BF16), wbh_ref[...])
    yg = _dot(yg_ref[0].astype(BF16), wbg_ref[...])
    y = _sigmoid(mh_ref[0]) * yh + _sigmoid(mg_ref[0]) * yg
    x1 = x_ref[0] + gate1_ref[0] * _dot(y.astype(BF16), wo_ref[...])
    x1_ref[0] = x1
    h2 = _rms(x1) * g2_ref[...] * (1.0 + scale2_ref[0]) + shift2_ref[0]
    h2b = h2.astype(BF16)
    h2_ref[0] = h2b
    q = _dot(h2b, wq_ref[...]).astype(BF16)
    for hh in range(2 * PEER_HEADS):
        keys = k1_ref if hh % 2 == 0 else k2_ref
        st_ref[hh] = lax.dot_general(keys[...], q[:, hh * PEER_SUB_DIM:(hh + 1) * PEER_SUB_DIM],
                                     (((1,), (1,)), ((), ())), preferred_element_type=F32)


def _mixer_out(y_hg, y_gla, m_hg, m_gla, x, gate1, shift2, scale2, g2, wbh, wbg, wo, wq, k1, k2):
    bsz, t, d = x.shape
    tm = ROW_TILE
    assert t % tm == 0
    row = pl.BlockSpec((1, tm, d), lambda b, i: (b, i, 0))
    per_batch = pl.BlockSpec((1, 1, d), lambda b, i: (b, 0, 0))
    tiles_per_batch = t // tm
    n = bsz * t
    return pl.pallas_call(
        _mixer_out_body,
        grid=(bsz, tiles_per_batch),
        in_specs=[row, row, row, row, row, per_batch, per_batch, per_batch, _resident((1, d)),
                  _resident(wbh.shape), _resident(wbg.shape), _resident(wo.shape), _resident(wq.shape),
                  _resident(k1.shape), _resident(k2.shape)],
        out_specs=[row, row,
                   pl.BlockSpec((2 * PEER_HEADS, PEER_N_KEYS, tm), lambda b, i: (0, 0, b * tiles_per_batch + i))],
        out_shape=[jax.ShapeDtypeStruct((bsz, t, d), F32), jax.ShapeDtypeStruct((bsz, t, d), BF16),
                   jax.ShapeDtypeStruct((2 * PEER_HEADS, PEER_N_KEYS, n), F32)],
        compiler_params=_params(2),
        name="mixer_out",
    )(y_hg, y_gla, m_hg, m_gla, x, gate1, shift2, scale2, g2, wbh, wbg, wo, wq, k1, k2)


_CAND_GROUPS = (("a", 0, 0), ("a", 0, 8), ("a", 1, 0), ("b", 0, 8), ("b", 0, 0),
                ("b", 1, 0), ("b", 2, 0), ("b", 3, 0), ("b", 4, 0), ("pad", 0, 0))
_CAND_ROWS = len(_CAND_GROUPS) * SUBLANES


def _candidate_tables():
    flat = np.full((_CAND_ROWS, 1), -1, np.float32)
    seen = set()
    for g, (kind, fixed, start) in enumerate(_CAND_GROUPS):
        for r in range(SUBLANES if kind != "pad" else 0):
            a, b = (fixed, start + r) if kind == "a" else (start + r, fixed)
            if (a + 1) * (b + 1) <= PEER_TOPK and (a, b) not in seen:
                seen.add((a, b))
                flat[g * SUBLANES + r, 0] = a * PEER_TOPK + b
    needed = {(a, b) for a in range(PEER_TOPK) for b in range(PEER_TOPK) if (a + 1) * (b + 1) <= PEER_TOPK}
    assert seen == needed
    return flat


def _extract_top(x, key, rounds):
    half = x.shape[0] // 2
    big = np.float32(2 ** 30)
    a, b = x[:half], x[half:]
    key_a, key_b = key
    first = (a > b) | ((a == b) & (key_a < key_b))
    win, lose = jnp.where(first, a, b), jnp.where(first, b, a)
    win_id, lose_id = jnp.where(first, key_a, key_b), jnp.where(first, key_b, key_a)
    vals, keys = [], []
    for _ in range(rounds):
        m = jnp.max(win, axis=0, keepdims=True)
        sel = jnp.min(jnp.where(win == m, win_id, big), axis=0, keepdims=True)
        vals.append(m)
        keys.append(sel)
        taken = win_id == sel
        win = jnp.where(taken, lose, win)
        win_id = jnp.where(taken, lose_id, win_id)
        lose = jnp.where(taken, -jnp.inf, lose)
    return vals, keys


def _pick_row(table, row_ids, sel):
    return jnp.sum(jnp.where(row_ids == sel, table, 0), axis=0, keepdims=True)


def _route_body(st_ref, flat_ref, key1_ref, key2_ref, gate_ref):
    tn = st_ref.shape[-1]
    key_ids = lax.broadcasted_iota(jnp.int32, (PEER_N_KEYS // 2, tn), 0).astype(F32)
    rank_ids = lax.broadcasted_iota(jnp.int32, (PEER_TOPK, tn), 0).astype(F32)
    flat = jnp.broadcast_to(flat_ref[...], (_CAND_ROWS, tn))
    valid = flat >= 0.0
    key_ids = (key_ids, key_ids + (PEER_N_KEYS // 2))
    flat_halves = tuple(jnp.broadcast_to(flat_ref[r:r + _CAND_ROWS // 2, :], (_CAND_ROWS // 2, tn))
                        for r in (0, _CAND_ROWS // 2))
    keys1, keys2, gates = [], [], []
    for h in range(PEER_HEADS):
        v1, i1 = _extract_top(st_ref[2 * h], key_ids, PEER_TOPK)
        v2, i2 = _extract_top(st_ref[2 * h + 1], key_ids, PEER_TOPK)
        v1c, i1c = jnp.concatenate(v1, axis=0), jnp.concatenate(i1, axis=0)
        v2c, i2c = jnp.concatenate(v2, axis=0), jnp.concatenate(i2, axis=0)
        groups = {"a": lambda fixed, start: v1[fixed] + v2c[start:start + SUBLANES],
                  "b": lambda fixed, start: v1c[start:start + SUBLANES] + v2[fixed],
                  "pad": lambda fixed, start: v2c[:SUBLANES]}
        cand = jnp.concatenate([groups[kind](fixed, start) for kind, fixed, start in _CAND_GROUPS], axis=0)
        cand = jnp.where(valid, cand, -jnp.inf)
        top_s, top_c = _extract_top(cand, flat_halves, PEER_TOPK)
        m = top_s[0]
        ex = [jnp.exp(s - m) for s in top_s]
        denom = functools.reduce(lambda a, b: a + b, ex)
        for s_k, c_k in zip(ex, top_c):
            c_int = c_k.astype(jnp.int32)
            a = (c_int >> 4).astype(F32)
            b = (c_int & (PEER_TOPK - 1)).astype(F32)
            keys1.append(_pick_row(i1c, rank_ids, a))
            keys2.append(_pick_row(i2c, rank_ids, b))
            gates.append(s_k / denom)
    key1_ref[...] = jnp.concatenate(keys1, axis=0).T
    key2_ref[...] = jnp.concatenate(keys2, axis=0).T
    gate_ref[...] = jnp.concatenate(gates, axis=0).T


def _route(scores_t):
    n = scores_t.shape[-1]
    tn = ROUTE_TILE
    assert n % tn == 0
    flat = jnp.asarray(_candidate_tables())
    return pl.pallas_call(
        _route_body,
        grid=(n // tn,),
        in_specs=[pl.BlockSpec((2 * PEER_HEADS, PEER_N_KEYS, tn), lambda i: (0, 0, i)),
                  pl.BlockSpec(flat.shape, lambda i: (0, 0))],
        out_specs=[pl.BlockSpec((tn, N_SEL), lambda i: (i, 0))] * 3,
        out_shape=[jax.ShapeDtypeStruct((n, N_SEL), F32)] * 3,
        compiler_params=_params(1),
        name="peer_route",
    )(scores_t, flat)


def _peer_dense_body(h_ref, key1_ref, key2_ref, gate_ref, x1_ref, gate2_ref, fg_ref, ut_ref, v_ref, o_ref, g_scr,
                     *, tm, blocks_per_step):
    k = pl.program_id(1)
    slab = tm + SUBLANES

    @pl.when(k == 0)
    def _():
        o_ref[...] = jnp.zeros_like(o_ref)
        sub = lax.broadcasted_iota(jnp.int32, (PEER_N_KEYS, N_SEL), 0).astype(F32).astype(BF16)
        zero = jnp.zeros((PEER_N_KEYS, N_SEL), BF16)
        one = jnp.ones((PEER_N_KEYS, N_SEL), BF16)

        def tokens(i, carry):
            for j in range(PEER_GATE_TOKENS_PER_TRIP):
                t = i * PEER_GATE_TOKENS_PER_TRIP + j
                key1 = key1_ref[pl.ds(t, 1), :].astype(BF16)
                key2 = key2_ref[pl.ds(t, 1), :].astype(BF16)
                gate = gate_ref[pl.ds(t, 1), :]
                hi = gate.astype(BF16)
                lo = (gate - hi.astype(F32)).astype(BF16)
                on1 = sub == key1
                lhs = jnp.concatenate([jnp.where(on1, hi, zero), jnp.where(on1, lo, zero)], axis=1)
                on2 = jnp.where(sub == key2, one, zero)
                g_t = lax.dot_general(lhs, jnp.concatenate([on2, on2], axis=1), (((1,), (1,)), ((), ())),
                                      preferred_element_type=F32)
                g_scr[pl.ds(t, PEER_N_KEYS, stride=slab), :] = g_t
            return carry

        lax.fori_loop(0, tm // PEER_GATE_TOKENS_PER_TRIP, tokens, 0)

    act = _dot(h_ref[...], ut_ref[...])
    g_blk = jnp.concatenate(
        [g_scr[pl.ds(pl.multiple_of((k * blocks_per_step + j) * slab, SUBLANES), tm), :]
         for j in range(blocks_per_step)], axis=1)
    coef = (g_blk * _gelu(act)).astype(BF16)
    o_ref[...] += _dot(coef, v_ref[...])

    @pl.when(k == pl.num_programs(1) - 1)
    def _():
        x2 = x1_ref[...] + gate2_ref[0] * o_ref[...]
        o_ref[...] = _rms(x2) * fg_ref[...]


def _peer_dense(h2_bf16, key1, key2, gate, x1, gate2, final_g, u_t, v, tokens_per_batch):
    n, d = x1.shape
    n_experts = v.shape[0]
    tm, eb = PEER_ROW_TILE, PEER_EXPERT_TILE
    assert tokens_per_batch % tm == 0 and n_experts % eb == 0 and eb % PEER_N_KEYS == 0
    tiles_per_batch = tokens_per_batch // tm
    n_blocks = n_experts // eb
    row = lambda w: pl.BlockSpec((tm, w), lambda i, k: (i, 0))
    once_per_tile = lambda w: pl.BlockSpec((tm, w), lambda i, k: (i, 0), pipeline_mode=pl.Buffered(1))
    return pl.pallas_call(
        functools.partial(_peer_dense_body, tm=tm, blocks_per_step=eb // PEER_N_KEYS),
        grid=(n // tm, n_blocks),
        in_specs=[once_per_tile(d), row(N_SEL), row(N_SEL), row(N_SEL), once_per_tile(d),
                  pl.BlockSpec((1, 1, d), lambda i, k: (i // tiles_per_batch, 0, 0)),
                  pl.BlockSpec((1, d), lambda i, k: (0, 0)),
                  pl.BlockSpec((d, eb), lambda i, k: (0, k)),
                  pl.BlockSpec((eb, d), lambda i, k: (k, 0))],
        out_specs=row(d),
        out_shape=jax.ShapeDtypeStruct((n, d), F32),
        scratch_shapes=[pltpu.VMEM((PEER_N_KEYS * (tm + SUBLANES), PEER_N_KEYS), F32)],
        compiler_params=_params(2),
        name="peer_dense",
    )(h2_bf16, key1, key2, gate, x1, gate2, final_g, u_t, v)


def kernel(x, c, ctx, c_ctx, ada_w, ada_b, norm_mix_g, w_in, hgrn_lb_logits, hgrn_norm_g, gla_gk_w, gla_gk_b, gla_norm_g, w_branch_hgrn, w_branch_gla, w_out, norm_ffn_g, peer_wq, peer_k1, peer_k2, peer_u, peer_v, final_g):
    bsz, seq, d = x.shape
    t_ctx = ctx.shape[1]
    depth = ada_w.shape[0]
    assert depth == 1 and d == D_MODEL and w_in.shape[-1] == D_IN
    l = 0
    row2 = lambda a: a.reshape(1, -1)

    lower = jnp.cumsum(jax.nn.softmax(hgrn_lb_logits, axis=0), axis=0)[l]
    lower = jnp.concatenate([lower, jnp.zeros((SUBLANES - 2, HG_KEY), F32)], axis=0)
    cvec = jnp.concatenate([c, c_ctx[None], jnp.zeros((SUBLANES - 1, d), F32)], axis=0)
    mod_all = _adaln(cvec, ada_w[l], ada_b[l])
    mod = [m.reshape(bsz, 1, d) for m in jnp.split(mod_all[:bsz], N_MOD, axis=-1)]
    mod_c = [jnp.broadcast_to(m.reshape(1, 1, d), (bsz, 1, d)) for m in jnp.split(mod_all[bsz:bsz + 1], N_MOD, axis=-1)]
    w_in_b = w_in[l].astype(BF16)
    zr = jnp.zeros((GLA_GATE_RANK, GLA_KEY), F32)
    wrank = jnp.concatenate([jnp.concatenate([gla_gk_w[l, 0], zr], axis=1),
                             jnp.concatenate([zr, gla_gk_w[l, 1]], axis=1)], axis=0).astype(BF16)
    brank = gla_gk_b[l].reshape(1, 2 * GLA_KEY)

    lat = _inproj(x, mod[0], mod[1], row2(norm_mix_g[l]), w_in_b, wrank, brank)
    cx = _inproj(ctx, mod_c[0], mod_c[1], row2(norm_mix_g[l]), w_in_b, wrank, brank)
    (hq, zf, zb, hi, hgate, gq, gk, gv, gg, glf, glb, m_hg, m_gla) = lat
    (_, czf, czb, chi, _, _, cgk, cgv, _, cglf, cglb, _, _) = cx

    y_hg = _scan_call(
        _hgrn_scan_body, HG_HEADS, HG_HEAD_V,
        [(hq, HEAD_K), (zf, HEAD_K), (zb, HEAD_K), (hi, HG_HEAD_V), (hgate, HG_HEAD_V)],
        [(czf, HEAD_K), (czb, HEAD_K), (chi, HG_HEAD_V)],
        [(lower, (SUBLANES, HEAD_K), lambda b, h: (0, h)), (row2(hgrn_norm_g[l]), (1, HG_HEAD_V), lambda b, h: (0, 0))],
        bsz, seq, t_ctx, "hgrn_scan",
        extra_scratch=[pltpu.VMEM((seq, HEAD_K), F32)] * 4 + [pltpu.VMEM((t_ctx, HEAD_K), F32)] * 4)
    y_gla = _scan_call(
        _gla_scan_body, GLA_HEADS, GLA_HEAD_V,
        [(gq, HEAD_K), (gk, HEAD_K), (glf, HEAD_K), (glb, HEAD_K), (gv, GLA_HEAD_V), (gg, GLA_HEAD_V)],
        [(cgk, HEAD_K), (cglf, HEAD_K), (cglb, HEAD_K), (cgv, GLA_HEAD_V)],
        [(row2(gla_norm_g[l]), (1, GLA_HEAD_V), lambda b, h: (0, 0))],
        bsz, seq, t_ctx, "gla_scan")

    x1, h2, scores_t = _mixer_out(
        y_hg, y_gla, m_hg, m_gla, x, mod[2], mod[3], mod[4], row2(norm_ffn_g[l]),
        w_branch_hgrn[l].astype(BF16), w_branch_gla[l].astype(BF16), w_out[l].astype(BF16),
        peer_wq[l].astype(BF16), peer_k1[l].astype(BF16), peer_k2[l].astype(BF16))

    key1, key2, gate = _route(scores_t)
    n = bsz * seq
    out = _peer_dense(h2.reshape(n, d), key1, key2, gate, x1.reshape(n, d), mod[5], row2(final_g),
                      peer_u[l].astype(BF16).T, peer_v[l].astype(BF16), seq)
    return out.reshape(bsz, seq, d)
```

```python
import functools
import math

import jax
import jax.numpy as jnp
import numpy as np
from jax import lax
from jax.experimental import pallas as pl
from jax.experimental.pallas import tpu as pltpu

F32 = jnp.float32
BF16 = jnp.bfloat16

D_MODEL = 1024
GRID_W = 64
CTX_CHUNK = 64
N_MOD = 6
EPS = 1e-6

HG_HEADS = 8
HEAD_K = 128
HG_KEY = HG_HEADS * HEAD_K
HG_VAL = D_MODEL
HG_HEAD_V = HG_VAL // HG_HEADS
GLA_HEADS = 4
GLA_KEY = D_MODEL // 2
GLA_VAL = D_MODEL
GLA_HEAD_V = GLA_VAL // GLA_HEADS
GLA_GATE_RANK = 16
GLA_GATE_NORMALIZER = 16.0

C_HQ, C_ZF, C_ZB, C_HI, C_HGATE = 0, 1024, 2048, 3072, 4096
C_GQ, C_GK, C_GV, C_GG = 5120, 5632, 6144, 7168
C_RANK = 8192
C_MHG = C_RANK + 2 * GLA_GATE_RANK
C_MGLA = C_MHG + D_MODEL
D_IN = C_MGLA + D_MODEL

PEER_HEADS = 8
PEER_N_KEYS = 128
PEER_SUB_DIM = 128
PEER_TOPK = 16
N_SEL = PEER_HEADS * PEER_TOPK

SUBLANES = 8
VMEM_LIMIT = 56 * 1024 * 1024

ROW_TILE = 256
INPROJ_ROW_TILE = 256
ROUTE_TILE = 256
PEER_ROW_TILE = 512
PEER_EXPERT_TILE = 1024
PEER_GATE_TOKENS_PER_TRIP = 64
SCAN_CHUNKS_PER_TRIP = 32
SCAN_CUMSUM_GROUP = 16


def _resident(shape):
    return pl.BlockSpec(shape, lambda *_: (0,) * len(shape), pipeline_mode=pl.Buffered(1))


def _params(n_axes):
    return pltpu.CompilerParams(dimension_semantics=("arbitrary",) * n_axes, vmem_limit_bytes=VMEM_LIMIT)


def _split3(a):
    p1 = a.astype(BF16)
    r1 = a - p1.astype(F32)
    p2 = r1.astype(BF16)
    p3 = (r1 - p2.astype(F32)).astype(BF16)
    return p1, p2, p3


def _dot(a, b):
    return jnp.dot(a, b, preferred_element_type=F32)


def _rms(x):
    return x * lax.rsqrt(jnp.mean(x * x, axis=-1, keepdims=True) + EPS)


def _gelu(x):
    return 0.5 * x * (1.0 + lax.erf(x * np.float32(2.0 ** -0.5)))


def _sigmoid(x):
    return 1.0 / (1.0 + jnp.exp(-x))


def _log_sigmoid(x):
    return jnp.minimum(x, 0.0) - jnp.log1p(jnp.exp(-jnp.abs(x)))


def _adaln_body(c_ref, w_ref, b_ref, o_ref):
    c = c_ref[...]
    s = (c * _sigmoid(c)).astype(BF16)
    o_ref[...] = _dot(s, w_ref[...].astype(BF16)) + b_ref[...]


def _adaln(cvec, w, b):
    rows, d = cvec.shape
    n_out = w.shape[1]
    return pl.pallas_call(
        _adaln_body,
        grid=(n_out // d,),
        in_specs=[pl.BlockSpec((rows, d), lambda j: (0, 0)),
                  pl.BlockSpec((d, d), lambda j: (0, j)),
                  pl.BlockSpec((1, d), lambda j: (0, j))],
        out_specs=pl.BlockSpec((rows, d), lambda j: (0, j)),
        out_shape=jax.ShapeDtypeStruct((rows, n_out), F32),
        compiler_params=_params(1),
        name="adaln",
    )(cvec, w, b.reshape(1, n_out))


def _inproj_body(x_ref, shift_ref, scale_ref, g_ref, w_ref, wrank_ref, brank_ref,
                 hq_ref, zf_ref, zb_ref, hi_ref, hgate_ref, gq_ref, gk_ref, gv_ref, gg_ref,
                 glf_ref, glb_ref, mhg_ref, mgla_ref):
    h = _rms(x_ref[0]) * g_ref[...] * (1.0 + scale_ref[0]) + shift_ref[0]
    hb = h.astype(BF16)

    def cols(start, width):
        return _dot(hb, w_ref[:, start:start + width])

    hq_ref[0] = cols(C_HQ, HG_KEY) * np.float32(HEAD_K ** -0.5)
    zf_ref[0] = cols(C_ZF, HG_KEY)
    zb_ref[0] = cols(C_ZB, HG_KEY)
    hi_ref[0] = cols(C_HI, HG_VAL)
    hgate_ref[0] = cols(C_HGATE, HG_VAL)
    gq_ref[0] = cols(C_GQ, GLA_KEY) * np.float32(HEAD_K ** -0.5)
    gk_ref[0] = cols(C_GK, GLA_KEY)
    gv_ref[0] = cols(C_GV, GLA_VAL)
    gg_ref[0] = cols(C_GG, GLA_VAL)
    mhg_ref[0] = cols(C_MHG, D_MODEL)
    mgla_ref[0] = cols(C_MGLA, D_MODEL)
    rank = cols(C_RANK, 2 * GLA_GATE_RANK).astype(BF16)
    pre = _dot(rank, wrank_ref[...]) + brank_ref[...]
    lg = _log_sigmoid(pre) / GLA_GATE_NORMALIZER
    glf_ref[0] = lg[:, :GLA_KEY]
    glb_ref[0] = lg[:, GLA_KEY:]


def _inproj(x, shift, scale, g, w_bf16, wrank, brank):
    bsz, t, d = x.shape
    tm = INPROJ_ROW_TILE
    assert t % tm == 0
    widths = (HG_KEY, HG_KEY, HG_KEY, HG_VAL, HG_VAL, GLA_KEY, GLA_KEY, GLA_VAL, GLA_VAL,
              GLA_KEY, GLA_KEY, D_MODEL, D_MODEL)
    row = lambda w: pl.BlockSpec((1, tm, w), lambda b, i: (b, i, 0))
    per_batch = pl.BlockSpec((1, 1, d), lambda b, i: (b, 0, 0))
    return pl.pallas_call(
        _inproj_body,
        grid=(bsz, t // tm),
        in_specs=[row(d), per_batch, per_batch, _resident((1, d)), _resident(w_bf16.shape),
                  _resident(wrank.shape), _resident(brank.shape)],
        out_specs=[row(w) for w in widths],
        out_shape=[jax.ShapeDtypeStruct((bsz, t, w), F32) for w in widths],
        compiler_params=_params(2),
        name="inproj",
    )(x, shift, scale, g, w_bf16, wrank, brank)


def _cumulative_decays(lgs, tri3):
    wide = jnp.concatenate(lgs, axis=1)
    b = _dot(tri3, jnp.concatenate(_split3(wide), axis=0))
    return [b[:, j * HEAD_K:(j + 1) * HEAD_K] for j in range(len(lgs))]


def _chunk_group(chunks, state, mask, ref_row, last_row, want_out):
    prepared = []
    for q, k, b, v in chunks:
        b_ref = b[ref_row:ref_row + 1]
        b_last = b[last_row:last_row + 1]
        vb = v.astype(BF16)
        kl = (k * jnp.exp(b_last - b)).astype(BF16)
        increment = lax.dot_general(vb, kl, (((0,), (0,)), ((), ())), preferred_element_type=F32)
        scores = qs = None
        if want_out:
            qd = (q * jnp.exp(b - b_ref)).astype(BF16)
            kd = (k * jnp.exp(b_ref - b)).astype(BF16)
            scores = lax.dot_general(qd, kd, (((1,), (1,)), ((), ())), preferred_element_type=F32)
            qs = (q * jnp.exp(b)).astype(BF16)
        prepared.append((vb, scores, qs, jnp.exp(b_last), increment))
    intra = [_dot(jnp.where(mask, scores, 0.0).astype(BF16), vb) if want_out else None
             for vb, scores, _, _, _ in prepared]
    outs = []
    for (_, _, qs, decay, increment), o_intra in zip(prepared, intra):
        if want_out:
            outs.append(o_intra + lax.dot_general(qs, state.astype(BF16), (((1,), (1,)), ((), ())),
                                                  preferred_element_type=F32))
        state = state * decay + increment
    return outs, state


def _scan_core(load_fwd, load_bwd, load_ctx_fwd, load_ctx_bwd, gate_ref, g_ref, y_ref,
               of_scr, ob_scr, sf_scr, sb_scr, n_chunks, n_ctx_chunks, dv):
    c = GRID_W
    ri = lax.broadcasted_iota(jnp.int32, (c, c), 0)
    ci = lax.broadcasted_iota(jnp.int32, (c, c), 1)
    lower = ci <= ri
    upper = ci >= ri
    tri_f = jnp.concatenate([lower.astype(BF16)] * 3, axis=1)
    tri_b = jnp.concatenate([upper.astype(BF16)] * 3, axis=1)
    fwd = dict(mask=lower, ref_row=c // 2, last_row=c - 1)
    bwd = dict(mask=upper, ref_row=c - 1 - c // 2, last_row=0)

    def run(load, rows, tri3, state, out_scr, kind):
        lg_of, qkv_of = load
        for first in range(0, len(rows), SCAN_CUMSUM_GROUP):
            group = rows[first:first + SCAN_CUMSUM_GROUP]
            bs = _cumulative_decays([lg_of(r) for r in group], tri3)
            chunks = []
            for r, b in zip(group, bs):
                q, k, v = qkv_of(r)
                chunks.append((q, k, b, v))
            outs, state = _chunk_group(chunks, state, want_out=out_scr is not None, **kind)
            for r, o in zip(group, outs):
                out_scr[pl.ds(r, c), :] = o
        return state

    zero = jnp.zeros((dv, HEAD_K), F32)
    ctx_rows = [i * CTX_CHUNK for i in range(n_ctx_chunks)]
    sf_scr[...] = run(load_ctx_fwd, ctx_rows, tri_f, zero, None, fwd)
    sb_scr[...] = run(load_ctx_bwd, ctx_rows[::-1], tri_b, zero, None, bwd)

    per_trip = math.gcd(n_chunks, SCAN_CHUNKS_PER_TRIP)

    def step(i, carry):
        ns = [i * per_trip + j for j in range(per_trip)]
        rows_f = [pl.multiple_of(n * c, c) for n in ns]
        rows_b = [pl.multiple_of((n_chunks - 1 - n) * c, c) for n in ns]
        sf_scr[...] = run(load_fwd, rows_f, tri_f, sf_scr[...], of_scr, fwd)
        sb_scr[...] = run(load_bwd, rows_b, tri_b, sb_scr[...], ob_scr, bwd)
        return carry

    lax.fori_loop(0, n_chunks // per_trip, step, 0)

    o = of_scr[...] + ob_scr[...]
    gate = gate_ref[0]
    y_ref[0] = _rms(o) * g_ref[...] * (gate * _sigmoid(gate))


def _chunk_rows(ref, rows):
    sl = pl.ds(rows, GRID_W)
    return ref[0, sl, :] if len(ref.shape) == 3 else ref[sl, :]


def _chunk_loader(q_r, k_r, l_r, v_r):
    lg_of = lambda rows: _chunk_rows(l_r, rows)
    qkv_of = lambda rows: (None if q_r is None else _chunk_rows(q_r, rows), _chunk_rows(k_r, rows),
                           _chunk_rows(v_r, rows))
    return lg_of, qkv_of


def _hgrn_scan_body(q_ref, zf_ref, zb_ref, v_ref, gate_ref, czf_ref, czb_ref, cv_ref, lb_ref, g_ref,
                    y_ref, of_scr, ob_scr, sf_scr, sb_scr, kf, lf, kb, lb, ckf, clf, ckb, clb,
                    *, n_chunks, n_ctx_chunks):
    for z_r, row, k_scr, l_scr in ((zf_ref, 0, kf, lf), (zb_ref, 1, kb, lb), (czf_ref, 0, ckf, clf),
                                   (czb_ref, 1, ckb, clb)):
        lower = lb_ref[row:row + 1, :]
        f = lower + (1.0 - lower) * _sigmoid(z_r[0])
        k_scr[...] = 1.0 - f
        l_scr[...] = jnp.log(f)

    _scan_core(_chunk_loader(q_ref, kf, lf, v_ref), _chunk_loader(q_ref, kb, lb, v_ref),
               _chunk_loader(None, ckf, clf, cv_ref), _chunk_loader(None, ckb, clb, cv_ref),
               gate_ref, g_ref, y_ref, of_scr, ob_scr, sf_scr, sb_scr, n_chunks, n_ctx_chunks, HG_HEAD_V)


def _gla_scan_body(q_ref, k_ref, lf_ref, lbk_ref, v_ref, gate_ref, ck_ref, clf_ref, clb_ref, cv_ref, g_ref,
                   y_ref, of_scr, ob_scr, sf_scr, sb_scr, *, n_chunks, n_ctx_chunks):
    _scan_core(_chunk_loader(q_ref, k_ref, lf_ref, v_ref), _chunk_loader(q_ref, k_ref, lbk_ref, v_ref),
               _chunk_loader(None, ck_ref, clf_ref, cv_ref), _chunk_loader(None, ck_ref, clb_ref, cv_ref),
               gate_ref, g_ref, y_ref, of_scr, ob_scr, sf_scr, sb_scr, n_chunks, n_ctx_chunks, GLA_HEAD_V)


def _scan_call(body, n_heads, dv, lat, ctx, small, bsz, t, t_ctx, name, extra_scratch=()):
    head = lambda rows, w: pl.BlockSpec((1, rows, w), lambda b, h: (b, 0, h))
    in_specs = ([head(t, w) for _, w in lat] + [head(t_ctx, w) for _, w in ctx]
                + [pl.BlockSpec(bs, im) for _, bs, im in small])
    return pl.pallas_call(
        functools.partial(body, n_chunks=t // GRID_W, n_ctx_chunks=t_ctx // CTX_CHUNK),
        grid=(bsz, n_heads),
        in_specs=in_specs,
        out_specs=head(t, dv),
        out_shape=jax.ShapeDtypeStruct((bsz, t, n_heads * dv), F32),
        scratch_shapes=[pltpu.VMEM((t, dv), F32), pltpu.VMEM((t, dv), F32),
                        pltpu.VMEM((dv, HEAD_K), F32), pltpu.VMEM((dv, HEAD_K), F32), *extra_scratch],
        compiler_params=_params(2),
        name=name,
    )(*[a for a, _ in lat], *[a for a, _ in ctx], *[a for a, _, _ in small])


def _mixer_out_body(yh_ref, yg_ref, mh_ref, mg_ref, x_ref, gate1_ref, shift2_ref, scale2_ref, g2_ref,
                    wbh_ref, wbg_ref, wo_ref, wq_ref, k1_ref, k2_ref,
                    x1_ref, h2_ref, st_ref):
    yh = _dot(yh_ref[0].astype(BF16), wbh_ref[...])
    yg = _dot(yg_ref[0].astype(BF16), wbg_ref[...])
    y = _sigmoid(mh_ref[0]) * yh + _sigmoid(mg_ref[0]) * yg
    x1 = x_ref[0] + gate1_ref[0] * _dot(y.astype(BF16), wo_ref[...])
    x1_ref[0] = x1
    h2 = _rms(x1) * g2_ref[...] * (1.0 + scale2_ref[0]) + shift2_ref[0]
    h2b = h2.astype(BF16)
    h2_ref[0] = h2b
    q = _dot(h2b, wq_ref[...]).astype(BF16)
    for hh in range(2 * PEER_HEADS):
        keys = k1_ref if hh % 2 == 0 else k2_ref
        st_ref[hh] = lax.dot_general(keys[...], q[:, hh * PEER_SUB_DIM:(hh + 1) * PEER_SUB_DIM],
                                     (((1,), (1,)), ((), ())), preferred_element_type=F32)


def _mixer_out(y_hg, y_gla, m_hg, m_gla, x, gate1, shift2, scale2, g2, wbh, wbg, wo, wq, k1, k2):
    bsz, t, d = x.shape
    tm = ROW_TILE
    assert t % tm == 0
    row = pl.BlockSpec((1, tm, d), lambda b, i: (b, i, 0))
    per_batch = pl.BlockSpec((1, 1, d), lambda b, i: (b, 0, 0))
    tiles_per_batch = t // tm
    n = bsz * t
    return pl.pallas_call(
        _mixer_out_body,
        grid=(bsz, tiles_per_batch),
        in_specs=[row, row, row, row, row, per_batch, per_batch, per_batch, _resident((1, d)),
                  _resident(wbh.shape), _resident(wbg.shape), _resident(wo.shape), _resident(wq.shape),
                  _resident(k1.shape), _resident(k2.shape)],
        out_specs=[row, row,
                   pl.BlockSpec((2 * PEER_HEADS, PEER_N_KEYS, tm), lambda b, i: (0, 0, b * tiles_per_batch + i))],
        out_shape=[jax.ShapeDtypeStruct((bsz, t, d), F32), jax.ShapeDtypeStruct((bsz, t, d), BF16),
                   jax.ShapeDtypeStruct((2 * PEER_HEADS, PEER_N_KEYS, n), F32)],
        compiler_params=_params(2),
        name="mixer_out",
    )(y_hg, y_gla, m_hg, m_gla, x, gate1, shift2, scale2, g2, wbh, wbg, wo, wq, k1, k2)


_CAND_GROUPS = (("a", 0, 0), ("a", 0, 8), ("a", 1, 0), ("b", 0, 8), ("b", 0, 0),
                ("b", 1, 0), ("b", 2, 0), ("b", 3, 0), ("b", 4, 0), ("pad", 0, 0))
_CAND_ROWS = len(_CAND_GROUPS) * SUBLANES


def _candidate_tables():
    flat = np.full((_CAND_ROWS, 1), -1, np.float32)
    seen = set()
    for g, (kind, fixed, start) in enumerate(_CAND_GROUPS):
        for r in range(SUBLANES if kind != "pad" else 0):
            a, b = (fixed, start + r) if kind == "a" else (start + r, fixed)
            if (a + 1) * (b + 1) <= PEER_TOPK and (a, b) not in seen:
                seen.add((a, b))
                flat[g * SUBLANES + r, 0] = a * PEER_TOPK + b
    needed = {(a, b) for a in range(PEER_TOPK) for b in range(PEER_TOPK) if (a + 1) * (b + 1) <= PEER_TOPK}
    assert seen == needed
    return flat


def _extract_top(x, key, rounds):
    half = x.shape[0] // 2
    big = np.float32(2 ** 30)
    a, b = x[:half], x[half:]
    key_a, key_b = key
    first = (a > b) | ((a == b) & (key_a < key_b))
    win, lose = jnp.where(first, a, b), jnp.where(first, b, a)
    win_id, lose_id = jnp.where(first, key_a, key_b), jnp.where(first, key_b, key_a)
    vals, keys = [], []
    for _ in range(rounds):
        m = jnp.max(win, axis=0, keepdims=True)
        sel = jnp.min(jnp.where(win == m, win_id, big), axis=0, keepdims=True)
        vals.append(m)
        keys.append(sel)
        taken = win_id == sel
        win = jnp.where(taken, lose, win)
        win_id = jnp.where(taken, lose_id, win_id)
        lose = jnp.where(taken, -jnp.inf, lose)
    return vals, keys


def _pick_row(table, row_ids, sel):
    return jnp.sum(jnp.where(row_ids == sel, table, 0), axis=0, keepdims=True)


def _route_body(st_ref, flat_ref, key1_ref, key2_ref, gate_ref):
    tn = st_ref.shape[-1]
    key_ids = lax.broadcasted_iota(jnp.int32, (PEER_N_KEYS // 2, tn), 0).astype(F32)
    rank_ids = lax.broadcasted_iota(jnp.int32, (PEER_TOPK, tn), 0).astype(F32)
    flat = jnp.broadcast_to(flat_ref[...], (_CAND_ROWS, tn))
    valid = flat >= 0.0
    key_ids = (key_ids, key_ids + (PEER_N_KEYS // 2))
    flat_halves = tuple(jnp.broadcast_to(flat_ref[r:r + _CAND_ROWS // 2, :], (_CAND_ROWS // 2, tn))
                        for r in (0, _CAND_ROWS // 2))
    keys1, keys2, gates = [], [], []
    for h in range(PEER_HEADS):
        v1, i1 = _extract_top(st_ref[2 * h], key_ids, PEER_TOPK)
        v2, i2 = _extract_top(st_ref[2 * h + 1], key_ids, PEER_TOPK)
        v1c, i1c = jnp.concatenate(v1, axis=0), jnp.concatenate(i1, axis=0)
        v2c, i2c = jnp.concatenate(v2, axis=0), jnp.concatenate(i2, axis=0)
        groups = {"a": lambda fixed, start: v1[fixed] + v2c[start:start + SUBLANES],
                  "b": lambda fixed, start: v1c[start:start + SUBLANES] + v2[fixed],
                  "pad": lambda fixed, start: v2c[:SUBLANES]}
        cand = jnp.concatenate([groups[kind](fixed, start) for kind, fixed, start in _CAND_GROUPS], axis=0)
        cand = jnp.where(valid, cand, -jnp.inf)
        top_s, top_c = _extract_top(cand, flat_halves, PEER_TOPK)
        m = top_s[0]
        ex = [jnp.exp(s - m) for s in top_s]
        denom = functools.reduce(lambda a, b: a + b, ex)
        for s_k, c_k in zip(ex, top_c):
            c_int = c_k.astype(jnp.int32)
            a = (c_int >> 4).astype(F32)
            b = (c_int & (PEER_TOPK - 1)).astype(F32)
            keys1.append(_pick_row(i1c, rank_ids, a))
            keys2.append(_pick_row(i2c, rank_ids, b))
            gates.append(s_k / denom)
    key1_ref[...] = jnp.concatenate(keys1, axis=0).T
    key2_ref[...] = jnp.concatenate(keys2, axis=0).T
    gate_ref[...] = jnp.concatenate(gates, axis=0).T


def _route(scores_t):
    n = scores_t.shape[-1]
    tn = ROUTE_TILE
    assert n % tn == 0
    flat = jnp.asarray(_candidate_tables())
    return pl.pallas_call(
        _route_body,
        grid=(n // tn,),
        in_specs=[pl.BlockSpec((2 * PEER_HEADS, PEER_N_KEYS, tn), lambda i: (0, 0, i)),
                  pl.BlockSpec(flat.shape, lambda i: (0, 0))],
        out_specs=[pl.BlockSpec((tn, N_SEL), lambda i: (i, 0))] * 3,
        out_shape=[jax.ShapeDtypeStruct((n, N_SEL), F32)] * 3,
        compiler_params=_params(1),
        name="peer_route",
    )(scores_t, flat)


def _peer_dense_body(h_ref, key1_ref, key2_ref, gate_ref, x1_ref, gate2_ref, fg_ref, ut_hbm, v_hbm, o_ref, g_scr,
                     ut_buf, v_buf, sem, *, tm, blocks_per_step, n_blocks):
    step = pl.program_id(0)
    slab = tm + SUBLANES

    def copies(block, slot):
        return (pltpu.make_async_copy(ut_hbm.at[block], ut_buf.at[slot], sem.at[0, slot]),
                pltpu.make_async_copy(v_hbm.at[block], v_buf.at[slot], sem.at[1, slot]))

    def start(block, slot):
        for cp in copies(block, slot):
            cp.start()

    def wait(slot):
        for cp in copies(0, slot):
            cp.wait()

    @pl.when(step == 0)
    def _():
        start(0, 0)

    def build_gates():
        o_ref[...] = jnp.zeros_like(o_ref)
        sub = lax.broadcasted_iota(jnp.int32, (PEER_N_KEYS, N_SEL), 0).astype(F32).astype(BF16)
        zero = jnp.zeros((PEER_N_KEYS, N_SEL), BF16)
        one = jnp.ones((PEER_N_KEYS, N_SEL), BF16)

        def tokens(i, carry):
            for j in range(PEER_GATE_TOKENS_PER_TRIP):
                t = i * PEER_GATE_TOKENS_PER_TRIP + j
                key1 = key1_ref[pl.ds(t, 1), :].astype(BF16)
                key2 = key2_ref[pl.ds(t, 1), :].astype(BF16)
                gate = gate_ref[pl.ds(t, 1), :]
                hi = gate.astype(BF16)
                lo = (gate - hi.astype(F32)).astype(BF16)
                on1 = sub == key1
                lhs = jnp.concatenate([jnp.where(on1, hi, zero), jnp.where(on1, lo, zero)], axis=1)
                on2 = jnp.where(sub == key2, one, zero)
                g_t = lax.dot_general(lhs, jnp.concatenate([on2, on2], axis=1), (((1,), (1,)), ((), ())),
                                      preferred_element_type=F32)
                g_scr[pl.ds(t, PEER_N_KEYS, stride=slab), :] = g_t
            return carry

        lax.fori_loop(0, tm // PEER_GATE_TOKENS_PER_TRIP, tokens, 0)

    build_gates()

    def accumulate(block, slot):
        act = _dot(h_ref[...], ut_buf[slot])
        g_blk = jnp.concatenate(
            [g_scr[pl.ds(pl.multiple_of((block * blocks_per_step + j) * slab, SUBLANES), tm), :]
             for j in range(blocks_per_step)], axis=1)
        coef = (g_blk * _gelu(act)).astype(BF16)
        o_ref[...] += _dot(coef, v_buf[slot])

    def pair(j, carry):
        block = 2 * j
        start(block + 1, 1)
        wait(0)
        accumulate(block, 0)
        start((block + 2) % n_blocks, 0)
        wait(1)
        accumulate(block + 1, 1)
        return carry

    lax.fori_loop(0, n_blocks // 2, pair, 0)

    @pl.when(step == pl.num_programs(0) - 1)
    def _():
        wait(0)

    x2 = x1_ref[...] + gate2_ref[0] * o_ref[...]
    o_ref[...] = _rms(x2) * fg_ref[...]


def _peer_dense(h2_bf16, key1, key2, gate, x1, gate2, final_g, u_t, v, tokens_per_batch):
    n, d = x1.shape
    n_experts = v.shape[0]
    tm, eb = PEER_ROW_TILE, PEER_EXPERT_TILE
    assert tokens_per_batch % tm == 0 and n_experts % eb == 0 and eb % PEER_N_KEYS == 0
    tiles_per_batch = tokens_per_batch // tm
    n_blocks = n_experts // eb
    assert n_blocks % 2 == 0
    row = lambda w: pl.BlockSpec((tm, w), lambda i: (i, 0))
    once_per_tile = lambda w: pl.BlockSpec((tm, w), lambda i: (i, 0), pipeline_mode=pl.Buffered(1))
    return pl.pallas_call(
        functools.partial(_peer_dense_body, tm=tm, blocks_per_step=eb // PEER_N_KEYS, n_blocks=n_blocks),
        grid=(n // tm,),
        in_specs=[once_per_tile(d), row(N_SEL), row(N_SEL), row(N_SEL), once_per_tile(d),
                  pl.BlockSpec((1, 1, d), lambda i: (i // tiles_per_batch, 0, 0)),
                  pl.BlockSpec((1, d), lambda i: (0, 0)),
                  pl.BlockSpec(memory_space=pl.ANY),
                  pl.BlockSpec(memory_space=pl.ANY)],
        out_specs=row(d),
        out_shape=jax.ShapeDtypeStruct((n, d), F32),
        scratch_shapes=[pltpu.VMEM((PEER_N_KEYS * (tm + SUBLANES), PEER_N_KEYS), F32),
                        pltpu.VMEM((2, d, eb), BF16), pltpu.VMEM((2, eb, d), BF16),
                        pltpu.SemaphoreType.DMA((2, 2))],
        compiler_params=_params(1),
        name="peer_dense",
    )(h2_bf16, key1, key2, gate, x1, gate2, final_g,
      u_t.reshape(d, n_blocks, eb).transpose(1, 0, 2), v.reshape(n_blocks, eb, d))


def kernel(x, c, ctx, c_ctx, ada_w, ada_b, norm_mix_g, w_in, hgrn_lb_logits, hgrn_norm_g, gla_gk_w, gla_gk_b, gla_norm_g, w_branch_hgrn, w_branch_gla, w_out, norm_ffn_g, peer_wq, peer_k1, peer_k2, peer_u, peer_v, final_g):
    bsz, seq, d = x.shape
    t_ctx = ctx.shape[1]
    depth = ada_w.shape[0]
    assert depth == 1 and d == D_MODEL and w_in.shape[-1] == D_IN
    l = 0
    row2 = lambda a: a.reshape(1, -1)

    lower = jnp.cumsum(jax.nn.softmax(hgrn_lb_logits, axis=0), axis=0)[l]
    lower = jnp.concatenate([lower, jnp.zeros((SUBLANES - 2, HG_KEY), F32)], axis=0)
    cvec = jnp.concatenate([c, c_ctx[None], jnp.zeros((SUBLANES - 1, d), F32)], axis=0)
    mod_all = _adaln(cvec, ada_w[l], ada_b[l])
    mod = [m.reshape(bsz, 1, d) for m in jnp.split(mod_all[:bsz], N_MOD, axis=-1)]
    mod_c = [jnp.broadcast_to(m.reshape(1, 1, d), (bsz, 1, d)) for m in jnp.split(mod_all[bsz:bsz + 1], N_MOD, axis=-1)]
    w_in_b = w_in[l].astype(BF16)
    zr = jnp.zeros((GLA_GATE_RANK, GLA_KEY), F32)
    wrank = jnp.concatenate([jnp.concatenate([gla_gk_w[l, 0], zr], axis=1),
                             jnp.concatenate([zr, gla_gk_w[l, 1]], axis=1)], axis=0).astype(BF16)
    brank = gla_gk_b[l].reshape(1, 2 * GLA_KEY)

    lat = _inproj(x, mod[0], mod[1], row2(norm_mix_g[l]), w_in_b, wrank, brank)
    cx = _inproj(ctx, mod_c[0], mod_c[1], row2(norm_mix_g[l]), w_in_b, wrank, brank)
    (hq, zf, zb, hi, hgate, gq, gk, gv, gg, glf, glb, m_hg, m_gla) = lat
    (_, czf, czb, chi, _, _, cgk, cgv, _, cglf, cglb, _, _) = cx

    y_hg = _scan_call(
        _hgrn_scan_body, HG_HEADS, HG_HEAD_V,
        [(hq, HEAD_K), (zf, HEAD_K), (zb, HEAD_K), (hi, HG_HEAD_V), (hgate, HG_HEAD_V)],
        [(czf, HEAD_K), (czb, HEAD_K), (chi, HG_HEAD_V)],
        [(lower, (SUBLANES, HEAD_K), lambda b, h: (0, h)), (row2(hgrn_norm_g[l]), (1, HG_HEAD_V), lambda b, h: (0, 0))],
        bsz, seq, t_ctx, "hgrn_scan",
        extra_scratch=[pltpu.VMEM((seq, HEAD_K), F32)] * 4 + [pltpu.VMEM((t_ctx, HEAD_K), F32)] * 4)
    y_gla = _scan_call(
        _gla_scan_body, GLA_HEADS, GLA_HEAD_V,
        [(gq, HEAD_K), (gk, HEAD_K), (glf, HEAD_K), (glb, HEAD_K), (gv, GLA_HEAD_V), (gg, GLA_HEAD_V)],
        [(cgk, HEAD_K), (cglf, HEAD_K), (cglb, HEAD_K), (cgv, GLA_HEAD_V)],
        [(row2(gla_norm_g[l]), (1, GLA_HEAD_V), lambda b, h: (0, 0))],
        bsz, seq, t_ctx, "gla_scan")

    x1, h2, scores_t = _mixer_out(
        y_hg, y_gla, m_hg, m_gla, x, mod[2], mod[3], mod[4], row2(norm_ffn_g[l]),
        w_branch_hgrn[l].astype(BF16), w_branch_gla[l].astype(BF16), w_out[l].astype(BF16),
        peer_wq[l].astype(BF16), peer_k1[l].astype(BF16), peer_k2[l].astype(BF16))

    key1, key2, gate = _route(scores_t)
    n = bsz * seq
    out = _peer_dense(h2.reshape(n, d), key1, key2, gate, x1.reshape(n, d), mod[5], row2(final_g),
                      peer_u[l].astype(BF16).T, peer_v[l].astype(BF16), seq)
    return out.reshape(bsz, seq, d)
```

```python
import functools
import math

import jax
import jax.numpy as jnp
import numpy as np
from jax import lax
from jax.experimental import pallas as pl
from jax.experimental.pallas import tpu as pltpu

F32 = jnp.float32
BF16 = jnp.bfloat16

D_MODEL = 1024
GRID_W = 64
CTX_CHUNK = 64
N_MOD = 6
EPS = 1e-6

HG_HEADS = 8
HEAD_K = 128
HG_KEY = HG_HEADS * HEAD_K
HG_VAL = D_MODEL
HG_HEAD_V = HG_VAL // HG_HEADS
GLA_HEADS = 4
GLA_KEY = D_MODEL // 2
GLA_VAL = D_MODEL
GLA_HEAD_V = GLA_VAL // GLA_HEADS
GLA_GATE_RANK = 16
GLA_GATE_NORMALIZER = 16.0

C_HQ, C_ZF, C_ZB, C_HI, C_HGATE = 0, 1024, 2048, 3072, 4096
C_GQ, C_GK, C_GV, C_GG = 5120, 5632, 6144, 7168
C_RANK = 8192
C_MHG = C_RANK + 2 * GLA_GATE_RANK
C_MGLA = C_MHG + D_MODEL
D_IN = C_MGLA + D_MODEL

PEER_HEADS = 8
PEER_N_KEYS = 128
PEER_SUB_DIM = 128
PEER_TOPK = 16
N_SEL = PEER_HEADS * PEER_TOPK

SUBLANES = 8
VMEM_LIMIT = 56 * 1024 * 1024

ROW_TILE = 256
INPROJ_ROW_TILE = 256
ROUTE_TILE = 256
PEER_ROW_TILE = 512
PEER_EXPERT_TILE = 1024
PEER_GATE_TOKENS_PER_TRIP = 64
SCAN_CHUNKS_PER_TRIP = 32
SCAN_CUMSUM_GROUP = 16


def _resident(shape):
    return pl.BlockSpec(shape, lambda *_: (0,) * len(shape), pipeline_mode=pl.Buffered(1))


def _params(n_axes):
    return pltpu.CompilerParams(dimension_semantics=("arbitrary",) * n_axes, vmem_limit_bytes=VMEM_LIMIT)


def _split3(a):
    p1 = a.astype(BF16)
    r1 = a - p1.astype(F32)
    p2 = r1.astype(BF16)
    p3 = (r1 - p2.astype(F32)).astype(BF16)
    return p1, p2, p3


def _dot(a, b):
    return jnp.dot(a, b, preferred_element_type=F32)


def _rms(x):
    return x * lax.rsqrt(jnp.mean(x * x, axis=-1, keepdims=True) + EPS)


def _gelu(x):
    return 0.5 * x * (1.0 + lax.erf(x * np.float32(2.0 ** -0.5)))


def _sigmoid(x):
    return 1.0 / (1.0 + jnp.exp(-x))


def _log_sigmoid(x):
    return jnp.minimum(x, 0.0) - jnp.log1p(jnp.exp(-jnp.abs(x)))


def _adaln_body(c_ref, w_ref, b_ref, o_ref):
    c = c_ref[...]
    s = (c * _sigmoid(c)).astype(BF16)
    o_ref[...] = _dot(s, w_ref[...].astype(BF16)) + b_ref[...]


def _adaln(cvec, w, b):
    rows, d = cvec.shape
    n_out = w.shape[1]
    return pl.pallas_call(
        _adaln_body,
        grid=(n_out // d,),
        in_specs=[pl.BlockSpec((rows, d), lambda j: (0, 0)),
                  pl.BlockSpec((d, d), lambda j: (0, j)),
                  pl.BlockSpec((1, d), lambda j: (0, j))],
        out_specs=pl.BlockSpec((rows, d), lambda j: (0, j)),
        out_shape=jax.ShapeDtypeStruct((rows, n_out), F32),
        compiler_params=_params(1),
        name="adaln",
    )(cvec, w, b.reshape(1, n_out))


def _inproj_body(x_ref, shift_ref, scale_ref, g_ref, w_ref, wrank_ref, brank_ref,
                 hq_ref, zf_ref, zb_ref, hi_ref, hgate_ref, gq_ref, gk_ref, gv_ref, gg_ref,
                 glf_ref, glb_ref, mhg_ref, mgla_ref):
    h = _rms(x_ref[0]) * g_ref[...] * (1.0 + scale_ref[0]) + shift_ref[0]
    hb = h.astype(BF16)

    def cols(start, width):
        return _dot(hb, w_ref[:, start:start + width])

    hq_ref[0] = cols(C_HQ, HG_KEY) * np.float32(HEAD_K ** -0.5)
    zf_ref[0] = cols(C_ZF, HG_KEY)
    zb_ref[0] = cols(C_ZB, HG_KEY)
    hi_ref[0] = cols(C_HI, HG_VAL)
    hgate_ref[0] = cols(C_HGATE, HG_VAL)
    gq_ref[0] = cols(C_GQ, GLA_KEY) * np.float32(HEAD_K ** -0.5)
    gk_ref[0] = cols(C_GK, GLA_KEY)
    gv_ref[0] = cols(C_GV, GLA_VAL)
    gg_ref[0] = cols(C_GG, GLA_VAL)
    mhg_ref[0] = cols(C_MHG, D_MODEL)
    mgla_ref[0] = cols(C_MGLA, D_MODEL)
    rank = cols(C_RANK, 2 * GLA_GATE_RANK).astype(BF16)
    pre = _dot(rank, wrank_ref[...]) + brank_ref[...]
    lg = _log_sigmoid(pre) / GLA_GATE_NORMALIZER
    glf_ref[0] = lg[:, :GLA_KEY]
    glb_ref[0] = lg[:, GLA_KEY:]


def _inproj(x, shift, scale, g, w_bf16, wrank, brank):
    bsz, t, d = x.shape
    tm = INPROJ_ROW_TILE
    assert t % tm == 0
    widths = (HG_KEY, HG_KEY, HG_KEY, HG_VAL, HG_VAL, GLA_KEY, GLA_KEY, GLA_VAL, GLA_VAL,
              GLA_KEY, GLA_KEY, D_MODEL, D_MODEL)
    row = lambda w: pl.BlockSpec((1, tm, w), lambda b, i: (b, i, 0))
    per_batch = pl.BlockSpec((1, 1, d), lambda b, i: (b, 0, 0))
    return pl.pallas_call(
        _inproj_body,
        grid=(bsz, t // tm),
        in_specs=[row(d), per_batch, per_batch, _resident((1, d)), _resident(w_bf16.shape),
                  _resident(wrank.shape), _resident(brank.shape)],
        out_specs=[row(w) for w in widths],
        out_shape=[jax.ShapeDtypeStruct((bsz, t, w), F32) for w in widths],
        compiler_params=_params(2),
        name="inproj",
    )(x, shift, scale, g, w_bf16, wrank, brank)


def _cumulative_decays(lgs, tri3):
    wide = jnp.concatenate(lgs, axis=1)
    b = _dot(tri3, jnp.concatenate(_split3(wide), axis=0))
    return [b[:, j * HEAD_K:(j + 1) * HEAD_K] for j in range(len(lgs))]


def _chunk_group(chunks, state, mask, ref_row, last_row, want_out):
    prepared = []
    for q, k, b, v in chunks:
        b_ref = b[ref_row:ref_row + 1]
        b_last = b[last_row:last_row + 1]
        vb = v.astype(BF16)
        kl = (k * jnp.exp(b_last - b)).astype(BF16)
        increment = lax.dot_general(vb, kl, (((0,), (0,)), ((), ())), preferred_element_type=F32)
        scores = qs = None
        if want_out:
            qd = (q * jnp.exp(b - b_ref)).astype(BF16)
            kd = (k * jnp.exp(b_ref - b)).astype(BF16)
            scores = lax.dot_general(qd, kd, (((1,), (1,)), ((), ())), preferred_element_type=F32)
            qs = (q * jnp.exp(b)).astype(BF16)
        prepared.append((vb, scores, qs, jnp.exp(b_last), increment))
    intra = [_dot(jnp.where(mask, scores, 0.0).astype(BF16), vb) if want_out else None
             for vb, scores, _, _, _ in prepared]
    outs = []
    for (_, _, qs, decay, increment), o_intra in zip(prepared, intra):
        if want_out:
            outs.append(o_intra + lax.dot_general(qs, state.astype(BF16), (((1,), (1,)), ((), ())),
                                                  preferred_element_type=F32))
        state = state * decay + increment
    return outs, state


def _scan_core(load_fwd, load_bwd, load_ctx_fwd, load_ctx_bwd, gate_ref, g_ref, y_ref,
               of_scr, ob_scr, sf_scr, sb_scr, n_chunks, n_ctx_chunks, dv):
    c = GRID_W
    ri = lax.broadcasted_iota(jnp.int32, (c, c), 0)
    ci = lax.broadcasted_iota(jnp.int32, (c, c), 1)
    lower = ci <= ri
    upper = ci >= ri
    tri_f = jnp.concatenate([lower.astype(BF16)] * 3, axis=1)
    tri_b = jnp.concatenate([upper.astype(BF16)] * 3, axis=1)
    fwd = dict(mask=lower, ref_row=c // 2, last_row=c - 1)
    bwd = dict(mask=upper, ref_row=c - 1 - c // 2, last_row=0)

    def run(load, rows, tri3, state, out_scr, kind):
        lg_of, qkv_of = load
        for first in range(0, len(rows), SCAN_CUMSUM_GROUP):
            group = rows[first:first + SCAN_CUMSUM_GROUP]
            bs = _cumulative_decays([lg_of(r) for r in group], tri3)
            chunks = []
            for r, b in zip(group, bs):
                q, k, v = qkv_of(r)
                chunks.append((q, k, b, v))
            outs, state = _chunk_group(chunks, state, want_out=out_scr is not None, **kind)
            for r, o in zip(group, outs):
                out_scr[pl.ds(r, c), :] = o
        return state

    zero = jnp.zeros((dv, HEAD_K), F32)
    ctx_rows = [i * CTX_CHUNK for i in range(n_ctx_chunks)]
    sf_scr[...] = run(load_ctx_fwd, ctx_rows, tri_f, zero, None, fwd)
    sb_scr[...] = run(load_ctx_bwd, ctx_rows[::-1], tri_b, zero, None, bwd)

    per_trip = math.gcd(n_chunks, SCAN_CHUNKS_PER_TRIP)

    def step(i, carry):
        ns = [i * per_trip + j for j in range(per_trip)]
        rows_f = [pl.multiple_of(n * c, c) for n in ns]
        rows_b = [pl.multiple_of((n_chunks - 1 - n) * c, c) for n in ns]
        sf_scr[...] = run(load_fwd, rows_f, tri_f, sf_scr[...], of_scr, fwd)
        sb_scr[...] = run(load_bwd, rows_b, tri_b, sb_scr[...], ob_scr, bwd)
        return carry

    lax.fori_loop(0, n_chunks // per_trip, step, 0)

    o = of_scr[...] + ob_scr[...]
    gate = gate_ref[0]
    y_ref[0] = (_rms(o) * g_ref[...] * (gate * _sigmoid(gate))).astype(BF16)


def _chunk_rows(ref, rows):
    sl = pl.ds(rows, GRID_W)
    return ref[0, sl, :] if len(ref.shape) == 3 else ref[sl, :]


def _chunk_loader(q_r, k_r, l_r, v_r):
    lg_of = lambda rows: _chunk_rows(l_r, rows)
    qkv_of = lambda rows: (None if q_r is None else _chunk_rows(q_r, rows), _chunk_rows(k_r, rows),
                           _chunk_rows(v_r, rows))
    return lg_of, qkv_of


def _hgrn_scan_body(q_ref, zf_ref, zb_ref, v_ref, gate_ref, czf_ref, czb_ref, cv_ref, lb_ref, g_ref,
                    y_ref, of_scr, ob_scr, sf_scr, sb_scr, kf, lf, kb, lb, ckf, clf, ckb, clb,
                    *, n_chunks, n_ctx_chunks):
    for z_r, row, k_scr, l_scr in ((zf_ref, 0, kf, lf), (zb_ref, 1, kb, lb), (czf_ref, 0, ckf, clf),
                                   (czb_ref, 1, ckb, clb)):
        lower = lb_ref[row:row + 1, :]
        f = lower + (1.0 - lower) * _sigmoid(z_r[0])
        k_scr[...] = 1.0 - f
        l_scr[...] = jnp.log(f)

    _scan_core(_chunk_loader(q_ref, kf, lf, v_ref), _chunk_loader(q_ref, kb, lb, v_ref),
               _chunk_loader(None, ckf, clf, cv_ref), _chunk_loader(None, ckb, clb, cv_ref),
               gate_ref, g_ref, y_ref, of_scr, ob_scr, sf_scr, sb_scr, n_chunks, n_ctx_chunks, HG_HEAD_V)


def _gla_scan_body(q_ref, k_ref, lf_ref, lbk_ref, v_ref, gate_ref, ck_ref, clf_ref, clb_ref, cv_ref, g_ref,
                   y_ref, of_scr, ob_scr, sf_scr, sb_scr, *, n_chunks, n_ctx_chunks):
    _scan_core(_chunk_loader(q_ref, k_ref, lf_ref, v_ref), _chunk_loader(q_ref, k_ref, lbk_ref, v_ref),
               _chunk_loader(None, ck_ref, clf_ref, cv_ref), _chunk_loader(None, ck_ref, clb_ref, cv_ref),
               gate_ref, g_ref, y_ref, of_scr, ob_scr, sf_scr, sb_scr, n_chunks, n_ctx_chunks, GLA_HEAD_V)


def _scan_call(body, n_heads, dv, lat, ctx, small, bsz, t, t_ctx, name, extra_scratch=()):
    head = lambda rows, w: pl.BlockSpec((1, rows, w), lambda b, h: (b, 0, h))
    in_specs = ([head(t, w) for _, w in lat] + [head(t_ctx, w) for _, w in ctx]
                + [pl.BlockSpec(bs, im) for _, bs, im in small])
    return pl.pallas_call(
        functools.partial(body, n_chunks=t // GRID_W, n_ctx_chunks=t_ctx // CTX_CHUNK),
        grid=(bsz, n_heads),
        in_specs=in_specs,
        out_specs=head(t, dv),
        out_shape=jax.ShapeDtypeStruct((bsz, t, n_heads * dv), BF16),
        scratch_shapes=[pltpu.VMEM((t, dv), F32), pltpu.VMEM((t, dv), F32),
                        pltpu.VMEM((dv, HEAD_K), F32), pltpu.VMEM((dv, HEAD_K), F32), *extra_scratch],
        compiler_params=_params(2),
        name=name,
    )(*[a for a, _ in lat], *[a for a, _ in ctx], *[a for a, _, _ in small])


def _mixer_out_body(yh_ref, yg_ref, mh_ref, mg_ref, x_ref, gate1_ref, shift2_ref, scale2_ref, g2_ref,
                    wbh_ref, wbg_ref, wo_ref, wq_ref, k1_ref, k2_ref,
                    x1_ref, h2_ref, st_ref):
    yh = _dot(yh_ref[0].astype(BF16), wbh_ref[...])
    yg = _dot(yg_ref[0].astype(BF16), wbg_ref[...])
    y = _sigmoid(mh_ref[0]) * yh + _sigmoid(mg_ref[0]) * yg
    x1 = x_ref[0] + gate1_ref[0] * _dot(y.astype(BF16), wo_ref[...])
    x1_ref[0] = x1
    h2 = _rms(x1) * g2_ref[...] * (1.0 + scale2_ref[0]) + shift2_ref[0]
    h2b = h2.astype(BF16)
    h2_ref[0] = h2b
    q = _dot(h2b, wq_ref[...]).astype(BF16)
    for hh in range(2 * PEER_HEADS):
        keys = k1_ref if hh % 2 == 0 else k2_ref
        st_ref[hh] = lax.dot_general(keys[...], q[:, hh * PEER_SUB_DIM:(hh + 1) * PEER_SUB_DIM],
                                     (((1,), (1,)), ((), ())), preferred_element_type=F32)


def _mixer_out(y_hg, y_gla, m_hg, m_gla, x, gate1, shift2, scale2, g2, wbh, wbg, wo, wq, k1, k2):
    bsz, t, d = x.shape
    tm = ROW_TILE
    assert t % tm == 0
    row = pl.BlockSpec((1, tm, d), lambda b, i: (b, i, 0))
    per_batch = pl.BlockSpec((1, 1, d), lambda b, i: (b, 0, 0))
    tiles_per_batch = t // tm
    n = bsz * t
    return pl.pallas_call(
        _mixer_out_body,
        grid=(bsz, tiles_per_batch),
        in_specs=[row, row, row, row, row, per_batch, per_batch, per_batch, _resident((1, d)),
                  _resident(wbh.shape), _resident(wbg.shape), _resident(wo.shape), _resident(wq.shape),
                  _resident(k1.shape), _resident(k2.shape)],
        out_specs=[row, row,
                   pl.BlockSpec((2 * PEER_HEADS, PEER_N_KEYS, tm), lambda b, i: (0, 0, b * tiles_per_batch + i))],
        out_shape=[jax.ShapeDtypeStruct((bsz, t, d), F32), jax.ShapeDtypeStruct((bsz, t, d), BF16),
                   jax.ShapeDtypeStruct((2 * PEER_HEADS, PEER_N_KEYS, n), F32)],
        compiler_params=_params(2),
        name="mixer_out",
    )(y_hg, y_gla, m_hg, m_gla, x, gate1, shift2, scale2, g2, wbh, wbg, wo, wq, k1, k2)


_CAND_GROUPS = (("a", 0, 0), ("a", 0, 8), ("a", 1, 0), ("b", 0, 8), ("b", 0, 0),
                ("b", 1, 0), ("b", 2, 0), ("b", 3, 0), ("b", 4, 0), ("pad", 0, 0))
_CAND_ROWS = len(_CAND_GROUPS) * SUBLANES


def _candidate_tables():
    flat = np.full((_CAND_ROWS, 1), -1, np.float32)
    seen = set()
    for g, (kind, fixed, start) in enumerate(_CAND_GROUPS):
        for r in range(SUBLANES if kind != "pad" else 0):
            a, b = (fixed, start + r) if kind == "a" else (start + r, fixed)
            if (a + 1) * (b + 1) <= PEER_TOPK and (a, b) not in seen:
                seen.add((a, b))
                flat[g * SUBLANES + r, 0] = a * PEER_TOPK + b
    needed = {(a, b) for a in range(PEER_TOPK) for b in range(PEER_TOPK) if (a + 1) * (b + 1) <= PEER_TOPK}
    assert seen == needed
    return flat


def _extract_top(x, key, rounds):
    half = x.shape[0] // 2
    big = np.float32(2 ** 30)
    a, b = x[:half], x[half:]
    key_a, key_b = key
    first = (a > b) | ((a == b) & (key_a < key_b))
    win, lose = jnp.where(first, a, b), jnp.where(first, b, a)
    win_id, lose_id = jnp.where(first, key_a, key_b), jnp.where(first, key_b, key_a)
    vals, keys = [], []
    for _ in range(rounds):
        m = jnp.max(win, axis=0, keepdims=True)
        sel = jnp.min(jnp.where(win == m, win_id, big), axis=0, keepdims=True)
        vals.append(m)
        keys.append(sel)
        taken = win_id == sel
        win = jnp.where(taken, lose, win)
        win_id = jnp.where(taken, lose_id, win_id)
        lose = jnp.where(taken, -jnp.inf, lose)
    return vals, keys


def _pick_row(table, row_ids, sel):
    return jnp.sum(jnp.where(row_ids == sel, table, 0), axis=0, keepdims=True)


def _route_body(st_ref, flat_ref, key1_ref, key2_ref, gate_ref):
    tn = st_ref.shape[-1]
    key_ids = lax.broadcasted_iota(jnp.int32, (PEER_N_KEYS // 2, tn), 0).astype(F32)
    rank_ids = lax.broadcasted_iota(jnp.int32, (PEER_TOPK, tn), 0).astype(F32)
    flat = jnp.broadcast_to(flat_ref[...], (_CAND_ROWS, tn))
    valid = flat >= 0.0
    key_ids = (key_ids, key_ids + (PEER_N_KEYS // 2))
    flat_halves = tuple(jnp.broadcast_to(flat_ref[r:r + _CAND_ROWS // 2, :], (_CAND_ROWS // 2, tn))
                        for r in (0, _CAND_ROWS // 2))
    keys1, keys2, gates = [], [], []
    for h in range(PEER_HEADS):
        v1, i1 = _extract_top(st_ref[2 * h], key_ids, PEER_TOPK)
        v2, i2 = _extract_top(st_ref[2 * h + 1], key_ids, PEER_TOPK)
        v1c, i1c = jnp.concatenate(v1, axis=0), jnp.concatenate(i1, axis=0)
        v2c, i2c = jnp.concatenate(v2, axis=0), jnp.concatenate(i2, axis=0)
        groups = {"a": lambda fixed, start: v1[fixed] + v2c[start:start + SUBLANES],
                  "b": lambda fixed, start: v1c[start:start + SUBLANES] + v2[fixed],
                  "pad": lambda fixed, start: v2c[:SUBLANES]}
        cand = jnp.concatenate([groups[kind](fixed, start) for kind, fixed, start in _CAND_GROUPS], axis=0)
        cand = jnp.where(valid, cand, -jnp.inf)
        top_s, top_c = _extract_top(cand, flat_halves, PEER_TOPK)
        m = top_s[0]
        ex = [jnp.exp(s - m) for s in top_s]
        denom = functools.reduce(lambda a, b: a + b, ex)
        for s_k, c_k in zip(ex, top_c):
            c_int = c_k.astype(jnp.int32)
            a = (c_int >> 4).astype(F32)
            b = (c_int & (PEER_TOPK - 1)).astype(F32)
            keys1.append(_pick_row(i1c, rank_ids, a))
            keys2.append(_pick_row(i2c, rank_ids, b))
            gates.append(s_k / denom)
    key1_ref[...] = jnp.concatenate(keys1, axis=0).T
    key2_ref[...] = jnp.concatenate(keys2, axis=0).T
    gate_ref[...] = jnp.concatenate(gates, axis=0).T


def _route(scores_t):
    n = scores_t.shape[-1]
    tn = ROUTE_TILE
    assert n % tn == 0
    flat = jnp.asarray(_candidate_tables())
    return pl.pallas_call(
        _route_body,
        grid=(n // tn,),
        in_specs=[pl.BlockSpec((2 * PEER_HEADS, PEER_N_KEYS, tn), lambda i: (0, 0, i)),
                  pl.BlockSpec(flat.shape, lambda i: (0, 0))],
        out_specs=[pl.BlockSpec((tn, N_SEL), lambda i: (i, 0))] * 3,
        out_shape=[jax.ShapeDtypeStruct((n, N_SEL), F32)] * 3,
        compiler_params=_params(1),
        name="peer_route",
    )(scores_t, flat)


def _peer_dense_body(h_ref, key1_ref, key2_ref, gate_ref, x1_ref, gate2_ref, fg_ref, ut_hbm, v_hbm, o_ref, g_scr,
                     ut_buf, v_buf, sem, *, tm, blocks_per_step, n_blocks):
    step = pl.program_id(0)
    slab = tm + SUBLANES

    def copies(block, slot):
        return (pltpu.make_async_copy(ut_hbm.at[block], ut_buf.at[slot], sem.at[0, slot]),
                pltpu.make_async_copy(v_hbm.at[block], v_buf.at[slot], sem.at[1, slot]))

    def start(block, slot):
        for cp in copies(block, slot):
            cp.start()

    def wait(slot):
        for cp in copies(0, slot):
            cp.wait()

    @pl.when(step == 0)
    def _():
        start(0, 0)

    def build_gates():
        o_ref[...] = jnp.zeros_like(o_ref)
        sub = lax.broadcasted_iota(jnp.int32, (PEER_N_KEYS, N_SEL), 0).astype(F32).astype(BF16)
        zero = jnp.zeros((PEER_N_KEYS, N_SEL), BF16)
        one = jnp.ones((PEER_N_KEYS, N_SEL), BF16)

        def tokens(i, carry):
            for j in range(PEER_GATE_TOKENS_PER_TRIP):
                t = i * PEER_GATE_TOKENS_PER_TRIP + j
                key1 = key1_ref[pl.ds(t, 1), :].astype(BF16)
                key2 = key2_ref[pl.ds(t, 1), :].astype(BF16)
                gate = gate_ref[pl.ds(t, 1), :]
                hi = gate.astype(BF16)
                lo = (gate - hi.astype(F32)).astype(BF16)
                on1 = sub == key1
                lhs = jnp.concatenate([jnp.where(on1, hi, zero), jnp.where(on1, lo, zero)], axis=1)
                on2 = jnp.where(sub == key2, one, zero)
                g_t = lax.dot_general(lhs, jnp.concatenate([on2, on2], axis=1), (((1,), (1,)), ((), ())),
                                      preferred_element_type=F32)
                g_scr[pl.ds(t, PEER_N_KEYS, stride=slab), :] = g_t
            return carry

        lax.fori_loop(0, tm // PEER_GATE_TOKENS_PER_TRIP, tokens, 0)

    build_gates()

    def accumulate(block, slot):
        act = _dot(h_ref[...], ut_buf[slot])
        g_blk = jnp.concatenate(
            [g_scr[pl.ds(pl.multiple_of((block * blocks_per_step + j) * slab, SUBLANES), tm), :]
             for j in range(blocks_per_step)], axis=1)
        coef = (g_blk * _gelu(act)).astype(BF16)
        o_ref[...] += _dot(coef, v_buf[slot])

    def pair(j, carry):
        block = 2 * j
        start(block + 1, 1)
        wait(0)
        accumulate(block, 0)
        start((block + 2) % n_blocks, 0)
        wait(1)
        accumulate(block + 1, 1)
        return carry

    lax.fori_loop(0, n_blocks // 2, pair, 0)

    @pl.when(step == pl.num_programs(0) - 1)
    def _():
        wait(0)

    x2 = x1_ref[...] + gate2_ref[0] * o_ref[...]
    o_ref[...] = _rms(x2) * fg_ref[...]


def _peer_dense(h2_bf16, key1, key2, gate, x1, gate2, final_g, u_t, v, tokens_per_batch):
    n, d = x1.shape
    n_experts = v.shape[0]
    tm, eb = PEER_ROW_TILE, PEER_EXPERT_TILE
    assert tokens_per_batch % tm == 0 and n_experts % eb == 0 and eb % PEER_N_KEYS == 0
    tiles_per_batch = tokens_per_batch // tm
    n_blocks = n_experts // eb
    assert n_blocks % 2 == 0
    row = lambda w: pl.BlockSpec((tm, w), lambda i: (i, 0))
    once_per_tile = lambda w: pl.BlockSpec((tm, w), lambda i: (i, 0), pipeline_mode=pl.Buffered(1))
    return pl.pallas_call(
        functools.partial(_peer_dense_body, tm=tm, blocks_per_step=eb // PEER_N_KEYS, n_blocks=n_blocks),
        grid=(n // tm,),
        in_specs=[once_per_tile(d), row(N_SEL), row(N_SEL), row(N_SEL), once_per_tile(d),
                  pl.BlockSpec((1, 1, d), lambda i: (i // tiles_per_batch, 0, 0)),
                  pl.BlockSpec((1, d), lambda i: (0, 0)),
                  pl.BlockSpec(memory_space=pl.ANY),
                  pl.BlockSpec(memory_space=pl.ANY)],
        out_specs=row(d),
        out_shape=jax.ShapeDtypeStruct((n, d), F32),
        scratch_shapes=[pltpu.VMEM((PEER_N_KEYS * (tm + SUBLANES), PEER_N_KEYS), F32),
                        pltpu.VMEM((2, d, eb), BF16), pltpu.VMEM((2, eb, d), BF16),
                        pltpu.SemaphoreType.DMA((2, 2))],
        compiler_params=_params(1),
        name="peer_dense",
    )(h2_bf16, key1, key2, gate, x1, gate2, final_g,
      u_t.reshape(d, n_blocks, eb).transpose(1, 0, 2), v.reshape(n_blocks, eb, d))


def kernel(x, c, ctx, c_ctx, ada_w, ada_b, norm_mix_g, w_in, hgrn_lb_logits, hgrn_norm_g, gla_gk_w, gla_gk_b, gla_norm_g, w_branch_hgrn, w_branch_gla, w_out, norm_ffn_g, peer_wq, peer_k1, peer_k2, peer_u, peer_v, final_g):
    bsz, seq, d = x.shape
    t_ctx = ctx.shape[1]
    depth = ada_w.shape[0]
    assert depth == 1 and d == D_MODEL and w_in.shape[-1] == D_IN
    l = 0
    row2 = lambda a: a.reshape(1, -1)

    lower = jnp.cumsum(jax.nn.softmax(hgrn_lb_logits, axis=0), axis=0)[l]
    lower = jnp.concatenate([lower, jnp.zeros((SUBLANES - 2, HG_KEY), F32)], axis=0)
    cvec = jnp.concatenate([c, c_ctx[None], jnp.zeros((SUBLANES - 1, d), F32)], axis=0)
    mod_all = _adaln(cvec, ada_w[l], ada_b[l])
    mod = [m.reshape(bsz, 1, d) for m in jnp.split(mod_all[:bsz], N_MOD, axis=-1)]
    mod_c = [jnp.broadcast_to(m.reshape(1, 1, d), (bsz, 1, d)) for m in jnp.split(mod_all[bsz:bsz + 1], N_MOD, axis=-1)]
    w_in_b = w_in[l].astype(BF16)
    zr = jnp.zeros((GLA_GATE_RANK, GLA_KEY), F32)
    wrank = jnp.concatenate([jnp.concatenate([gla_gk_w[l, 0], zr], axis=1),
                             jnp.concatenate([zr, gla_gk_w[l, 1]], axis=1)], axis=0).astype(BF16)
    brank = gla_gk_b[l].reshape(1, 2 * GLA_KEY)

    lat = _inproj(x, mod[0], mod[1], row2(norm_mix_g[l]), w_in_b, wrank, brank)
    cx = _inproj(ctx, mod_c[0], mod_c[1], row2(norm_mix_g[l]), w_in_b, wrank, brank)
    (hq, zf, zb, hi, hgate, gq, gk, gv, gg, glf, glb, m_hg, m_gla) = lat
    (_, czf, czb, chi, _, _, cgk, cgv, _, cglf, cglb, _, _) = cx

    y_hg = _scan_call(
        _hgrn_scan_body, HG_HEADS, HG_HEAD_V,
        [(hq, HEAD_K), (zf, HEAD_K), (zb, HEAD_K), (hi, HG_HEAD_V), (hgate, HG_HEAD_V)],
        [(czf, HEAD_K), (czb, HEAD_K), (chi, HG_HEAD_V)],
        [(lower, (SUBLANES, HEAD_K), lambda b, h: (0, h)), (row2(hgrn_norm_g[l]), (1, HG_HEAD_V), lambda b, h: (0, 0))],
        bsz, seq, t_ctx, "hgrn_scan",
        extra_scratch=[pltpu.VMEM((seq, HEAD_K), F32)] * 4 + [pltpu.VMEM((t_ctx, HEAD_K), F32)] * 4)
    y_gla = _scan_call(
        _gla_scan_body, GLA_HEADS, GLA_HEAD_V,
        [(gq, HEAD_K), (gk, HEAD_K), (glf, HEAD_K), (glb, HEAD_K), (gv, GLA_HEAD_V), (gg, GLA_HEAD_V)],
        [(cgk, HEAD_K), (cglf, HEAD_K), (cglb, HEAD_K), (cgv, GLA_HEAD_V)],
        [(row2(gla_norm_g[l]), (1, GLA_HEAD_V), lambda b, h: (0, 0))],
        bsz, seq, t_ctx, "gla_scan")

    x1, h2, scores_t = _mixer_out(
        y_hg, y_gla, m_hg, m_gla, x, mod[2], mod[3], mod[4], row2(norm_ffn_g[l]),
        w_branch_hgrn[l].astype(BF16), w_branch_gla[l].astype(BF16), w_out[l].astype(BF16),
        peer_wq[l].astype(BF16), peer_k1[l].astype(BF16), peer_k2[l].astype(BF16))

    key1, key2, gate = _route(scores_t)
    n = bsz * seq
    out = _peer_dense(h2.reshape(n, d), key1, key2, gate, x1.reshape(n, d), mod[5], row2(final_g),
                      peer_u[l].astype(BF16).T, peer_v[l].astype(BF16), seq)
    return out.reshape(bsz, seq, d)
```
